```python
import jax, jax.numpy as jnp
from jax import lax
import numpy as np

D_MODEL = 2048
BATCH = 1
SEQ = 8192
DEPTH = 1
DEC_BATCH = 8
DEC_SEQ = 2048
PAST_LEN = 128

N_META = 16
GRID_W = 64
WIN_H = 8
WIN_W = 16
D_ATTN = D_MODEL // 2
HD_ATTN = 128
H_ATTN = D_ATTN // HD_ATTN
D_RWKV = D_MODEL - D_ATTN
HD_RWKV = 64
H_RWKV = D_RWKV // HD_RWKV
R_DECAY = 96
R_ICLR = 96
R_GATE = 256
N_SHIFT = 3 * D_RWKV + R_DECAY + R_ICLR
N_IN = 3 * D_ATTN + N_SHIFT + R_GATE
D_FF = ((8 * D_MODEL // 3 + 255) // 256) * 256
DEEPNORM_ALPHA = (2 * DEPTH) ** 0.25
DEEPNORM_BETA = (8 * DEPTH) ** -0.25
LN_EPS = 1e-5
GN_EPS = 64e-5
NEG_INF = -1e30

kernel_name = "hymba_natten2d_rwkv7bi_deepnorm_encoder"


def _layer_norm(x, g, b):
    xf = x.astype(jnp.float32)
    mu = jnp.mean(xf, axis=-1, keepdims=True)
    xc = xf - mu
    var = jnp.mean(xc * xc, axis=-1, keepdims=True)
    return (xc * lax.rsqrt(var + LN_EPS) * g + b).astype(x.dtype)


def _neighbourhood_attention(q, k, v, rpb):
    B, L, _ = q.shape
    T = L - N_META
    rows = T // GRID_W
    kh = min(WIN_H, rows)

    def heads(t):
        return t.reshape(B, L, H_ATTN, HD_ATTN).transpose(0, 2, 1, 3)

    qh = heads(q) * (HD_ATTN ** -0.5)
    kh_, vh = heads(k), heads(v)
    qm, km, vm = qh[:, :, :N_META], kh_[:, :, :N_META], vh[:, :, :N_META]
    qg = qh[:, :, N_META:].reshape(B, H_ATTN, rows, GRID_W, HD_ATTN)
    kg = kh_[:, :, N_META:].reshape(B, H_ATTN, rows, GRID_W, HD_ATTN)
    vg = vh[:, :, N_META:].reshape(B, H_ATTN, rows, GRID_W, HD_ATTN)

    s_mm = jnp.einsum('bhmd,bhnd->bhmn', qm, km).astype(jnp.float32)
    o_meta = jnp.einsum('bhmn,bhnd->bhmd', jax.nn.softmax(s_mm, axis=-1).astype(vm.dtype), vm)

    qc = np.arange(GRID_W)
    c0 = np.clip(qc - WIN_W // 2, 0, GRID_W - WIN_W)
    kc = np.arange(GRID_W)
    colmask = (kc[None, :] >= c0[:, None]) & (kc[None, :] < c0[:, None] + WIN_W)
    dc = np.clip(kc[None, :] - qc[:, None], -(WIN_W - 1), WIN_W - 1) + (WIN_W - 1)
    rpb_c = rpb[:, :, dc]
    colmask_j = jnp.asarray(colmask)[:, None, :]

    def row_block(r):
        r0 = jnp.clip(r - kh // 2, 0, rows - kh)
        kb = lax.dynamic_slice_in_dim(kg, r0, kh, axis=2)
        vb = lax.dynamic_slice_in_dim(vg, r0, kh, axis=2)
        qr = lax.dynamic_index_in_dim(qg, r, axis=2, keepdims=False)
        s = jnp.einsum('bhqd,bhjkd->bhqjk', qr, kb).astype(jnp.float32)
        dr = r0 + jnp.arange(kh) - r + (WIN_H - 1)
        bias = jnp.take(rpb_c, dr, axis=1).transpose(0, 2, 1, 3)
        s = jnp.where(colmask_j, s + bias[None].astype(jnp.float32), NEG_INF)
        s_m = jnp.einsum('bhqd,bhmd->bhqm', qr, km).astype(jnp.float32)
        p = jax.nn.softmax(jnp.concatenate([s.reshape(B, H_ATTN, GRID_W, kh * GRID_W), s_m], axis=-1), axis=-1)
        p = p.astype(vb.dtype)
        pb = p[..., :kh * GRID_W].reshape(B, H_ATTN, GRID_W, kh, GRID_W)
        pm = p[..., kh * GRID_W:]
        return jnp.einsum('bhqjk,bhjkd->bhqd', pb, vb) + jnp.einsum('bhqm,bhmd->bhqd', pm, vm)

    og = lax.map(row_block, jnp.arange(rows))
    og = og.transpose(1, 2, 0, 3, 4).reshape(B, H_ATTN, T, HD_ATTN)
    o = jnp.concatenate([o_meta, og], axis=2)
    return o.transpose(0, 2, 1, 3).reshape(B, L, D_ATTN)


def _wkv7_scan(r, decay, k, v, kk, a, reverse):
    B, L, H, N = r.shape

    def step(S, inp):
        r_t, w_t, k_t, v_t, kk_t, a_t = inp
        sa = jnp.einsum('bhvk,bhk->bhv', S, -kk_t)
        S = (S * w_t[:, :, None, :] + sa[..., None] * (kk_t * a_t)[:, :, None, :]
             + v_t[..., None] * k_t[:, :, None, :])
        return S, jnp.einsum('bhvk,bhk->bhv', S, r_t)

    xs = tuple(jnp.moveaxis(t, 1, 0) for t in (r, decay, k, v, kk, a))
    S0 = jnp.zeros((B, H, N, N), jnp.float32)
    _, ys = lax.scan(step, S0, xs, reverse=reverse)
    return jnp.moveaxis(ys, 0, 1)


def _rwkv7_direction(p, mu, w0, w2, a0, a2, k_k, k_a, r_k, reverse):
    B, L, _ = p.shape
    if reverse:
        nb = jnp.pad(p[:, 1:], ((0, 0), (0, 1), (0, 0)))
    else:
        nb = jnp.pad(p[:, :-1], ((0, 0), (1, 0), (0, 0)))
    f = (p + (nb - p) * mu).astype(jnp.float32)
    r, k, v, wd, ad = jnp.split(f, [D_RWKV, 2 * D_RWKV, 3 * D_RWKV, 3 * D_RWKV + R_DECAY], axis=-1)
    w = -jax.nn.softplus(-(w0 + jnp.tanh(wd) @ w2)) - 0.5
    decay = jnp.exp(-jnp.exp(w))
    a = jax.nn.sigmoid(a0 + ad @ a2)
    hs = lambda t: t.reshape(B, L, H_RWKV, HD_RWKV)
    kk = hs(k * k_k)
    kk = kk / jnp.maximum(jnp.sqrt(jnp.sum(kk * kk, axis=-1, keepdims=True)), 1e-12)
    k = k * (1.0 + (a - 1.0) * k_a)
    r, k, v, decay, a = hs(r), hs(k), hs(v), hs(decay), hs(a)
    y = _wkv7_scan(r, decay, k, v, kk, a, reverse)
    bonus = jnp.sum(r * k * r_k, axis=-1, keepdims=True) * v
    return y, bonus


def _rwkv7_bidirectional(p, gd, mu, w0, w2, a0, a2, g2, k_k, k_a, r_k, lnx_g, lnx_b):
    B, L, _ = p.shape
    y_f, b_f = _rwkv7_direction(p, mu[0], w0[0], w2[0], a0[0], a2[0], k_k, k_a, r_k, False)
    y_b, b_b = _rwkv7_direction(p, mu[1], w0[1], w2[1], a0[1], a2[1], k_k, k_a, r_k, True)
    y = y_f + y_b
    m = jnp.mean(y, axis=-1, keepdims=True)
    yc = y - m
    var = jnp.mean(yc * yc, axis=-1, keepdims=True)
    y = (yc * lax.rsqrt(var + GN_EPS) * lnx_g.reshape(H_RWKV, HD_RWKV)
         + lnx_b.reshape(H_RWKV, HD_RWKV))
    y = (y + b_f + b_b).reshape(B, L, D_RWKV)
    g = jax.nn.sigmoid(gd.astype(jnp.float32)) @ g2
    return (y * g).astype(p.dtype)


def _conv_glu_ffn(h, w_in, conv_w, conv_b, w_out):
    u = h @ w_in
    gate, up = u[..., :D_FF], u[..., D_FF:]
    prev = jnp.pad(gate[:, :-1], ((0, 0), (1, 0), (0, 0)))
    nxt = jnp.pad(gate[:, 1:], ((0, 0), (0, 1), (0, 0)))
    gate = prev * conv_w[0] + gate * conv_w[1] + nxt * conv_w[2] + conv_b
    return (jax.nn.gelu(gate, approximate=False) * up) @ w_out


def _encode(x, meta_tokens, emb_ln_g, emb_ln_b, w_in, attn_rpb, rwkv_mu, rwkv_w0, rwkv_w2,
            rwkv_a0, rwkv_a2, rwkv_g2, rwkv_k_k, rwkv_k_a, rwkv_r_k, rwkv_lnx_g, rwkv_lnx_b,
            w_out, ln1_g, ln1_b, ffn_w_in, ffn_conv_w, ffn_conv_b, ffn_w_out, ln2_g, ln2_b):
    B = x.shape[0]
    meta = jnp.broadcast_to(meta_tokens.astype(x.dtype)[None], (B, N_META, D_MODEL))
    h = _layer_norm(jnp.concatenate([meta, x], axis=1), emb_ln_g, emb_ln_b)
    for l in range(DEPTH):
        proj = h @ w_in[l]
        qa = proj[..., :D_ATTN]
        ka = proj[..., D_ATTN:2 * D_ATTN]
        va = proj[..., 2 * D_ATTN:3 * D_ATTN]
        pr = proj[..., 3 * D_ATTN:3 * D_ATTN + N_SHIFT]
        gd = proj[..., 3 * D_ATTN + N_SHIFT:]
        o_attn = _neighbourhood_attention(qa, ka, va, attn_rpb[l]).astype(h.dtype)
        o_rwkv = _rwkv7_bidirectional(pr, gd, rwkv_mu[l], rwkv_w0[l], rwkv_w2[l], rwkv_a0[l],
                                      rwkv_a2[l], rwkv_g2[l], rwkv_k_k[l], rwkv_k_a[l],
                                      rwkv_r_k[l], rwkv_lnx_g[l], rwkv_lnx_b[l]).astype(h.dtype)
        mix = jnp.concatenate([o_attn, o_rwkv], axis=-1) @ w_out[l]
        h = _layer_norm(DEEPNORM_ALPHA * h + mix, ln1_g[l], ln1_b[l])
        ffn = _conv_glu_ffn(h, ffn_w_in[l], ffn_conv_w[l], ffn_conv_b[l], ffn_w_out[l])
        h = _layer_norm(DEEPNORM_ALPHA * h + ffn, ln2_g[l], ln2_b[l])
    return h[:, N_META:]


def setup_inputs(seed: int = 0) -> dict:
    key = jax.random.key(seed)
    ks = jax.random.split(key, 32)
    nrm = lambda k, shape, s: jax.random.normal(k, shape, jnp.float32) * s
    L = DEPTH
    return {
        "x_prompt": nrm(ks[0], (BATCH, SEQ, D_MODEL), 1.0),
        "x_sample": nrm(ks[1], (DEC_BATCH, DEC_SEQ, D_MODEL), 1.0),
        "meta_tokens": nrm(ks[2], (N_META, D_MODEL), 1.0),
        "emb_ln_g": 1.0 + nrm(ks[3], (D_MODEL,), 0.05),
        "emb_ln_b": nrm(ks[4], (D_MODEL,), 0.01),
        "w_in": nrm(ks[5], (L, D_MODEL, N_IN), D_MODEL ** -0.5),
        "attn_rpb": nrm(ks[6], (L, H_ATTN, 2 * WIN_H - 1, 2 * WIN_W - 1), 0.1),
        "rwkv_mu": jax.random.uniform(ks[7], (L, 2, N_SHIFT), jnp.float32),
        "rwkv_w0": jax.random.uniform(ks[8], (L, 2, D_RWKV), jnp.float32, minval=-6.0, maxval=-1.0),
        "rwkv_w2": nrm(ks[9], (L, 2, R_DECAY, D_RWKV), 0.5 * R_DECAY ** -0.5),
        "rwkv_a0": nrm(ks[10], (L, 2, D_RWKV), 0.1),
        "rwkv_a2": nrm(ks[11], (L, 2, R_ICLR, D_RWKV), R_ICLR ** -0.5),
        "rwkv_g2": nrm(ks[12], (L, R_GATE, D_RWKV), R_GATE ** -0.5),
        "rwkv_k_k": 0.85 + nrm(ks[13], (L, D_RWKV), 0.05),
        "rwkv_k_a": 1.0 + nrm(ks[14], (L, D_RWKV), 0.05),
        "rwkv_r_k": nrm(ks[15], (L, H_RWKV, HD_RWKV), 0.1),
        "rwkv_lnx_g": 1.0 + nrm(ks[16], (L, D_RWKV), 0.05),
        "rwkv_lnx_b": nrm(ks[17], (L, D_RWKV), 0.01),
        "w_out": nrm(ks[18], (L, D_MODEL, D_MODEL), D_MODEL ** -0.5 * DEEPNORM_BETA),
        "ln1_g": 1.0 + nrm(ks[19], (L, D_MODEL), 0.05),
        "ln1_b": nrm(ks[20], (L, D_MODEL), 0.01),
        "ffn_w_in": nrm(ks[21], (L, D_MODEL, 2 * D_FF), D_MODEL ** -0.5),
        "ffn_conv_w": nrm(ks[22], (L, 3, D_FF), 3.0 ** -0.5),
        "ffn_conv_b": nrm(ks[23], (L, D_FF), 0.01),
        "ffn_w_out": nrm(ks[24], (L, D_FF, D_MODEL), D_FF ** -0.5 * DEEPNORM_BETA),
        "ln2_g": 1.0 + nrm(ks[25], (L, D_MODEL), 0.05),
        "ln2_b": nrm(ks[26], (L, D_MODEL), 0.01),
    }


def reference(x_prompt, x_sample, meta_tokens, emb_ln_g, emb_ln_b, w_in, attn_rpb, rwkv_mu,
              rwkv_w0, rwkv_w2, rwkv_a0, rwkv_a2, rwkv_g2, rwkv_k_k, rwkv_k_a, rwkv_r_k,
              rwkv_lnx_g, rwkv_lnx_b, w_out, ln1_g, ln1_b, ffn_w_in, ffn_conv_w, ffn_conv_b,
              ffn_w_out, ln2_g, ln2_b):
    params = (meta_tokens, emb_ln_g, emb_ln_b, w_in, attn_rpb, rwkv_mu, rwkv_w0, rwkv_w2,
              rwkv_a0, rwkv_a2, rwkv_g2, rwkv_k_k, rwkv_k_a, rwkv_r_k, rwkv_lnx_g, rwkv_lnx_b,
              w_out, ln1_g, ln1_b, ffn_w_in, ffn_conv_w, ffn_conv_b, ffn_w_out, ln2_g, ln2_b)
    y_prompt = _encode(x_prompt, *params)
    y_sample = _encode(x_sample, *params)
    return (y_prompt, y_sample)
```

```python
import functools

import numpy as np
import jax
import jax.numpy as jnp
from jax import lax
from jax.experimental import pallas as pl
from jax.experimental.pallas import tpu as pltpu

D_MODEL = 2048
N_META = 16
GRID_W = 64
WIN_H = 8
WIN_W = 16
D_ATTN = 1024
HD_ATTN = 128
H_ATTN = D_ATTN // HD_ATTN
D_RWKV = 1024
HD_RWKV = 64
R_DECAY = 96
R_ICLR = 96
R_GATE = 256
D_FF = 5632
DEEPNORM_ALPHA = 2.0 ** 0.25
LN_EPS = 1e-5
GN_EPS = 64e-5
NEG_INF = -1e30

LANE = 128
CHUNK = 64
PAD_ROWS = CHUNK - N_META
R_LOW = 128
N_SHIFT_P = 3 * D_RWKV + 2 * R_LOW
N_RW = N_SHIFT_P + R_GATE
N_QKV = 3 * D_ATTN
N_IN_P = N_QKV + N_RW
N_PAIR = D_RWKV // LANE
QB_ROWS = 8
QB_TOK = QB_ROWS * GRID_W
KV_BLK = 256
WIN_TOK = 4 * KV_BLK
TM = 512
TN_IN = 512
FC = 512
HALO = 16
VMEM_LIMIT = 56 * 1024 * 1024

F32 = jnp.float32
BF16 = jnp.bfloat16


def _layer_norm(x, g, b):
    mu = jnp.mean(x, axis=-1, keepdims=True)
    xc = x - mu
    var = jnp.mean(xc * xc, axis=-1, keepdims=True)
    return xc * lax.rsqrt(var + LN_EPS) * g + b


def _dot(a, b):
    return jnp.dot(a, b, preferred_element_type=F32)


def _dot_nt(a, b):
    return lax.dot_general(a, b, (((1,), (1,)), ((), ())), preferred_element_type=F32)


def _dot_tn(a, b):
    return lax.dot_general(a, b, (((0,), (0,)), ((), ())), preferred_element_type=F32)


def _inproj_kernel(x_ref, g_ref, b_ref, w_ref, cs_ref, oa_ref, or_ref, h_scr, *, n_qkv_tiles):
    j = pl.program_id(1)

    @pl.when(j == 0)
    def _():
        h_scr[...] = _layer_norm(x_ref[...], g_ref[...], b_ref[...]).astype(BF16)

    acc = _dot(h_scr[...], w_ref[...]) * cs_ref[...]

    @pl.when(j < n_qkv_tiles)
    def _():
        oa_ref[...] = acc.astype(BF16)

    @pl.when(j >= n_qkv_tiles)
    def _():
        or_ref[...] = acc


def _inproj(x, ln_g, ln_b, w_in_p, col_scale, tm):
    rows = x.shape[0]
    nq = N_QKV // TN_IN
    nt = N_IN_P // TN_IN
    return pl.pallas_call(
        functools.partial(_inproj_kernel, n_qkv_tiles=nq),
        grid=(rows // tm, nt),
        in_specs=[
            pl.BlockSpec((tm, D_MODEL), lambda i, j: (i, 0)),
            pl.BlockSpec((1, D_MODEL), lambda i, j: (0, 0)),
            pl.BlockSpec((1, D_MODEL), lambda i, j: (0, 0)),
            pl.BlockSpec((D_MODEL, TN_IN), lambda i, j: (0, j)),
            pl.BlockSpec((1, TN_IN), lambda i, j: (0, j)),
        ],
        out_specs=[
            pl.BlockSpec((tm, TN_IN), lambda i, j: (i, jnp.minimum(j, nq - 1))),
            pl.BlockSpec((tm, TN_IN), lambda i, j: (i, jnp.maximum(j - nq, 0))),
        ],
        out_shape=[
            jax.ShapeDtypeStruct((rows, N_QKV), BF16),
            jax.ShapeDtypeStruct((rows, N_RW), F32),
        ],
        scratch_shapes=[pltpu.VMEM((tm, D_MODEL), BF16)],
        compiler_params=pltpu.CompilerParams(
            dimension_semantics=("arbitrary", "arbitrary"), vmem_limit_bytes=VMEM_LIMIT),
        name="inproj",
    )(x, ln_g, ln_b, w_in_p, col_scale)


def _attn_kernel(qblk_s, kvblk_s, var_s, mseq_s, q_ref, k0, k1, k2, k3, v0, v1, v2, v3,
                 mk_ref, mv_ref, bias_ref, o_ref):
    del qblk_s, kvblk_s, var_s, mseq_s
    q = q_ref[...]
    ks = (k0, k1, k2, k3)
    vs = (v0, v1, v2, v3)
    s = [_dot_nt(q, ks[c][...]) + bias_ref[0, 0, :, c * KV_BLK:(c + 1) * KV_BLK] for c in range(4)]
    km = mk_ref[PAD_ROWS:, :]
    vm = mv_ref[PAD_ROWS:, :]
    sm = _dot_nt(q, km)
    m = jnp.max(sm, axis=-1, keepdims=True)
    for c in range(4):
        m = jnp.maximum(m, jnp.max(s[c], axis=-1, keepdims=True))
    pm = jnp.exp(sm - m)
    l = jnp.sum(pm, axis=-1, keepdims=True)
    o = _dot(pm.astype(BF16), vm)
    for c in range(4):
        p = jnp.exp(s[c] - m)
        l = l + jnp.sum(p, axis=-1, keepdims=True)
        o = o + _dot(p.astype(BF16), vs[c][...])
    o_ref[...] = (o / l).astype(BF16)


def _attention(g_qkv, m_qkv, bias, tabs):
    qblk, kvblk, var, mseq = tabs
    nsteps = qblk.shape[0]
    rows = g_qkv.shape[0]

    def kv_spec(col0, c):
        return pl.BlockSpec((KV_BLK, HD_ATTN),
                            lambda h, t, qb, kb, vr, ms: (kb[t] + c, col0 + h))

    in_specs = [pl.BlockSpec((QB_TOK, HD_ATTN), lambda h, t, qb, kb, vr, ms: (qb[t], h))]
    in_specs += [kv_spec(H_ATTN, c) for c in range(4)]
    in_specs += [kv_spec(2 * H_ATTN, c) for c in range(4)]
    in_specs += [
        pl.BlockSpec((CHUNK, HD_ATTN), lambda h, t, qb, kb, vr, ms: (ms[t], H_ATTN + h)),
        pl.BlockSpec((CHUNK, HD_ATTN), lambda h, t, qb, kb, vr, ms: (ms[t], 2 * H_ATTN + h)),
        pl.BlockSpec((1, 1, QB_TOK, WIN_TOK), lambda h, t, qb, kb, vr, ms: (vr[t], h, 0, 0)),
    ]
    grid_spec = pltpu.PrefetchScalarGridSpec(
        num_scalar_prefetch=4,
        grid=(H_ATTN, nsteps),
        in_specs=in_specs,
        out_specs=pl.BlockSpec((QB_TOK, HD_ATTN), lambda h, t, qb, kb, vr, ms: (qb[t], h)),
    )
    return pl.pallas_call(
        _attn_kernel,
        grid_spec=grid_spec,
        out_shape=jax.ShapeDtypeStruct((rows, D_ATTN), BF16),
        compiler_params=pltpu.CompilerParams(
            dimension_semantics=("arbitrary", "arbitrary"), vmem_limit_bytes=VMEM_LIMIT),
        name="nbr_attention",
    )(qblk, kvblk, var, mseq, g_qkv, *([g_qkv] * 8), m_qkv, m_qkv, bias)


def _meta_attn_kernel(q_ref, k_ref, v_ref, o_ref):
    q = q_ref[...]
    km = k_ref[PAD_ROWS:, :]
    vm = v_ref[PAD_ROWS:, :]
    s = _dot_nt(q, km)
    m = jnp.max(s, axis=-1, keepdims=True)
    p = jnp.exp(s - m)
    l = jnp.sum(p, axis=-1, keepdims=True)
    o_ref[...] = (_dot(p.astype(BF16), vm) / l).astype(BF16)


def _meta_attention(m_qkv):
    nseq = m_qkv.shape[0] // CHUNK
    return pl.pallas_call(
        _meta_attn_kernel,
        grid=(nseq, H_ATTN),
        in_specs=[
            pl.BlockSpec((CHUNK, HD_ATTN), lambda s, h: (s, h)),
            pl.BlockSpec((CHUNK, HD_ATTN), lambda s, h: (s, H_ATTN + h)),
            pl.BlockSpec((CHUNK, HD_ATTN), lambda s, h: (s, 2 * H_ATTN + h)),
        ],
        out_specs=pl.BlockSpec((CHUNK, HD_ATTN), lambda s, h: (s, h)),
        out_shape=jax.ShapeDtypeStruct((m_qkv.shape[0], D_ATTN), BF16),
        compiler_params=pltpu.CompilerParams(dimension_semantics=("arbitrary", "arbitrary")),
        name="meta_attention",
    )(m_qkv, m_qkv, m_qkv)


def _attn_bias_table(rpb):
    qc = np.arange(GRID_W)
    kc = np.arange(GRID_W)
    c0 = np.clip(qc - WIN_W // 2, 0, GRID_W - WIN_W)
    colmask = (kc[None, :] >= c0[:, None]) & (kc[None, :] < c0[:, None] + WIN_W)
    dc = np.clip(kc[None, :] - qc[:, None], -(WIN_W - 1), WIN_W - 1) + (WIN_W - 1)
    i = np.arange(QB_ROWS)[:, None]
    j = np.arange(2 * QB_ROWS)[None, :]
    tabs = []
    for variant in range(3):
        if variant == 0:
            krel = j + 0 * i
            r0rel = np.maximum(i - WIN_H // 2, 0) + 0 * j
        elif variant == 1:
            krel = j - WIN_H // 2 + 0 * i
            r0rel = i - WIN_H // 2 + 0 * j
        else:
            krel = j - WIN_H + 0 * i
            r0rel = np.minimum(i - WIN_H // 2, 0) + 0 * j
        rowvalid = (krel >= r0rel) & (krel < r0rel + WIN_H)
        dr = np.clip(krel - i + (WIN_H - 1), 0, 2 * WIN_H - 2)
        b = rpb[:, dr[:, None, :, None], dc[None, :, None, :]]
        valid = rowvalid[:, None, :, None] & colmask[None, :, None, :]
        b = jnp.where(jnp.asarray(valid)[None], b, NEG_INF)
        tabs.append(b.reshape(H_ATTN, QB_TOK, WIN_TOK))
    return jnp.stack(tabs).astype(F32)


def _seg_sum(x, e2):
    stacked = jnp.concatenate([x[:, j * LANE:(j + 1) * LANE] for j in range(N_PAIR)], axis=0)
    res = jnp.dot(stacked, e2, preferred_element_type=F32, precision=lax.Precision.HIGHEST)
    return jnp.concatenate([res[j * CHUNK:(j + 1) * CHUNK] for j in range(N_PAIR)], axis=1)


def _rwkv_kernel(gblk_s, mblk_s, flag_s, *refs, reverse):
    del gblk_s, mblk_s
    if reverse:
        (gp_ref, mp_ref, mu_ref, w0_ref, w2_ref, a0_ref, a2_ref, kk_ref, ka_ref, rk_ref,
         gyb_ref, myb_ref, g2_ref, lg_ref, lb_ref, og_ref, om_ref, state, carry) = refs
    else:
        (gp_ref, mp_ref, mu_ref, w0_ref, w2_ref, a0_ref, a2_ref, kk_ref, ka_ref, rk_ref,
         og_ref, om_ref, state, carry) = refs
    step = pl.program_id(0)
    fl = flag_s[step]
    is_first = (fl & 1) == 1
    is_meta = (fl & 2) == 2

    @pl.when(is_first)
    def _():
        state[...] = jnp.zeros_like(state)
        carry[...] = jnp.zeros_like(carry)

    praw = jnp.where(is_meta, mp_ref[...], gp_ref[...])
    row = lax.broadcasted_iota(jnp.int32, (CHUNK, 1), 0)
    pad = jnp.logical_and(is_meta, row < PAD_ROWS)
    p_sh = jnp.where(pad, 0.0, praw[:, :N_SHIFT_P])
    gd = praw[:, N_SHIFT_P:]

    if reverse:
        rolled = pltpu.roll(p_sh, CHUNK - 1, axis=0)
        nb = jnp.where(row == CHUNK - 1, carry[0:1, :], rolled)
        carry[...] = p_sh[0:8, :]
    else:
        rolled = pltpu.roll(p_sh, 1, axis=0)
        nb = jnp.where(row == 0, carry[7:8, :], rolled)
        carry[...] = p_sh[CHUNK - 8:CHUNK, :]
    f = p_sh + (nb - p_sh) * mu_ref[...]
    r = f[:, 0:D_RWKV]
    k = f[:, D_RWKV:2 * D_RWKV]
    v = f[:, 2 * D_RWKV:3 * D_RWKV]
    wd = f[:, 3 * D_RWKV:3 * D_RWKV + R_LOW]
    ad = f[:, 3 * D_RWKV + R_LOW:3 * D_RWKV + 2 * R_LOW]

    ii = lax.broadcasted_iota(jnp.int32, (LANE, LANE), 0)
    jj = lax.broadcasted_iota(jnp.int32, (LANE, LANE), 1)
    same = (ii // CHUNK) == (jj // CHUNK)
    e2 = jnp.where(same, 1.0, 0.0).astype(F32)

    wl = w0_ref[...] + _dot(jnp.tanh(wd).astype(BF16), w2_ref[...])
    nwl = -wl
    softplus = jnp.maximum(nwl, 0.0) + jnp.log(1.0 + jnp.exp(-jnp.abs(nwl)))
    lw = -jnp.exp(-softplus - 0.5)
    a = 1.0 / (1.0 + jnp.exp(-(a0_ref[...] + _dot(ad.astype(BF16), a2_ref[...]))))
    kkr = k * kk_ref[...]
    kk = kkr / jnp.maximum(jnp.sqrt(_seg_sum(kkr * kkr, e2)), 1e-12)
    kmod = k * (1.0 + (a - 1.0) * ka_ref[...])
    bonus = _seg_sum(r * kmod * rk_ref[...], e2) * v

    ti = lax.broadcasted_iota(jnp.int32, (CHUNK, CHUNK), 0)
    tj = lax.broadcasted_iota(jnp.int32, (CHUNK, CHUNK), 1)
    tri = jnp.where((tj >= ti) if reverse else (tj <= ti), 1.0, 0.0).astype(F32)
    cum = jnp.dot(tri, lw, preferred_element_type=F32, precision=lax.Precision.HIGHEST)
    cum_end = cum[0:1, :] if reverse else cum[CHUNK - 1:CHUNK, :]
    e_pos = jnp.exp(cum)
    e_neg = jnp.exp(-cum)
    e_exc = jnp.exp(cum - lw)
    e_end = jnp.exp(cum_end - cum)
    bb = kk * a
    r_t = r * e_pos
    k_t = kmod * e_neg
    b_t = bb * e_neg
    a_t = -kk * e_exc
    k_h = kmod * e_end
    b_h = bb * e_end

    ti2 = ii % CHUNK
    tj2 = jj % CHUNK
    if reverse:
        strict = jnp.logical_and(same, tj2 > ti2)
        incl = jnp.logical_and(same, tj2 >= ti2)
    else:
        strict = jnp.logical_and(same, tj2 < ti2)
        incl = jnp.logical_and(same, tj2 <= ti2)
    eye = jnp.where(ii == jj, 1.0, 0.0).astype(F32)
    lane64 = lax.broadcasted_iota(jnp.int32, (CHUNK, LANE), 1) < HD_RWKV

    def stack_masked(x):
        return jnp.concatenate([jnp.where(lane64, x, 0.0), jnp.where(lane64, 0.0, x)], axis=0)

    def stack_dup(x):
        return jnp.concatenate([x, x], axis=0)

    ys = []
    for j in range(N_PAIR):
        sl = slice(j * LANE, (j + 1) * LANE)
        a_s = stack_masked(a_t[:, sl])
        r_s = stack_masked(r_t[:, sl])
        bh_s = stack_masked(b_h[:, sl])
        kh_s = stack_masked(k_h[:, sl])
        v_s = stack_masked(v[:, sl])
        b_d = stack_dup(b_t[:, sl])
        k_d = stack_dup(k_t[:, sl])
        v_d = stack_dup(v[:, sl])
        m1 = _dot_nt(jnp.concatenate([a_s, r_s], axis=0).astype(BF16),
                     jnp.concatenate([b_d, k_d], axis=0).astype(BF16))
        n_ab = jnp.where(strict, m1[:LANE, :LANE], 0.0)
        a_ak = jnp.where(strict, m1[:LANE, LANE:], 0.0)
        a_rb = jnp.where(incl, m1[LANE:, :LANE], 0.0)
        a_rk = jnp.where(incl, m1[LANE:, LANE:], 0.0)
        tinv = eye + n_ab
        npow = n_ab
        for _ in range(5):
            npb = npow.astype(BF16)
            npow = _dot(npb, npb)
            tinv = tinv + _dot(tinv.astype(BF16), npow.astype(BF16))
        gy = _dot(jnp.concatenate([a_ak, a_rk], axis=0).astype(BF16), v_d.astype(BF16))
        g_s = jnp.where(same, gy[:LANE], 0.0)
        y0_s = jnp.where(same, gy[LANE:], 0.0)
        tz = _dot(tinv.astype(BF16), jnp.concatenate([a_s, g_s], axis=1).astype(BF16))
        w_s = tz[:, :LANE]
        u0_s = tz[:, LANE:]
        s_add = _dot_tn(kh_s.astype(BF16), v_s.astype(BF16))
        s_old = state[j]
        xr = _dot(jnp.concatenate([w_s, r_s], axis=0).astype(BF16), s_old.astype(BF16))
        u_s = xr[:LANE] + u0_s
        y_s = xr[LANE:] + _dot(a_rb.astype(BF16), u_s.astype(BF16)) + y0_s
        ys.append(y_s[:CHUNK] + y_s[CHUNK:])
        g_end = jnp.exp(jnp.transpose(jnp.broadcast_to(cum_end[:, sl], (LANE, LANE))))
        state[j] = g_end * s_old + _dot_tn(bh_s.astype(BF16), u_s.astype(BF16)) + s_add
    y = jnp.concatenate(ys, axis=1)

    if not reverse:
        res = jnp.concatenate([y, bonus], axis=1)
    else:
        yb = jnp.where(is_meta, myb_ref[...], gyb_ref[...])
        yy = y + yb[:, :D_RWKV]
        mean = _seg_sum(yy, e2) * (1.0 / HD_RWKV)
        yc = yy - mean
        var = _seg_sum(yc * yc, e2) * (1.0 / HD_RWKV)
        yn = yc * lax.rsqrt(var + GN_EPS) * lg_ref[...] + lb_ref[...]
        gate = _dot((1.0 / (1.0 + jnp.exp(-gd))).astype(BF16), g2_ref[...])
        res = ((yn + yb[:, D_RWKV:] + bonus) * gate).astype(BF16)

    @pl.when(is_meta)
    def _():
        om_ref[...] = res

    @pl.when(jnp.logical_not(is_meta))
    def _():
        og_ref[...] = res


def _rwkv_scan(g_prw, m_prw, params, tabs, reverse, prev=None):
    gblk, mblk, flags = tabs
    nsteps = gblk.shape[0]
    gmap = lambda t, gb, mb, fl: (gb[t], 0)
    mmap = lambda t, gb, mb, fl: (mb[t], 0)
    cmap = lambda t, gb, mb, fl: (0, 0)
    vec = lambda n: pl.BlockSpec((1, n), cmap)
    in_specs = [
        pl.BlockSpec((CHUNK, N_RW), gmap),
        pl.BlockSpec((CHUNK, N_RW), mmap),
        vec(N_SHIFT_P), vec(D_RWKV),
        pl.BlockSpec((R_LOW, D_RWKV), cmap), vec(D_RWKV),
        pl.BlockSpec((R_LOW, D_RWKV), cmap), vec(D_RWKV), vec(D_RWKV), vec(D_RWKV),
    ]
    args = [g_prw, m_prw, params["mu"], params["w0"], params["w2"], params["a0"], params["a2"],
            params["k_k"], params["k_a"], params["r_k"]]
    if reverse:
        g_yb, m_yb = prev
        in_specs += [
            pl.BlockSpec((CHUNK, 2 * D_RWKV), gmap),
            pl.BlockSpec((CHUNK, 2 * D_RWKV), mmap),
            pl.BlockSpec((R_GATE, D_RWKV), cmap), vec(D_RWKV), vec(D_RWKV),
        ]
        args += [g_yb, m_yb, params["g2"], params["lnx_g"], params["lnx_b"]]
        width, dt = D_RWKV, BF16
    else:
        width, dt = 2 * D_RWKV, F32
    grid_spec = pltpu.PrefetchScalarGridSpec(
        num_scalar_prefetch=3,
        grid=(nsteps,),
        in_specs=in_specs,
        out_specs=[pl.BlockSpec((CHUNK, width), gmap), pl.BlockSpec((CHUNK, width), mmap)],
        scratch_shapes=[pltpu.VMEM((N_PAIR, LANE, LANE), F32), pltpu.VMEM((8, N_SHIFT_P), F32)],
    )
    return pl.pallas_call(
        functools.partial(_rwkv_kernel, reverse=reverse),
        grid_spec=grid_spec,
        out_shape=[jax.ShapeDtypeStruct((g_prw.shape[0], width), dt),
                   jax.ShapeDtypeStruct((m_prw.shape[0], width), dt)],
        compiler_params=pltpu.CompilerParams(
            dimension_semantics=("arbitrary",), vmem_limit_bytes=VMEM_LIMIT),
        name="rwkv_bwd" if reverse else "rwkv_fwd",
    )(gblk, mblk, flags, *args)


def _outproj_kernel(oa_ref, or_ref, x_ref, eg_ref, eb_ref, wa_ref, wr_ref, g1_ref, b1_ref, h1_ref):
    h0 = _layer_norm(x_ref[...], eg_ref[...], eb_ref[...])
    mix = _dot(oa_ref[...], wa_ref[...]) + _dot(or_ref[...], wr_ref[...])
    h1_ref[...] = _layer_norm(DEEPNORM_ALPHA * h0 + mix, g1_ref[...], b1_ref[...])


def _outproj(o_attn, o_rwkv, x, eg, eb, w_out_bf, g1, b1, tm):
    rows = x.shape[0]
    cmap = lambda i: (0, 0)
    vec = pl.BlockSpec((1, D_MODEL), cmap)
    return pl.pallas_call(
        _outproj_kernel,
        grid=(rows // tm,),
        in_specs=[
            pl.BlockSpec((tm, D_ATTN), lambda i: (i, 0)),
            pl.BlockSpec((tm, D_RWKV), lambda i: (i, 0)),
            pl.BlockSpec((tm, D_MODEL), lambda i: (i, 0)),
            vec, vec,
            pl.BlockSpec((D_ATTN, D_MODEL), lambda i: (0, 0)),
            pl.BlockSpec((D_RWKV, D_MODEL), lambda i: (1, 0)),
            vec, vec,
        ],
        out_specs=pl.BlockSpec((tm, D_MODEL), lambda i: (i, 0)),
        out_shape=jax.ShapeDtypeStruct((rows, D_MODEL), F32),
        compiler_params=pltpu.CompilerParams(
            dimension_semantics=("arbitrary",), vmem_limit_bytes=VMEM_LIMIT),
        name="outproj_ln1",
    )(o_attn, o_rwkv, x, eg, eb, w_out_bf, w_out_bf, g1, b1)


def _ffn_kernel(pg_s, pm_s, nx_s, fl_s, h_ref, pg_ref, pm_ref, nx_ref, wg_ref, wu_ref, wo_ref,
                cw_ref, cb_ref, g2_ref, b2_ref, *refs, n_out, tiles_first):
    del pg_s, pm_s, nx_s
    outs = refs[:n_out]
    lhs, gate_s, acc = refs[n_out:]
    i = pl.program_id(0)
    c = pl.program_id(1)
    fl = fl_s[i]

    @pl.when(c == 0)
    def _():
        prev = jnp.where((fl & 1) == 1, pm_ref[...], pg_ref[...])
        nxt = jnp.where((fl & 2) == 2, 0.0, nx_ref[...])
        lhs[0:HALO, :] = prev.astype(BF16)
        lhs[HALO:HALO + TM, :] = h_ref[...].astype(BF16)
        lhs[HALO + TM:, :] = nxt.astype(BF16)
        acc[...] = jnp.zeros_like(acc)

    gate_s[...] = _dot(lhs[...], wg_ref[...])
    up = _dot(lhs[HALO:HALO + TM, :], wu_ref[...])
    gate = (gate_s[HALO - 1:HALO - 1 + TM, :] * cw_ref[0:1, :]
            + gate_s[HALO:HALO + TM, :] * cw_ref[1:2, :]
            + gate_s[HALO + 1:HALO + 1 + TM, :] * cw_ref[2:3, :] + cb_ref[...])
    act = 0.5 * gate * (1.0 + lax.erf(gate * 0.7071067811865476)) * up
    acc[...] += _dot(act.astype(BF16), wo_ref[...])

    @pl.when(c == pl.num_programs(1) - 1)
    def _():
        y = _layer_norm(DEEPNORM_ALPHA * h_ref[...] + acc[...], g2_ref[...], b2_ref[...])
        if n_out == 1:
            outs[0][...] = y
        else:
            @pl.when(i < tiles_first)
            def _():
                outs[0][...] = y

            @pl.when(i >= tiles_first)
            def _():
                outs[1][...] = y


def _ffn(g_h1, m_h1, w_in_bf, w_out_bf, conv_w, conv_b, g2, b2, tabs, out_rows):
    pg, pm, nx, fl = tabs
    ntiles = g_h1.shape[0] // TM
    nfc = D_FF // FC
    cmap = lambda i, c, *_: (0, 0)
    vec = pl.BlockSpec((1, D_MODEL), cmap)
    tiles_first = out_rows[0] // TM
    if len(out_rows) == 1:
        out_specs = [pl.BlockSpec((TM, D_MODEL), lambda i, c, *_: (i, 0))]
    else:
        out_specs = [
            pl.BlockSpec((TM, D_MODEL), lambda i, c, *_: (jnp.minimum(i, tiles_first - 1), 0)),
            pl.BlockSpec((TM, D_MODEL), lambda i, c, *_: (jnp.maximum(i - tiles_first, 0), 0)),
        ]
    grid_spec = pltpu.PrefetchScalarGridSpec(
        num_scalar_prefetch=4,
        grid=(ntiles, nfc),
        in_specs=[
            pl.BlockSpec((TM, D_MODEL), lambda i, c, *_: (i, 0)),
            pl.BlockSpec((HALO, D_MODEL), lambda i, c, pg, pm, nx, fl: (pg[i], 0)),
            pl.BlockSpec((HALO, D_MODEL), lambda i, c, pg, pm, nx, fl: (pm[i], 0)),
            pl.BlockSpec((HALO, D_MODEL), lambda i, c, pg, pm, nx, fl: (nx[i], 0)),
            pl.BlockSpec((D_MODEL, FC), lambda i, c, *_: (0, c)),
            pl.BlockSpec((D_MODEL, FC), lambda i, c, *_: (0, nfc + c)),
            pl.BlockSpec((FC, D_MODEL), lambda i, c, *_: (c, 0)),
            pl.BlockSpec((3, FC), lambda i, c, *_: (0, c)),
            pl.BlockSpec((1, FC), lambda i, c, *_: (0, c)),
            vec, vec,
        ],
        out_specs=out_specs,
        scratch_shapes=[
            pltpu.VMEM((TM + 2 * HALO, D_MODEL), BF16),
            pltpu.VMEM((TM + 2 * HALO, FC), F32),
            pltpu.VMEM((TM, D_MODEL), F32),
        ],
    )
    return pl.pallas_call(
        functools.partial(_ffn_kernel, n_out=len(out_rows), tiles_first=tiles_first),
        grid_spec=grid_spec,
        out_shape=[jax.ShapeDtypeStruct((r, D_MODEL), F32) for r in out_rows],
        compiler_params=pltpu.CompilerParams(
            dimension_semantics=("arbitrary", "arbitrary"), vmem_limit_bytes=VMEM_LIMIT),
        name="ffn_ln2",
    )(pg, pm, nx, fl, g_h1, g_h1, m_h1, g_h1, w_in_bf, w_in_bf, w_out_bf, conv_w, conv_b, g2, b2)


def _attn_tables(seq_lens):
    rec = []
    base = 0
    for s, t_len in enumerate(seq_lens):
        rows = t_len // GRID_W
        for qb in range(rows // QB_ROWS):
            w0 = min(max(qb * QB_ROWS - WIN_H // 2, 0), rows - 2 * QB_ROWS)
            variant = 0 if qb == 0 else (2 if qb == rows // QB_ROWS - 1 else 1)
            rec.append((variant, (base + qb * QB_TOK) // QB_TOK, (base + w0 * GRID_W) // KV_BLK, s))
        base += t_len
    rec.sort(key=lambda x: x[0])
    arr = np.asarray(rec, np.int32)
    return tuple(jnp.asarray(arr[:, c]) for c in (1, 2, 0, 3))


def _rwkv_tables(seq_lens, reverse):
    gblk, mblk, flags = [], [], []
    base = 0
    for s, t_len in enumerate(seq_lens):
        n = t_len // CHUNK
        first_blk = base // CHUNK
        steps = [(first_blk, 2)] + [(first_blk + c, 0) for c in range(n)]
        if reverse:
            steps = steps[::-1]
        for idx, (blk, fl) in enumerate(steps):
            gblk.append(blk)
            mblk.append(s)
            flags.append(fl | (1 if idx == 0 else 0))
        base += t_len
    return tuple(jnp.asarray(np.asarray(a, np.int32)) for a in (gblk, mblk, flags))


def _ffn_tables(seq_lens):
    pg, pm, nx, fl = [], [], [], []
    base = 0
    total = sum(seq_lens)
    for s, t_len in enumerate(seq_lens):
        for t in range(t_len // TM):
            r0 = base + t * TM
            first = t == 0
            last = t == t_len // TM - 1
            pg.append(max(r0 // HALO - 1, 0))
            pm.append((s * CHUNK + CHUNK - HALO) // HALO)
            nx.append(min((r0 + TM) // HALO, total // HALO - 1))
            fl.append((1 if first else 0) | (2 if last else 0))
        base += t_len
    return tuple(jnp.asarray(np.asarray(a, np.int32)) for a in (pg, pm, nx, fl))


def _pad_cols(w, n):
    return jnp.pad(w, ((0, 0), (0, n - w.shape[1])))


def _encode_all(xs, meta_tokens, emb_ln_g, emb_ln_b, w_in, attn_rpb, rwkv_mu, rwkv_w0, rwkv_w2,
                rwkv_a0, rwkv_a2, rwkv_g2, rwkv_k_k, rwkv_k_a, rwkv_r_k, rwkv_lnx_g, rwkv_lnx_b,
                w_out, ln1_g, ln1_b, ffn_w_in, ffn_conv_w, ffn_conv_b, ffn_w_out, ln2_g, ln2_b):
    seq_lens = []
    for x in xs:
        seq_lens += [x.shape[1]] * x.shape[0]
    nseq = len(seq_lens)
    x_g = jnp.concatenate([x.reshape(-1, D_MODEL) for x in xs], axis=0)
    meta_blk = jnp.concatenate([jnp.zeros((PAD_ROWS, D_MODEL), F32), meta_tokens.astype(F32)], axis=0)
    x_m = jnp.tile(meta_blk, (nseq, 1))

    o1 = N_QKV
    o2 = o1 + 3 * D_RWKV
    o3 = o2 + R_DECAY
    o4 = o3 + R_ICLR
    wl = w_in[0]
    w_in_p = jnp.concatenate(
        [wl[:, :o2], _pad_cols(wl[:, o2:o3], R_LOW), _pad_cols(wl[:, o3:o4], R_LOW), wl[:, o4:]],
        axis=1).astype(BF16)
    col_scale = jnp.concatenate(
        [jnp.full((1, D_ATTN), HD_ATTN ** -0.5, F32), jnp.ones((1, N_IN_P - D_ATTN), F32)], axis=1)
    eg = emb_ln_g.reshape(1, D_MODEL)
    eb = emb_ln_b.reshape(1, D_MODEL)
    mu = rwkv_mu[0]
    o5 = 3 * D_RWKV
    mu_p = jnp.concatenate(
        [mu[:, :o5], _pad_cols(mu[:, o5:o5 + R_DECAY], R_LOW), _pad_cols(mu[:, o5 + R_DECAY:], R_LOW)],
        axis=1)
    pad_rows = lambda w: jnp.pad(w, ((0, R_LOW - w.shape[0]), (0, 0))).astype(BF16)
    row = lambda v: v.reshape(1, -1)

    def dir_params(d):
        return dict(mu=mu_p[d:d + 1], w0=row(rwkv_w0[0, d]), w2=pad_rows(rwkv_w2[0, d]),
                    a0=row(rwkv_a0[0, d]), a2=pad_rows(rwkv_a2[0, d]),
                    k_k=row(rwkv_k_k[0]), k_a=row(rwkv_k_a[0]), r_k=row(rwkv_r_k[0]),
                    g2=rwkv_g2[0].astype(BF16), lnx_g=row(rwkv_lnx_g[0]), lnx_b=row(rwkv_lnx_b[0]))

    w_out_bf = w_out[0].astype(BF16)
    ffn_w_in_bf = ffn_w_in[0].astype(BF16)
    ffn_w_out_bf = ffn_w_out[0].astype(BF16)
    bias = _attn_bias_table(attn_rpb[0])

    g_qkv, g_prw = _inproj(x_g, eg, eb, w_in_p, col_scale, TM)
    m_qkv, m_prw = _inproj(x_m, eg, eb, w_in_p, col_scale, x_m.shape[0])

    g_oattn = _attention(g_qkv, m_qkv, bias, _attn_tables(seq_lens))
    m_oattn = _meta_attention(m_qkv)
    g_yb, m_yb = _rwkv_scan(g_prw, m_prw, dir_params(0), _rwkv_tables(seq_lens, False), False)
    g_orwkv, m_orwkv = _rwkv_scan(g_prw, m_prw, dir_params(1), _rwkv_tables(seq_lens, True), True,
                                  prev=(g_yb, m_yb))

    g1 = ln1_g.reshape(1, D_MODEL)
    b1 = ln1_b.reshape(1, D_MODEL)
    g_h1 = _outproj(g_oattn, g_orwkv, x_g, eg, eb, w_out_bf, g1, b1, TM)
    m_h1 = _outproj(m_oattn, m_orwkv, x_m, eg, eb, w_out_bf, g1, b1, x_m.shape[0])

    out_rows = [x.shape[0] * x.shape[1] for x in xs]
    ys = _ffn(g_h1, m_h1, ffn_w_in_bf, ffn_w_out_bf, ffn_conv_w[0], ffn_conv_b.reshape(1, D_FF),
              ln2_g.reshape(1, D_MODEL), ln2_b.reshape(1, D_MODEL), _ffn_tables(seq_lens), out_rows)
    return [y.reshape(x.shape) for y, x in zip(ys, xs)]


def kernel(x_prompt, x_sample, meta_tokens, emb_ln_g, emb_ln_b, w_in, attn_rpb, rwkv_mu, rwkv_w0, rwkv_w2, rwkv_a0, rwkv_a2, rwkv_g2, rwkv_k_k, rwkv_k_a, rwkv_r_k, rwkv_lnx_g, rwkv_lnx_b, w_out, ln1_g, ln1_b, ffn_w_in, ffn_conv_w, ffn_conv_b, ffn_w_out, ln2_g, ln2_b):
    y_prompt, y_sample = _encode_all(
        [x_prompt, x_sample], meta_tokens, emb_ln_g, emb_ln_b, w_in, attn_rpb, rwkv_mu, rwkv_w0,
        rwkv_w2, rwkv_a0, rwkv_a2, rwkv_g2, rwkv_k_k, rwkv_k_a, rwkv_r_k, rwkv_lnx_g, rwkv_lnx_b,
        w_out, ln1_g, ln1_b, ffn_w_in, ffn_conv_w, ffn_conv_b, ffn_w_out, ln2_g, ln2_b)
    return (y_prompt, y_sample)
```

```python
import functools

import numpy as np
import jax
import jax.numpy as jnp
from jax import lax
from jax.experimental import pallas as pl
from jax.experimental.pallas import tpu as pltpu

D_MODEL = 2048
N_META = 16
GRID_W = 64
WIN_H = 8
WIN_W = 16
D_ATTN = 1024
HD_ATTN = 128
H_ATTN = D_ATTN // HD_ATTN
D_RWKV = 1024
HD_RWKV = 64
R_DECAY = 96
R_ICLR = 96
R_GATE = 256
D_FF = 5632
DEEPNORM_ALPHA = 2.0 ** 0.25
LN_EPS = 1e-5
GN_EPS = 64e-5
NEG_INF = -1e30

LANE = 128
CHUNK = 64
PAD_ROWS = CHUNK - N_META
R_LOW = 128
N_SHIFT_P = 3 * D_RWKV + 2 * R_LOW
N_RW = N_SHIFT_P + R_GATE
N_QKV = 3 * D_ATTN
N_IN_P = N_QKV + N_RW
N_PAIR = D_RWKV // LANE
QB_ROWS = 8
QB_TOK = QB_ROWS * GRID_W
KV_BLK = 256
WIN_TOK = 4 * KV_BLK
TM = 512
TN_IN = 512
FC = 512
HALO = 16
VMEM_LIMIT = 56 * 1024 * 1024

F32 = jnp.float32
BF16 = jnp.bfloat16


def _layer_norm(x, g, b):
    mu = jnp.mean(x, axis=-1, keepdims=True)
    xc = x - mu
    var = jnp.mean(xc * xc, axis=-1, keepdims=True)
    return xc * lax.rsqrt(var + LN_EPS) * g + b


def _dot(a, b):
    return jnp.dot(a, b, preferred_element_type=F32)


def _dot_nt(a, b):
    return lax.dot_general(a, b, (((1,), (1,)), ((), ())), preferred_element_type=F32)


def _bmm(a, b):
    return jnp.einsum("bik,bkj->bij", a, b, preferred_element_type=F32)


def _bmm_nt(a, b):
    return jnp.einsum("bik,bjk->bij", a, b, preferred_element_type=F32)


def _bmm_tn(a, b):
    return jnp.einsum("bki,bkj->bij", a, b, preferred_element_type=F32)


def _inproj_kernel(x_ref, g_ref, b_ref, w_ref, cs_ref, oa_ref, or_ref, h_scr, *, n_qkv_tiles):
    j = pl.program_id(1)

    @pl.when(j == 0)
    def _():
        h_scr[...] = _layer_norm(x_ref[...], g_ref[...], b_ref[...]).astype(BF16)

    acc = _dot(h_scr[...], w_ref[...]) * cs_ref[...]

    @pl.when(j < n_qkv_tiles)
    def _():
        oa_ref[...] = acc.astype(BF16)

    @pl.when(j >= n_qkv_tiles)
    def _():
        or_ref[...] = acc


def _inproj(x, ln_g, ln_b, w_in_p, col_scale, tm):
    rows = x.shape[0]
    nq = N_QKV // TN_IN
    nt = N_IN_P // TN_IN
    return pl.pallas_call(
        functools.partial(_inproj_kernel, n_qkv_tiles=nq),
        grid=(rows // tm, nt),
        in_specs=[
            pl.BlockSpec((tm, D_MODEL), lambda i, j: (i, 0)),
            pl.BlockSpec((1, D_MODEL), lambda i, j: (0, 0)),
            pl.BlockSpec((1, D_MODEL), lambda i, j: (0, 0)),
            pl.BlockSpec((D_MODEL, TN_IN), lambda i, j: (0, j)),
            pl.BlockSpec((1, TN_IN), lambda i, j: (0, j)),
        ],
        out_specs=[
            pl.BlockSpec((tm, TN_IN), lambda i, j: (i, jnp.minimum(j, nq - 1))),
            pl.BlockSpec((tm, TN_IN), lambda i, j: (i, jnp.maximum(j - nq, 0))),
        ],
        out_shape=[
            jax.ShapeDtypeStruct((rows, N_QKV), BF16),
            jax.ShapeDtypeStruct((rows, N_RW), F32),
        ],
        scratch_shapes=[pltpu.VMEM((tm, D_MODEL), BF16)],
        compiler_params=pltpu.CompilerParams(
            dimension_semantics=("arbitrary", "arbitrary"), vmem_limit_bytes=VMEM_LIMIT),
        name="inproj",
    )(x, ln_g, ln_b, w_in_p, col_scale)


def _attn_kernel(qblk_s, kvblk_s, var_s, mseq_s, q_ref, k0, k1, k2, k3, v0, v1, v2, v3,
                 mk_ref, mv_ref, bias_ref, o_ref):
    del qblk_s, kvblk_s, var_s, mseq_s
    q = q_ref[...]
    ks = (k0, k1, k2, k3)
    vs = (v0, v1, v2, v3)
    s = [_dot_nt(q, ks[c][...]) + bias_ref[0, 0, :, c * KV_BLK:(c + 1) * KV_BLK] for c in range(4)]
    km = mk_ref[PAD_ROWS:, :]
    vm = mv_ref[PAD_ROWS:, :]
    sm = _dot_nt(q, km)
    m = jnp.max(sm, axis=-1, keepdims=True)
    for c in range(4):
        m = jnp.maximum(m, jnp.max(s[c], axis=-1, keepdims=True))
    pm = jnp.exp(sm - m)
    l = jnp.sum(pm, axis=-1, keepdims=True)
    o = _dot(pm.astype(BF16), vm)
    for c in range(4):
        p = jnp.exp(s[c] - m)
        l = l + jnp.sum(p, axis=-1, keepdims=True)
        o = o + _dot(p.astype(BF16), vs[c][...])
    o_ref[...] = (o / l).astype(BF16)


def _attention(g_qkv, m_qkv, bias, tabs):
    qblk, kvblk, var, mseq = tabs
    nsteps = qblk.shape[0]
    rows = g_qkv.shape[0]

    def kv_spec(col0, c):
        return pl.BlockSpec((KV_BLK, HD_ATTN),
                            lambda h, t, qb, kb, vr, ms: (kb[t] + c, col0 + h))

    in_specs = [pl.BlockSpec((QB_TOK, HD_ATTN), lambda h, t, qb, kb, vr, ms: (qb[t], h))]
    in_specs += [kv_spec(H_ATTN, c) for c in range(4)]
    in_specs += [kv_spec(2 * H_ATTN, c) for c in range(4)]
    in_specs += [
        pl.BlockSpec((CHUNK, HD_ATTN), lambda h, t, qb, kb, vr, ms: (ms[t], H_ATTN + h)),
        pl.BlockSpec((CHUNK, HD_ATTN), lambda h, t, qb, kb, vr, ms: (ms[t], 2 * H_ATTN + h)),
        pl.BlockSpec((1, 1, QB_TOK, WIN_TOK), lambda h, t, qb, kb, vr, ms: (vr[t], h, 0, 0)),
    ]
    grid_spec = pltpu.PrefetchScalarGridSpec(
        num_scalar_prefetch=4,
        grid=(H_ATTN, nsteps),
        in_specs=in_specs,
        out_specs=pl.BlockSpec((QB_TOK, HD_ATTN), lambda h, t, qb, kb, vr, ms: (qb[t], h)),
    )
    return pl.pallas_call(
        _attn_kernel,
        grid_spec=grid_spec,
        out_shape=jax.ShapeDtypeStruct((rows, D_ATTN), BF16),
        compiler_params=pltpu.CompilerParams(
            dimension_semantics=("arbitrary", "arbitrary"), vmem_limit_bytes=VMEM_LIMIT),
        name="nbr_attention",
    )(qblk, kvblk, var, mseq, g_qkv, *([g_qkv] * 8), m_qkv, m_qkv, bias)


def _meta_attn_kernel(q_ref, k_ref, v_ref, o_ref):
    q = q_ref[...]
    km = k_ref[PAD_ROWS:, :]
    vm = v_ref[PAD_ROWS:, :]
    s = _dot_nt(q, km)
    m = jnp.max(s, axis=-1, keepdims=True)
    p = jnp.exp(s - m)
    l = jnp.sum(p, axis=-1, keepdims=True)
    o_ref[...] = (_dot(p.astype(BF16), vm) / l).astype(BF16)


def _meta_attention(m_qkv):
    nseq = m_qkv.shape[0] // CHUNK
    return pl.pallas_call(
        _meta_attn_kernel,
        grid=(nseq, H_ATTN),
        in_specs=[
            pl.BlockSpec((CHUNK, HD_ATTN), lambda s, h: (s, h)),
            pl.BlockSpec((CHUNK, HD_ATTN), lambda s, h: (s, H_ATTN + h)),
            pl.BlockSpec((CHUNK, HD_ATTN), lambda s, h: (s, 2 * H_ATTN + h)),
        ],
        out_specs=pl.BlockSpec((CHUNK, HD_ATTN), lambda s, h: (s, h)),
        out_shape=jax.ShapeDtypeStruct((m_qkv.shape[0], D_ATTN), BF16),
        compiler_params=pltpu.CompilerParams(dimension_semantics=("arbitrary", "arbitrary")),
        name="meta_attention",
    )(m_qkv, m_qkv, m_qkv)


def _attn_bias_table(rpb):
    qc = np.arange(GRID_W)
    kc = np.arange(GRID_W)
    c0 = np.clip(qc - WIN_W // 2, 0, GRID_W - WIN_W)
    colmask = (kc[None, :] >= c0[:, None]) & (kc[None, :] < c0[:, None] + WIN_W)
    dc = np.clip(kc[None, :] - qc[:, None], -(WIN_W - 1), WIN_W - 1) + (WIN_W - 1)
    n_dr = 2 * WIN_H - 1
    n_dc = 2 * WIN_W - 1
    dc_onehot = (dc[None] == np.arange(n_dc)[:, None, None]).astype(np.float32)
    toeplitz = jnp.einsum("hrd,dqk->hrqk", rpb, jnp.asarray(dc_onehot),
                          precision=lax.Precision.HIGHEST)
    i = np.arange(QB_ROWS)[:, None]
    j = np.arange(2 * QB_ROWS)[None, :]
    dr_onehot = np.zeros((3, QB_ROWS, 2 * QB_ROWS, n_dr), np.float32)
    rowvalid_all = np.zeros((3, QB_ROWS, 2 * QB_ROWS), bool)
    for variant in range(3):
        if variant == 0:
            krel = j + 0 * i
            r0rel = np.maximum(i - WIN_H // 2, 0) + 0 * j
        elif variant == 1:
            krel = j - WIN_H // 2 + 0 * i
            r0rel = i - WIN_H // 2 + 0 * j
        else:
            krel = j - WIN_H + 0 * i
            r0rel = np.minimum(i - WIN_H // 2, 0) + 0 * j
        rowvalid = (krel >= r0rel) & (krel < r0rel + WIN_H)
        dr = np.clip(krel - i + (WIN_H - 1), 0, 2 * WIN_H - 2)
        dr_onehot[variant] = (dr[:, :, None] == np.arange(n_dr)) & rowvalid[:, :, None]
        rowvalid_all[variant] = rowvalid
    b = jnp.einsum("vijr,hrqk->vhiqjk", jnp.asarray(dr_onehot), toeplitz,
                   precision=lax.Precision.HIGHEST)
    valid = jnp.logical_and(jnp.asarray(rowvalid_all)[:, None, :, None, :, None],
                            jnp.asarray(colmask)[None, None, None, :, None, :])
    b = jnp.where(valid, b, NEG_INF)
    return b.reshape(3, H_ATTN, QB_TOK, WIN_TOK)


def _seg_sum(x, e2):
    stacked = jnp.concatenate([x[:, j * LANE:(j + 1) * LANE] for j in range(N_PAIR)], axis=0)
    res = jnp.dot(stacked, e2, preferred_element_type=F32, precision=lax.Precision.HIGHEST)
    return jnp.concatenate([res[j * CHUNK:(j + 1) * CHUNK] for j in range(N_PAIR)], axis=1)


def _rwkv_kernel(gblk_s, mblk_s, flag_s, *refs, reverse):
    del gblk_s, mblk_s
    if reverse:
        (gp_ref, mp_ref, mu_ref, w0_ref, w2_ref, a0_ref, a2_ref, kk_ref, ka_ref, rk_ref,
         gyb_ref, myb_ref, g2_ref, lg_ref, lb_ref, og_ref, om_ref, state, carry) = refs
    else:
        (gp_ref, mp_ref, mu_ref, w0_ref, w2_ref, a0_ref, a2_ref, kk_ref, ka_ref, rk_ref,
         og_ref, om_ref, state, carry) = refs
    step = pl.program_id(0)
    fl = flag_s[step]
    is_first = (fl & 1) == 1
    is_meta = (fl & 2) == 2

    @pl.when(is_first)
    def _():
        state[...] = jnp.zeros_like(state)
        carry[...] = jnp.zeros_like(carry)

    praw = jnp.where(is_meta, mp_ref[...], gp_ref[...])
    row = lax.broadcasted_iota(jnp.int32, (CHUNK, 1), 0)
    pad = jnp.logical_and(is_meta, row < PAD_ROWS)
    p_sh = jnp.where(pad, 0.0, praw[:, :N_SHIFT_P])
    gd = praw[:, N_SHIFT_P:]

    if reverse:
        rolled = pltpu.roll(p_sh, CHUNK - 1, axis=0)
        nb = jnp.where(row == CHUNK - 1, carry[0:1, :], rolled)
        carry[...] = p_sh[0:8, :]
    else:
        rolled = pltpu.roll(p_sh, 1, axis=0)
        nb = jnp.where(row == 0, carry[7:8, :], rolled)
        carry[...] = p_sh[CHUNK - 8:CHUNK, :]
    f = p_sh + (nb - p_sh) * mu_ref[...]
    r = f[:, 0:D_RWKV]
    k = f[:, D_RWKV:2 * D_RWKV]
    v = f[:, 2 * D_RWKV:3 * D_RWKV]
    wd = f[:, 3 * D_RWKV:3 * D_RWKV + R_LOW]
    ad = f[:, 3 * D_RWKV + R_LOW:3 * D_RWKV + 2 * R_LOW]

    ii = lax.broadcasted_iota(jnp.int32, (LANE, LANE), 0)
    jj = lax.broadcasted_iota(jnp.int32, (LANE, LANE), 1)
    same = (ii // CHUNK) == (jj // CHUNK)
    e2 = jnp.where(same, 1.0, 0.0).astype(F32)

    wl = w0_ref[...] + _dot(jnp.tanh(wd).astype(BF16), w2_ref[...])
    nwl = -wl
    softplus = jnp.maximum(nwl, 0.0) + jnp.log(1.0 + jnp.exp(-jnp.abs(nwl)))
    lw = -jnp.exp(-softplus - 0.5)
    a = 1.0 / (1.0 + jnp.exp(-(a0_ref[...] + _dot(ad.astype(BF16), a2_ref[...]))))
    kkr = k * kk_ref[...]
    kk = kkr / jnp.maximum(jnp.sqrt(_seg_sum(kkr * kkr, e2)), 1e-12)
    kmod = k * (1.0 + (a - 1.0) * ka_ref[...])
    bonus = _seg_sum(r * kmod * rk_ref[...], e2) * v

    ti = lax.broadcasted_iota(jnp.int32, (CHUNK, CHUNK), 0)
    tj = lax.broadcasted_iota(jnp.int32, (CHUNK, CHUNK), 1)
    tri = jnp.where((tj >= ti) if reverse else (tj <= ti), 1.0, 0.0).astype(F32)
    cum = jnp.dot(tri, lw, preferred_element_type=F32, precision=lax.Precision.HIGHEST)
    cum_end = cum[0:1, :] if reverse else cum[CHUNK - 1:CHUNK, :]
    e_pos = jnp.exp(cum)
    e_neg = jnp.exp(-cum)
    e_exc = jnp.exp(cum - lw)
    e_end = jnp.exp(cum_end - cum)
    bb = kk * a
    r_t = r * e_pos
    k_t = kmod * e_neg
    b_t = bb * e_neg
    a_t = -kk * e_exc
    k_h = kmod * e_end
    b_h = bb * e_end

    ti2 = ii % CHUNK
    tj2 = jj % CHUNK
    if reverse:
        strict = jnp.logical_and(same, tj2 > ti2)
        incl = jnp.logical_and(same, tj2 >= ti2)
    else:
        strict = jnp.logical_and(same, tj2 < ti2)
        incl = jnp.logical_and(same, tj2 <= ti2)
    eye = jnp.where(ii == jj, 1.0, 0.0).astype(F32)
    lane64 = lax.broadcasted_iota(jnp.int32, (1, CHUNK, LANE), 2) < HD_RWKV

    def tiles(x):
        return jnp.stack([x[:, j * LANE:(j + 1) * LANE] for j in range(N_PAIR)], axis=0)

    def stack_masked(x):
        return jnp.concatenate([jnp.where(lane64, x, 0.0), jnp.where(lane64, 0.0, x)], axis=1)

    def stack_dup(x):
        return jnp.concatenate([x, x], axis=1)

    v3 = tiles(v)
    a_s = stack_masked(tiles(a_t))
    r_s = stack_masked(tiles(r_t))
    bh_s = stack_masked(tiles(b_h))
    kh_s = stack_masked(tiles(k_h))
    v_s = stack_masked(v3)
    b_d = stack_dup(tiles(b_t))
    k_d = stack_dup(tiles(k_t))
    v_d = stack_dup(v3)
    m1 = _bmm_nt(jnp.concatenate([a_s, r_s], axis=1).astype(BF16),
                 jnp.concatenate([b_d, k_d], axis=1).astype(BF16))
    n_ab = jnp.where(strict, m1[:, :LANE, :LANE], 0.0)
    a_ak = jnp.where(strict, m1[:, :LANE, LANE:], 0.0)
    a_rb = jnp.where(incl, m1[:, LANE:, :LANE], 0.0)
    a_rk = jnp.where(incl, m1[:, LANE:, LANE:], 0.0)
    tinv = eye + n_ab
    npow = n_ab
    for _ in range(5):
        npb = npow.astype(BF16)
        npow = _bmm(npb, npb)
        tinv = tinv + _bmm(tinv.astype(BF16), npow.astype(BF16))
    gy = _bmm(jnp.concatenate([a_ak, a_rk], axis=1).astype(BF16), v_d.astype(BF16))
    g_s = jnp.where(same, gy[:, :LANE], 0.0)
    y0_s = jnp.where(same, gy[:, LANE:], 0.0)
    tz = _bmm(tinv.astype(BF16), jnp.concatenate([a_s, g_s], axis=2).astype(BF16))
    w_s = tz[:, :, :LANE]
    u0_s = tz[:, :, LANE:]
    s_add = _bmm_tn(kh_s.astype(BF16), v_s.astype(BF16))
    s_old = state[...]
    xr = _bmm(jnp.concatenate([w_s, r_s], axis=1).astype(BF16), s_old.astype(BF16))
    u_s = xr[:, :LANE] + u0_s
    y_s = xr[:, LANE:] + _bmm(a_rb.astype(BF16), u_s.astype(BF16)) + y0_s
    y3 = y_s[:, :CHUNK] + y_s[:, CHUNK:]
    g_end = jnp.exp(jnp.swapaxes(jnp.broadcast_to(tiles(cum_end), (N_PAIR, LANE, LANE)), 1, 2))
    state[...] = g_end * s_old + _bmm_tn(bh_s.astype(BF16), u_s.astype(BF16)) + s_add
    y = jnp.concatenate([y3[j] for j in range(N_PAIR)], axis=1)

    if not reverse:
        res = jnp.concatenate([y, bonus], axis=1)
    else:
        yb = jnp.where(is_meta, myb_ref[...], gyb_ref[...])
        yy = y + yb[:, :D_RWKV]
        mean = _seg_sum(yy, e2) * (1.0 / HD_RWKV)
        yc = yy - mean
        var = _seg_sum(yc * yc, e2) * (1.0 / HD_RWKV)
        yn = yc * lax.rsqrt(var + GN_EPS) * lg_ref[...] + lb_ref[...]
        gate = _dot((1.0 / (1.0 + jnp.exp(-gd))).astype(BF16), g2_ref[...])
        res = ((yn + yb[:, D_RWKV:] + bonus) * gate).astype(BF16)

    @pl.when(is_meta)
    def _():
        om_ref[...] = res

    @pl.when(jnp.logical_not(is_meta))
    def _():
        og_ref[...] = res


def _rwkv_scan(g_prw, m_prw, params, tabs, reverse, prev=None):
    gblk, mblk, flags = tabs
    nsteps = gblk.shape[0]
    gmap = lambda t, gb, mb, fl: (gb[t], 0)
    mmap = lambda t, gb, mb, fl: (mb[t], 0)
    cmap = lambda t, gb, mb, fl: (0, 0)
    vec = lambda n: pl.BlockSpec((1, n), cmap)
    in_specs = [
        pl.BlockSpec((CHUNK, N_RW), gmap),
        pl.BlockSpec((CHUNK, N_RW), mmap),
        vec(N_SHIFT_P), vec(D_RWKV),
        pl.BlockSpec((R_LOW, D_RWKV), cmap), vec(D_RWKV),
        pl.BlockSpec((R_LOW, D_RWKV), cmap), vec(D_RWKV), vec(D_RWKV), vec(D_RWKV),
    ]
    args = [g_prw, m_prw, params["mu"], params["w0"], params["w2"], params["a0"], params["a2"],
            params["k_k"], params["k_a"], params["r_k"]]
    if reverse:
        g_yb, m_yb = prev
        in_specs += [
            pl.BlockSpec((CHUNK, 2 * D_RWKV), gmap),
            pl.BlockSpec((CHUNK, 2 * D_RWKV), mmap),
            pl.BlockSpec((R_GATE, D_RWKV), cmap), vec(D_RWKV), vec(D_RWKV),
        ]
        args += [g_yb, m_yb, params["g2"], params["lnx_g"], params["lnx_b"]]
        width, dt = D_RWKV, BF16
    else:
        width, dt = 2 * D_RWKV, F32
    grid_spec = pltpu.PrefetchScalarGridSpec(
        num_scalar_prefetch=3,
        grid=(nsteps,),
        in_specs=in_specs,
        out_specs=[pl.BlockSpec((CHUNK, width), gmap), pl.BlockSpec((CHUNK, width), mmap)],
        scratch_shapes=[pltpu.VMEM((N_PAIR, LANE, LANE), F32), pltpu.VMEM((8, N_SHIFT_P), F32)],
    )
    return pl.pallas_call(
        functools.partial(_rwkv_kernel, reverse=reverse),
        grid_spec=grid_spec,
        out_shape=[jax.ShapeDtypeStruct((g_prw.shape[0], width), dt),
                   jax.ShapeDtypeStruct((m_prw.shape[0], width), dt)],
        compiler_params=pltpu.CompilerParams(
            dimension_semantics=("arbitrary",), vmem_limit_bytes=VMEM_LIMIT),
        name="rwkv_bwd" if reverse else "rwkv_fwd",
    )(gblk, mblk, flags, *args)


def _outproj_kernel(oa_ref, or_ref, x_ref, eg_ref, eb_ref, wa_ref, wr_ref, g1_ref, b1_ref, h1_ref):
    h0 = _layer_norm(x_ref[...], eg_ref[...], eb_ref[...])
    mix = _dot(oa_ref[...], wa_ref[...]) + _dot(or_ref[...], wr_ref[...])
    h1_ref[...] = _layer_norm(DEEPNORM_ALPHA * h0 + mix, g1_ref[...], b1_ref[...])


def _outproj(o_attn, o_rwkv, x, eg, eb, w_out_bf, g1, b1, tm):
    rows = x.shape[0]
    cmap = lambda i: (0, 0)
    vec = pl.BlockSpec((1, D_MODEL), cmap)
    return pl.pallas_call(
        _outproj_kernel,
        grid=(rows // tm,),
        in_specs=[
            pl.BlockSpec((tm, D_ATTN), lambda i: (i, 0)),
            pl.BlockSpec((tm, D_RWKV), lambda i: (i, 0)),
            pl.BlockSpec((tm, D_MODEL), lambda i: (i, 0)),
            vec, vec,
            pl.BlockSpec((D_ATTN, D_MODEL), lambda i: (0, 0)),
            pl.BlockSpec((D_RWKV, D_MODEL), lambda i: (1, 0)),
            vec, vec,
        ],
        out_specs=pl.BlockSpec((tm, D_MODEL), lambda i: (i, 0)),
        out_shape=jax.ShapeDtypeStruct((rows, D_MODEL), F32),
        compiler_params=pltpu.CompilerParams(
            dimension_semantics=("arbitrary",), vmem_limit_bytes=VMEM_LIMIT),
        name="outproj_ln1",
    )(o_attn, o_rwkv, x, eg, eb, w_out_bf, w_out_bf, g1, b1)


def _ffn_kernel(pg_s, pm_s, nx_s, fl_s, h_ref, pg_ref, pm_ref, nx_ref, wg_ref, wu_ref, wo_ref,
                cw_ref, cb_ref, g2_ref, b2_ref, *refs, n_out, tiles_first):
    del pg_s, pm_s, nx_s
    outs = refs[:n_out]
    lhs, gate_s, acc = refs[n_out:]
    i = pl.program_id(0)
    c = pl.program_id(1)
    fl = fl_s[i]

    @pl.when(c == 0)
    def _():
        prev = jnp.where((fl & 1) == 1, pm_ref[...], pg_ref[...])
        nxt = jnp.where((fl & 2) == 2, 0.0, nx_ref[...])
        lhs[0:HALO, :] = prev.astype(BF16)
        lhs[HALO:HALO + TM, :] = h_ref[...].astype(BF16)
        lhs[HALO + TM:, :] = nxt.astype(BF16)
        acc[...] = jnp.zeros_like(acc)

    gate_s[...] = _dot(lhs[...], wg_ref[...])
    up = _dot(lhs[HALO:HALO + TM, :], wu_ref[...])
    gate = (gate_s[HALO - 1:HALO - 1 + TM, :] * cw_ref[0:1, :]
            + gate_s[HALO:HALO + TM, :] * cw_ref[1:2, :]
            + gate_s[HALO + 1:HALO + 1 + TM, :] * cw_ref[2:3, :] + cb_ref[...])
    act = 0.5 * gate * (1.0 + lax.erf(gate * 0.7071067811865476)) * up
    acc[...] += _dot(act.astype(BF16), wo_ref[...])

    @pl.when(c == pl.num_programs(1) - 1)
    def _():
        y = _layer_norm(DEEPNORM_ALPHA * h_ref[...] + acc[...], g2_ref[...], b2_ref[...])
        if n_out == 1:
            outs[0][...] = y
        else:
            @pl.when(i < tiles_first)
            def _():
                outs[0][...] = y

            @pl.when(i >= tiles_first)
            def _():
                outs[1][...] = y


def _ffn(g_h1, m_h1, w_in_bf, w_out_bf, conv_w, conv_b, g2, b2, tabs, out_rows):
    pg, pm, nx, fl = tabs
    ntiles = g_h1.shape[0] // TM
    nfc = D_FF // FC
    cmap = lambda i, c, *_: (0, 0)
    vec = pl.BlockSpec((1, D_MODEL), cmap)
    tiles_first = out_rows[0] // TM
    if len(out_rows) == 1:
        out_specs = [pl.BlockSpec((TM, D_MODEL), lambda i, c, *_: (i, 0))]
    else:
        out_specs = [
            pl.BlockSpec((TM, D_MODEL), lambda i, c, *_: (jnp.minimum(i, tiles_first - 1), 0)),
            pl.BlockSpec((TM, D_MODEL), lambda i, c, *_: (jnp.maximum(i - tiles_first, 0), 0)),
        ]
    grid_spec = pltpu.PrefetchScalarGridSpec(
        num_scalar_prefetch=4,
        grid=(ntiles, nfc),
        in_specs=[
            pl.BlockSpec((TM, D_MODEL), lambda i, c, *_: (i, 0)),
            pl.BlockSpec((HALO, D_MODEL), lambda i, c, pg, pm, nx, fl: (pg[i], 0)),
            pl.BlockSpec((HALO, D_MODEL), lambda i, c, pg, pm, nx, fl: (pm[i], 0)),
            pl.BlockSpec((HALO, D_MODEL), lambda i, c, pg, pm, nx, fl: (nx[i], 0)),
            pl.BlockSpec((D_MODEL, FC), lambda i, c, *_: (0, c)),
            pl.BlockSpec((D_MODEL, FC), lambda i, c, *_: (0, nfc + c)),
            pl.BlockSpec((FC, D_MODEL), lambda i, c, *_: (c, 0)),
            pl.BlockSpec((3, FC), lambda i, c, *_: (0, c)),
            pl.BlockSpec((1, FC), lambda i, c, *_: (0, c)),
            vec, vec,
        ],
        out_specs=out_specs,
        scratch_shapes=[
            pltpu.VMEM((TM + 2 * HALO, D_MODEL), BF16),
            pltpu.VMEM((TM + 2 * HALO, FC), F32),
            pltpu.VMEM((TM, D_MODEL), F32),
        ],
    )
    return pl.pallas_call(
        functools.partial(_ffn_kernel, n_out=len(out_rows), tiles_first=tiles_first),
        grid_spec=grid_spec,
        out_shape=[jax.ShapeDtypeStruct((r, D_MODEL), F32) for r in out_rows],
        compiler_params=pltpu.CompilerParams(
            dimension_semantics=("arbitrary", "arbitrary"), vmem_limit_bytes=VMEM_LIMIT),
        name="ffn_ln2",
    )(pg, pm, nx, fl, g_h1, g_h1, m_h1, g_h1, w_in_bf, w_in_bf, w_out_bf, conv_w, conv_b, g2, b2)


def _attn_tables(seq_lens):
    rec = []
    base = 0
    for s, t_len in enumerate(seq_lens):
        rows = t_len // GRID_W
        for qb in range(rows // QB_ROWS):
            w0 = min(max(qb * QB_ROWS - WIN_H // 2, 0), rows - 2 * QB_ROWS)
            variant = 0 if qb == 0 else (2 if qb == rows // QB_ROWS - 1 else 1)
            rec.append((variant, (base + qb * QB_TOK) // QB_TOK, (base + w0 * GRID_W) // KV_BLK, s))
        base += t_len
    rec.sort(key=lambda x: x[0])
    arr = np.asarray(rec, np.int32)
    return tuple(jnp.asarray(arr[:, c]) for c in (1, 2, 0, 3))


def _rwkv_tables(seq_lens, reverse):
    gblk, mblk, flags = [], [], []
    base = 0
    for s, t_len in enumerate(seq_lens):
        n = t_len // CHUNK
        first_blk = base // CHUNK
        steps = [(first_blk, 2)] + [(first_blk + c, 0) for c in range(n)]
        if reverse:
            steps = steps[::-1]
        for idx, (blk, fl) in enumerate(steps):
            gblk.append(blk)
            mblk.append(s)
            flags.append(fl | (1 if idx == 0 else 0))
        base += t_len
    return tuple(jnp.asarray(np.asarray(a, np.int32)) for a in (gblk, mblk, flags))


def _ffn_tables(seq_lens):
    pg, pm, nx, fl = [], [], [], []
    base = 0
    total = sum(seq_lens)
    for s, t_len in enumerate(seq_lens):
        for t in range(t_len // TM):
            r0 = base + t * TM
            first = t == 0
            last = t == t_len // TM - 1
            pg.append(max(r0 // HALO - 1, 0))
            pm.append((s * CHUNK + CHUNK - HALO) // HALO)
            nx.append(min((r0 + TM) // HALO, total // HALO - 1))
            fl.append((1 if first else 0) | (2 if last else 0))
        base += t_len
    return tuple(jnp.asarray(np.asarray(a, np.int32)) for a in (pg, pm, nx, fl))


def _pad_cols(w, n):
    return jnp.pad(w, ((0, 0), (0, n - w.shape[1])))


def _encode_all(xs, meta_tokens, emb_ln_g, emb_ln_b, w_in, attn_rpb, rwkv_mu, rwkv_w0, rwkv_w2,
                rwkv_a0, rwkv_a2, rwkv_g2, rwkv_k_k, rwkv_k_a, rwkv_r_k, rwkv_lnx_g, rwkv_lnx_b,
                w_out, ln1_g, ln1_b, ffn_w_in, ffn_conv_w, ffn_conv_b, ffn_w_out, ln2_g, ln2_b):
    seq_lens = []
    for x in xs:
        seq_lens += [x.shape[1]] * x.shape[0]
    nseq = len(seq_lens)
    x_g = jnp.concatenate([x.reshape(-1, D_MODEL) for x in xs], axis=0)
    meta_blk = jnp.concatenate([jnp.zeros((PAD_ROWS, D_MODEL), F32), meta_tokens.astype(F32)], axis=0)
    x_m = jnp.tile(meta_blk, (nseq, 1))

    o1 = N_QKV
    o2 = o1 + 3 * D_RWKV
    o3 = o2 + R_DECAY
    o4 = o3 + R_ICLR
    wl = w_in[0]
    w_in_p = jnp.concatenate(
        [wl[:, :o2], _pad_cols(wl[:, o2:o3], R_LOW), _pad_cols(wl[:, o3:o4], R_LOW), wl[:, o4:]],
        axis=1).astype(BF16)
    col_scale = jnp.concatenate(
        [jnp.full((1, D_ATTN), HD_ATTN ** -0.5, F32), jnp.ones((1, N_IN_P - D_ATTN), F32)], axis=1)
    eg = emb_ln_g.reshape(1, D_MODEL)
    eb = emb_ln_b.reshape(1, D_MODEL)
    mu = rwkv_mu[0]
    o5 = 3 * D_RWKV
    mu_p = jnp.concatenate(
        [mu[:, :o5], _pad_cols(mu[:, o5:o5 + R_DECAY], R_LOW), _pad_cols(mu[:, o5 + R_DECAY:], R_LOW)],
        axis=1)
    pad_rows = lambda w: jnp.pad(w, ((0, R_LOW - w.shape[0]), (0, 0))).astype(BF16)
    row = lambda v: v.reshape(1, -1)

    def dir_params(d):
        return dict(mu=mu_p[d:d + 1], w0=row(rwkv_w0[0, d]), w2=pad_rows(rwkv_w2[0, d]),
                    a0=row(rwkv_a0[0, d]), a2=pad_rows(rwkv_a2[0, d]),
                    k_k=row(rwkv_k_k[0]), k_a=row(rwkv_k_a[0]), r_k=row(rwkv_r_k[0]),
                    g2=rwkv_g2[0].astype(BF16), lnx_g=row(rwkv_lnx_g[0]), lnx_b=row(rwkv_lnx_b[0]))

    w_out_bf = w_out[0].astype(BF16)
    ffn_w_in_bf = ffn_w_in[0].astype(BF16)
    ffn_w_out_bf = ffn_w_out[0].astype(BF16)
    bias = _attn_bias_table(attn_rpb[0])

    g_qkv, g_prw = _inproj(x_g, eg, eb, w_in_p, col_scale, TM)
    m_qkv, m_prw = _inproj(x_m, eg, eb, w_in_p, col_scale, x_m.shape[0])

    g_oattn = _attention(g_qkv, m_qkv, bias, _attn_tables(seq_lens))
    m_oattn = _meta_attention(m_qkv)
    g_yb, m_yb = _rwkv_scan(g_prw, m_prw, dir_params(0), _rwkv_tables(seq_lens, False), False)
    g_orwkv, m_orwkv = _rwkv_scan(g_prw, m_prw, dir_params(1), _rwkv_tables(seq_lens, True), True,
                                  prev=(g_yb, m_yb))

    g1 = ln1_g.reshape(1, D_MODEL)
    b1 = ln1_b.reshape(1, D_MODEL)
    g_h1 = _outproj(g_oattn, g_orwkv, x_g, eg, eb, w_out_bf, g1, b1, TM)
    m_h1 = _outproj(m_oattn, m_orwkv, x_m, eg, eb, w_out_bf, g1, b1, x_m.shape[0])

    out_rows = [x.shape[0] * x.shape[1] for x in xs]
    ys = _ffn(g_h1, m_h1, ffn_w_in_bf, ffn_w_out_bf, ffn_conv_w[0], ffn_conv_b.reshape(1, D_FF),
              ln2_g.reshape(1, D_MODEL), ln2_b.reshape(1, D_MODEL), _ffn_tables(seq_lens), out_rows)
    return [y.reshape(x.shape) for y, x in zip(ys, xs)]


def kernel(x_prompt, x_sample, meta_tokens, emb_ln_g, emb_ln_b, w_in, attn_rpb, rwkv_mu, rwkv_w0, rwkv_w2, rwkv_a0, rwkv_a2, rwkv_g2, rwkv_k_k, rwkv_k_a, rwkv_r_k, rwkv_lnx_g, rwkv_lnx_b, w_out, ln1_g, ln1_b, ffn_w_in, ffn_conv_w, ffn_conv_b, ffn_w_out, ln2_g, ln2_b):
    y_prompt, y_sample = _encode_all(
        [x_prompt, x_sample], meta_tokens, emb_ln_g, emb_ln_b, w_in, attn_rpb, rwkv_mu, rwkv_w0,
        rwkv_w2, rwkv_a0, rwkv_a2, rwkv_g2, rwkv_k_k, rwkv_k_a, rwkv_r_k, rwkv_lnx_g, rwkv_lnx_b,
        w_out, ln1_g, ln1_b, ffn_w_in, ffn_conv_w, ffn_conv_b, ffn_w_out, ln2_g, ln2_b)
    return (y_prompt, y_sample)
```

```python
import functools

import numpy as np
import jax
import jax.numpy as jnp
from jax import lax
from jax.experimental import pallas as pl
from jax.experimental.pallas import tpu as pltpu

D_MODEL = 2048
N_META = 16
GRID_W = 64
WIN_H = 8
WIN_W = 16
D_ATTN = 1024
HD_ATTN = 128
H_ATTN = D_ATTN // HD_ATTN
D_RWKV = 1024
HD_RWKV = 64
R_DECAY = 96
R_ICLR = 96
R_GATE = 256
D_FF = 5632
DEEPNORM_ALPHA = 2.0 ** 0.25
LN_EPS = 1e-5
GN_EPS = 64e-5
NEG_INF = -1e30

LANE = 128
CHUNK = 64
PAD_ROWS = CHUNK - N_META
R_LOW = 128
N_SHIFT_P = 3 * D_RWKV + 2 * R_LOW
N_RW = N_SHIFT_P + R_GATE
N_QKV = 3 * D_ATTN
N_PAIR = D_RWKV // LANE
QB_ROWS = 8
QB_TOK = QB_ROWS * GRID_W
KV_BLK = 256
WIN_TOK = 4 * KV_BLK
TM = 512
TM_FFN = 1024
TM_OUT = 256
FC = 512
FC_SUB = 256
HALO = 16
VMEM_LIMIT = 56 * 1024 * 1024

F32 = jnp.float32
BF16 = jnp.bfloat16


def _layer_norm(x, g, b):
    mu = jnp.mean(x, axis=-1, keepdims=True)
    xc = x - mu
    var = jnp.mean(xc * xc, axis=-1, keepdims=True)
    return xc * lax.rsqrt(var + LN_EPS) * g + b


def _dot(a, b):
    return jnp.dot(a, b, preferred_element_type=F32)


def _dot_nt(a, b):
    return lax.dot_general(a, b, (((1,), (1,)), ((), ())), preferred_element_type=F32)


def _bmm(a, b):
    return jnp.einsum("bik,bkj->bij", a, b, preferred_element_type=F32)


def _bmm_nt(a, b):
    return jnp.einsum("bik,bjk->bij", a, b, preferred_element_type=F32)


def _bmm_tn(a, b):
    return jnp.einsum("bki,bkj->bij", a, b, preferred_element_type=F32)


def _select_rows(x_refs, tiles_first):
    if len(x_refs) == 1:
        return x_refs[0][...]
    return jnp.where(pl.program_id(0) < tiles_first, x_refs[0][...], x_refs[1][...])


def _row_specs(xs, tm, width):
    if len(xs) == 1:
        return [pl.BlockSpec((tm, width), lambda i: (i, 0))], 0
    tiles_first = xs[0].shape[0] // tm
    return [pl.BlockSpec((tm, width), lambda i: (jnp.minimum(i, tiles_first - 1), 0)),
            pl.BlockSpec((tm, width), lambda i: (jnp.maximum(i - tiles_first, 0), 0))], tiles_first


def _inproj_kernel(*refs, n_x, tiles_first, q_scale):
    x_refs = refs[:n_x]
    g_ref, b_ref, w_ref, o_ref = refs[n_x:]
    h = _layer_norm(_select_rows(x_refs, tiles_first), g_ref[...], b_ref[...]).astype(BF16)
    acc = _dot(h, w_ref[...])
    if q_scale:
        o_ref[:, :D_ATTN] = (acc[:, :D_ATTN] * (HD_ATTN ** -0.5)).astype(o_ref.dtype)
        o_ref[:, D_ATTN:] = acc[:, D_ATTN:].astype(o_ref.dtype)
    else:
        o_ref[...] = acc.astype(o_ref.dtype)


def _inproj(xs, ln_g, ln_b, w, out_dtype, q_scale, tm):
    rows = sum(x.shape[0] for x in xs)
    n_out = w.shape[1]
    x_specs, tiles_first = _row_specs(xs, tm, D_MODEL)
    cmap = lambda i: (0, 0)
    return pl.pallas_call(
        functools.partial(_inproj_kernel, n_x=len(xs), tiles_first=tiles_first, q_scale=q_scale),
        grid=(rows // tm,),
        in_specs=x_specs + [
            pl.BlockSpec((1, D_MODEL), cmap),
            pl.BlockSpec((1, D_MODEL), cmap),
            pl.BlockSpec((D_MODEL, n_out), cmap, pipeline_mode=pl.Buffered(1)),
        ],
        out_specs=pl.BlockSpec((tm, n_out), lambda i: (i, 0)),
        out_shape=jax.ShapeDtypeStruct((rows, n_out), out_dtype),
        compiler_params=pltpu.CompilerParams(
            dimension_semantics=("arbitrary",), vmem_limit_bytes=VMEM_LIMIT),
        name="inproj_qkv" if q_scale else "inproj_rwkv",
    )(*xs, ln_g, ln_b, w)


def _attn_kernel(qblk_s, kvblk_s, var_s, mseq_s, q_ref, k0, k1, k2, k3, v0, v1, v2, v3,
                 mk_ref, mv_ref, bias_ref, o_ref):
    del qblk_s, kvblk_s, var_s, mseq_s
    q = q_ref[...]
    ks = (k0, k1, k2, k3)
    vs = (v0, v1, v2, v3)
    s = [_dot_nt(q, ks[c][...]) + bias_ref[0, 0, :, c * KV_BLK:(c + 1) * KV_BLK] for c in range(4)]
    km = mk_ref[PAD_ROWS:, :]
    vm = mv_ref[PAD_ROWS:, :]
    sm = _dot_nt(q, km)
    m = jnp.max(sm, axis=-1, keepdims=True)
    for c in range(4):
        m = jnp.maximum(m, jnp.max(s[c], axis=-1, keepdims=True))
    pm = jnp.exp(sm - m)
    l = jnp.sum(pm, axis=-1, keepdims=True)
    o = _dot(pm.astype(BF16), vm)
    for c in range(4):
        p = jnp.exp(s[c] - m)
        l = l + jnp.sum(p, axis=-1, keepdims=True)
        o = o + _dot(p.astype(BF16), vs[c][...])
    o_ref[...] = (o / l).astype(BF16)


def _attention(g_qkv, m_qkv, bias, tabs):
    qblk, kvblk, var, mseq = tabs
    nsteps = qblk.shape[0]
    rows = g_qkv.shape[0]

    def kv_spec(col0, c):
        return pl.BlockSpec((KV_BLK, HD_ATTN),
                            lambda h, t, qb, kb, vr, ms: (kb[t] + c, col0 + h))

    in_specs = [pl.BlockSpec((QB_TOK, HD_ATTN), lambda h, t, qb, kb, vr, ms: (qb[t], h))]
    in_specs += [kv_spec(H_ATTN, c) for c in range(4)]
    in_specs += [kv_spec(2 * H_ATTN, c) for c in range(4)]
    in_specs += [
        pl.BlockSpec((CHUNK, HD_ATTN), lambda h, t, qb, kb, vr, ms: (ms[t], H_ATTN + h)),
        pl.BlockSpec((CHUNK, HD_ATTN), lambda h, t, qb, kb, vr, ms: (ms[t], 2 * H_ATTN + h)),
        pl.BlockSpec((1, 1, QB_TOK, WIN_TOK), lambda h, t, qb, kb, vr, ms: (vr[t], h, 0, 0)),
    ]
    grid_spec = pltpu.PrefetchScalarGridSpec(
        num_scalar_prefetch=4,
        grid=(H_ATTN, nsteps),
        in_specs=in_specs,
        out_specs=pl.BlockSpec((QB_TOK, HD_ATTN), lambda h, t, qb, kb, vr, ms: (qb[t], h)),
    )
    return pl.pallas_call(
        _attn_kernel,
        grid_spec=grid_spec,
        out_shape=jax.ShapeDtypeStruct((rows, D_ATTN), BF16),
        compiler_params=pltpu.CompilerParams(
            dimension_semantics=("arbitrary", "arbitrary"), vmem_limit_bytes=VMEM_LIMIT),
        name="nbr_attention",
    )(qblk, kvblk, var, mseq, g_qkv, *([g_qkv] * 8), m_qkv, m_qkv, bias)


def _meta_attn_kernel(q_ref, k_ref, v_ref, o_ref):
    q = q_ref[...]
    km = k_ref[PAD_ROWS:, :]
    vm = v_ref[PAD_ROWS:, :]
    s = _dot_nt(q, km)
    m = jnp.max(s, axis=-1, keepdims=True)
    p = jnp.exp(s - m)
    l = jnp.sum(p, axis=-1, keepdims=True)
    o_ref[...] = (_dot(p.astype(BF16), vm) / l).astype(BF16)


def _meta_attention(m_qkv):
    nseq = m_qkv.shape[0] // CHUNK
    return pl.pallas_call(
        _meta_attn_kernel,
        grid=(nseq, H_ATTN),
        in_specs=[
            pl.BlockSpec((CHUNK, HD_ATTN), lambda s, h: (s, h)),
            pl.BlockSpec((CHUNK, HD_ATTN), lambda s, h: (s, H_ATTN + h)),
            pl.BlockSpec((CHUNK, HD_ATTN), lambda s, h: (s, 2 * H_ATTN + h)),
        ],
        out_specs=pl.BlockSpec((CHUNK, HD_ATTN), lambda s, h: (s, h)),
        out_shape=jax.ShapeDtypeStruct((m_qkv.shape[0], D_ATTN), BF16),
        compiler_params=pltpu.CompilerParams(dimension_semantics=("arbitrary", "arbitrary")),
        name="meta_attention",
    )(m_qkv, m_qkv, m_qkv)


def _attn_bias_table(rpb):
    qc = np.arange(GRID_W)
    kc = np.arange(GRID_W)
    c0 = np.clip(qc - WIN_W // 2, 0, GRID_W - WIN_W)
    colmask = (kc[None, :] >= c0[:, None]) & (kc[None, :] < c0[:, None] + WIN_W)
    dc = np.clip(kc[None, :] - qc[:, None], -(WIN_W - 1), WIN_W - 1) + (WIN_W - 1)
    n_dr = 2 * WIN_H - 1
    n_dc = 2 * WIN_W - 1
    dc_onehot = (dc[None] == np.arange(n_dc)[:, None, None]).astype(np.float32)
    toeplitz = jnp.einsum("hrd,dqk->hrqk", rpb, jnp.asarray(dc_onehot),
                          precision=lax.Precision.HIGHEST)
    i = np.arange(QB_ROWS)[:, None]
    j = np.arange(2 * QB_ROWS)[None, :]
    dr_onehot = np.zeros((3, QB_ROWS, 2 * QB_ROWS, n_dr), np.float32)
    rowvalid_all = np.zeros((3, QB_ROWS, 2 * QB_ROWS), bool)
    for variant in range(3):
        if variant == 0:
            krel = j + 0 * i
            r0rel = np.maximum(i - WIN_H // 2, 0) + 0 * j
        elif variant == 1:
            krel = j - WIN_H // 2 + 0 * i
            r0rel = i - WIN_H // 2 + 0 * j
        else:
            krel = j - WIN_H + 0 * i
            r0rel = np.minimum(i - WIN_H // 2, 0) + 0 * j
        rowvalid = (krel >= r0rel) & (krel < r0rel + WIN_H)
        dr = np.clip(krel - i + (WIN_H - 1), 0, 2 * WIN_H - 2)
        dr_onehot[variant] = (dr[:, :, None] == np.arange(n_dr)) & rowvalid[:, :, None]
        rowvalid_all[variant] = rowvalid
    b = jnp.einsum("vijr,hrqk->vhiqjk", jnp.asarray(dr_onehot), toeplitz,
                   precision=lax.Precision.HIGHEST)
    valid = jnp.logical_and(jnp.asarray(rowvalid_all)[:, None, :, None, :, None],
                            jnp.asarray(colmask)[None, None, None, :, None, :])
    b = jnp.where(valid, b, NEG_INF)
    return b.reshape(3, H_ATTN, QB_TOK, WIN_TOK)


def _seg_sum(x, e2):
    stacked = jnp.concatenate([x[:, j * LANE:(j + 1) * LANE] for j in range(N_PAIR)], axis=0)
    res = jnp.dot(stacked, e2, preferred_element_type=F32, precision=lax.Precision.HIGHEST)
    return jnp.concatenate([res[j * CHUNK:(j + 1) * CHUNK] for j in range(N_PAIR)], axis=1)


def _rwkv_kernel(gblk_s, mblk_s, flag_s, *refs, reverse):
    del gblk_s, mblk_s
    if reverse:
        (gp_ref, mp_ref, mu_ref, w0_ref, w2_ref, a0_ref, a2_ref, kk_ref, ka_ref, rk_ref,
         gyb_ref, myb_ref, g2_ref, lg_ref, lb_ref, og_ref, om_ref, state, carry) = refs
    else:
        (gp_ref, mp_ref, mu_ref, w0_ref, w2_ref, a0_ref, a2_ref, kk_ref, ka_ref, rk_ref,
         og_ref, om_ref, state, carry) = refs
    step = pl.program_id(0)
    fl = flag_s[step]
    is_first = (fl & 1) == 1
    is_meta = (fl & 2) == 2

    @pl.when(is_first)
    def _():
        state[...] = jnp.zeros_like(state)
        carry[...] = jnp.zeros_like(carry)

    praw = jnp.where(is_meta, mp_ref[...], gp_ref[...])
    row = lax.broadcasted_iota(jnp.int32, (CHUNK, 1), 0)
    pad = jnp.logical_and(is_meta, row < PAD_ROWS)
    p_sh = jnp.where(pad, 0.0, praw[:, :N_SHIFT_P])
    gd = praw[:, N_SHIFT_P:]

    if reverse:
        rolled = pltpu.roll(p_sh, CHUNK - 1, axis=0)
        nb = jnp.where(row == CHUNK - 1, carry[0:1, :], rolled)
        carry[...] = p_sh[0:8, :]
    else:
        rolled = pltpu.roll(p_sh, 1, axis=0)
        nb = jnp.where(row == 0, carry[7:8, :], rolled)
        carry[...] = p_sh[CHUNK - 8:CHUNK, :]
    f = p_sh + (nb - p_sh) * mu_ref[...]
    r = f[:, 0:D_RWKV]
    k = f[:, D_RWKV:2 * D_RWKV]
    v = f[:, 2 * D_RWKV:3 * D_RWKV]
    wd = f[:, 3 * D_RWKV:3 * D_RWKV + R_LOW]
    ad = f[:, 3 * D_RWKV + R_LOW:3 * D_RWKV + 2 * R_LOW]

    ii = lax.broadcasted_iota(jnp.int32, (LANE, LANE), 0)
    jj = lax.broadcasted_iota(jnp.int32, (LANE, LANE), 1)
    same = (ii // CHUNK) == (jj // CHUNK)
    e2 = jnp.where(same, 1.0, 0.0).astype(F32)

    wl = w0_ref[...] + _dot(jnp.tanh(wd).astype(BF16), w2_ref[...])
    nwl = -wl
    softplus = jnp.maximum(nwl, 0.0) + jnp.log(1.0 + jnp.exp(-jnp.abs(nwl)))
    lw = -jnp.exp(-softplus - 0.5)
    a = 1.0 / (1.0 + jnp.exp(-(a0_ref[...] + _dot(ad.astype(BF16), a2_ref[...]))))
    kkr = k * kk_ref[...]
    kk = kkr / jnp.maximum(jnp.sqrt(_seg_sum(kkr * kkr, e2)), 1e-12)
    kmod = k * (1.0 + (a - 1.0) * ka_ref[...])
    bonus = _seg_sum(r * kmod * rk_ref[...], e2) * v

    ti = lax.broadcasted_iota(jnp.int32, (CHUNK, CHUNK), 0)
    tj = lax.broadcasted_iota(jnp.int32, (CHUNK, CHUNK), 1)
    tri = jnp.where((tj >= ti) if reverse else (tj <= ti), 1.0, 0.0).astype(F32)
    cum = jnp.dot(tri, lw, preferred_element_type=F32, precision=lax.Precision.HIGHEST)
    cum_end = cum[0:1, :] if reverse else cum[CHUNK - 1:CHUNK, :]
    e_pos = jnp.exp(cum)
    e_neg = jnp.exp(-cum)
    e_exc = jnp.exp(cum - lw)
    e_end = jnp.exp(cum_end - cum)
    bb = kk * a
    r_t = r * e_pos
    k_t = kmod * e_neg
    b_t = bb * e_neg
    a_t = -kk * e_exc
    k_h = kmod * e_end
    b_h = bb * e_end

    ti2 = ii % CHUNK
    tj2 = jj % CHUNK
    if reverse:
        strict = jnp.logical_and(same, tj2 > ti2)
        incl = jnp.logical_and(same, tj2 >= ti2)
    else:
        strict = jnp.logical_and(same, tj2 < ti2)
        incl = jnp.logical_and(same, tj2 <= ti2)
    eye = jnp.where(ii == jj, 1.0, 0.0).astype(F32)
    lane64 = lax.broadcasted_iota(jnp.int32, (1, CHUNK, LANE), 2) < HD_RWKV

    def tiles(x):
        return jnp.stack([x[:, j * LANE:(j + 1) * LANE] for j in range(N_PAIR)], axis=0)

    def stack_masked(x):
        return jnp.concatenate([jnp.where(lane64, x, 0.0), jnp.where(lane64, 0.0, x)], axis=1)

    def stack_dup(x):
        return jnp.concatenate([x, x], axis=1)

    v3 = tiles(v)
    a_s = stack_masked(tiles(a_t))
    r_s = stack_masked(tiles(r_t))
    bh_s = stack_masked(tiles(b_h))
    kh_s = stack_masked(tiles(k_h))
    v_s = stack_masked(v3)
    b_d = stack_dup(tiles(b_t))
    k_d = stack_dup(tiles(k_t))
    v_d = stack_dup(v3)
    m1 = _bmm_nt(jnp.concatenate([a_s, r_s], axis=1).astype(BF16),
                 jnp.concatenate([b_d, k_d], axis=1).astype(BF16))
    n_ab = jnp.where(strict, m1[:, :LANE, :LANE], 0.0)
    a_ak = jnp.where(strict, m1[:, :LANE, LANE:], 0.0)
    a_rb = jnp.where(incl, m1[:, LANE:, :LANE], 0.0)
    a_rk = jnp.where(incl, m1[:, LANE:, LANE:], 0.0)
    tinv = eye + n_ab
    npow = n_ab
    for _ in range(5):
        npb = npow.astype(BF16)
        npow = _bmm(npb, npb)
        tinv = tinv + _bmm(tinv.astype(BF16), npow.astype(BF16))
    gy = _bmm(jnp.concatenate([a_ak, a_rk], axis=1).astype(BF16), v_d.astype(BF16))
    g_s = jnp.where(same, gy[:, :LANE], 0.0)
    y0_s = jnp.where(same, gy[:, LANE:], 0.0)
    tz = _bmm(tinv.astype(BF16), jnp.concatenate([a_s, g_s], axis=2).astype(BF16))
    w_s = tz[:, :, :LANE]
    u0_s = tz[:, :, LANE:]
    s_add = _bmm_tn(kh_s.astype(BF16), v_s.astype(BF16))
    s_old = state[...]
    xr = _bmm(jnp.concatenate([w_s, r_s], axis=1).astype(BF16), s_old.astype(BF16))
    u_s = xr[:, :LANE] + u0_s
    y_s = xr[:, LANE:] + _bmm(a_rb.astype(BF16), u_s.astype(BF16)) + y0_s
    y3 = y_s[:, :CHUNK] + y_s[:, CHUNK:]
    g_end = jnp.exp(jnp.swapaxes(jnp.broadcast_to(tiles(cum_end), (N_PAIR, LANE, LANE)), 1, 2))
    state[...] = g_end * s_old + _bmm_tn(bh_s.astype(BF16), u_s.astype(BF16)) + s_add
    y = jnp.concatenate([y3[j] for j in range(N_PAIR)], axis=1)

    if not reverse:
        res = jnp.concatenate([y, bonus], axis=1)
    else:
        yb = jnp.where(is_meta, myb_ref[...], gyb_ref[...])
        yy = y + yb[:, :D_RWKV]
        mean = _seg_sum(yy, e2) * (1.0 / HD_RWKV)
        yc = yy - mean
        var = _seg_sum(yc * yc, e2) * (1.0 / HD_RWKV)
        yn = yc * lax.rsqrt(var + GN_EPS) * lg_ref[...] + lb_ref[...]
        gate = _dot((1.0 / (1.0 + jnp.exp(-gd))).astype(BF16), g2_ref[...])
        res = ((yn + yb[:, D_RWKV:] + bonus) * gate).astype(BF16)

    @pl.when(is_meta)
    def _():
        om_ref[...] = res

    @pl.when(jnp.logical_not(is_meta))
    def _():
        og_ref[...] = res


def _rwkv_scan(g_prw, m_prw, params, tabs, reverse, prev=None):
    gblk, mblk, flags = tabs
    nsteps = gblk.shape[0]
    gmap = lambda t, gb, mb, fl: (gb[t], 0)
    mmap = lambda t, gb, mb, fl: (mb[t], 0)
    cmap = lambda t, gb, mb, fl: (0, 0)
    vec = lambda n: pl.BlockSpec((1, n), cmap)
    in_specs = [
        pl.BlockSpec((CHUNK, N_RW), gmap),
        pl.BlockSpec((CHUNK, N_RW), mmap),
        vec(N_SHIFT_P), vec(D_RWKV),
        pl.BlockSpec((R_LOW, D_RWKV), cmap), vec(D_RWKV),
        pl.BlockSpec((R_LOW, D_RWKV), cmap), vec(D_RWKV), vec(D_RWKV), vec(D_RWKV),
    ]
    args = [g_prw, m_prw, params["mu"], params["w0"], params["w2"], params["a0"], params["a2"],
            params["k_k"], params["k_a"], params["r_k"]]
    if reverse:
        g_yb, m_yb = prev
        in_specs += [
            pl.BlockSpec((CHUNK, 2 * D_RWKV), gmap),
            pl.BlockSpec((CHUNK, 2 * D_RWKV), mmap),
            pl.BlockSpec((R_GATE, D_RWKV), cmap), vec(D_RWKV), vec(D_RWKV),
        ]
        args += [g_yb, m_yb, params["g2"], params["lnx_g"], params["lnx_b"]]
        width, dt = D_RWKV, BF16
    else:
        width, dt = 2 * D_RWKV, F32
    grid_spec = pltpu.PrefetchScalarGridSpec(
        num_scalar_prefetch=3,
        grid=(nsteps,),
        in_specs=in_specs,
        out_specs=[pl.BlockSpec((CHUNK, width), gmap), pl.BlockSpec((CHUNK, width), mmap)],
        scratch_shapes=[pltpu.VMEM((N_PAIR, LANE, LANE), F32), pltpu.VMEM((8, N_SHIFT_P), F32)],
    )
    return pl.pallas_call(
        functools.partial(_rwkv_kernel, reverse=reverse),
        grid_spec=grid_spec,
        out_shape=[jax.ShapeDtypeStruct((g_prw.shape[0], width), dt),
                   jax.ShapeDtypeStruct((m_prw.shape[0], width), dt)],
        compiler_params=pltpu.CompilerParams(
            dimension_semantics=("arbitrary",), vmem_limit_bytes=VMEM_LIMIT),
        name="rwkv_bwd" if reverse else "rwkv_fwd",
    )(gblk, mblk, flags, *args)


def _outproj_kernel(*refs, n_x, tiles_first):
    x_refs = refs[:n_x]
    oa_ref, or_ref, eg_ref, eb_ref, wa_ref, wr_ref, g1_ref, b1_ref, h1_ref, h1b_ref = refs[n_x:]
    h0 = _layer_norm(_select_rows(x_refs, tiles_first), eg_ref[...], eb_ref[...])
    mix = _dot(oa_ref[...], wa_ref[...]) + _dot(or_ref[...], wr_ref[...])
    h1 = _layer_norm(DEEPNORM_ALPHA * h0 + mix, g1_ref[...], b1_ref[...])
    h1_ref[...] = h1
    h1b_ref[...] = h1.astype(BF16)


def _outproj(o_attn, o_rwkv, xs, eg, eb, w_out_bf, g1, b1, tm):
    rows = o_attn.shape[0]
    x_specs, tiles_first = _row_specs(xs, tm, D_MODEL)
    cmap = lambda i: (0, 0)
    vec = pl.BlockSpec((1, D_MODEL), cmap)
    return pl.pallas_call(
        functools.partial(_outproj_kernel, n_x=len(xs), tiles_first=tiles_first),
        grid=(rows // tm,),
        in_specs=x_specs + [
            pl.BlockSpec((tm, D_ATTN), lambda i: (i, 0)),
            pl.BlockSpec((tm, D_RWKV), lambda i: (i, 0)),
            vec, vec,
            pl.BlockSpec((D_ATTN, D_MODEL), lambda i: (0, 0), pipeline_mode=pl.Buffered(1)),
            pl.BlockSpec((D_RWKV, D_MODEL), lambda i: (1, 0), pipeline_mode=pl.Buffered(1)),
            vec, vec,
        ],
        out_specs=[pl.BlockSpec((tm, D_MODEL), lambda i: (i, 0)),
                   pl.BlockSpec((tm, D_MODEL), lambda i: (i, 0))],
        out_shape=[jax.ShapeDtypeStruct((rows, D_MODEL), F32),
                   jax.ShapeDtypeStruct((rows, D_MODEL), BF16)],
        compiler_params=pltpu.CompilerParams(
            dimension_semantics=("arbitrary",), vmem_limit_bytes=VMEM_LIMIT),
        name="outproj_ln1",
    )(*xs, o_attn, o_rwkv, eg, eb, w_out_bf, w_out_bf, g1, b1)


def _ffn_in_kernel(pg_s, pm_s, nx_s, fl_s, h_ref, pg_ref, pm_ref, nx_ref, wg_ref, wu_ref,
                   cw_ref, cb_ref, act_ref, lhs, gate_s):
    del pg_s, pm_s, nx_s
    i = pl.program_id(0)
    c = pl.program_id(1)
    fl = fl_s[i]

    @pl.when(c == 0)
    def _():
        lhs[0:HALO, :] = jnp.where((fl & 1) == 1, pm_ref[...], pg_ref[...])
        lhs[HALO:HALO + TM_FFN, :] = h_ref[...]
        lhs[HALO + TM_FFN:, :] = jnp.where((fl & 2) == 2, jnp.zeros_like(nx_ref), nx_ref[...])

    for k in range(FC // FC_SUB):
        cols = slice(k * FC_SUB, (k + 1) * FC_SUB)
        gate_s[k] = _dot(lhs[...], wg_ref[:, cols])
        up = _dot(lhs[HALO:HALO + TM_FFN, :], wu_ref[:, cols])
        gate = (gate_s[k, HALO - 1:HALO - 1 + TM_FFN, :] * cw_ref[0:1, cols]
                + gate_s[k, HALO:HALO + TM_FFN, :] * cw_ref[1:2, cols]
                + gate_s[k, HALO + 1:HALO + 1 + TM_FFN, :] * cw_ref[2:3, cols] + cb_ref[:, cols])
        act = 0.5 * gate * (1.0 + lax.erf(gate * 0.7071067811865476)) * up
        act_ref[:, cols] = act.astype(BF16)


def _ffn_in(g_h1b, m_h1b, w_in_bf, conv_w, conv_b, tabs):
    pg, pm, nx, fl = tabs
    rows = g_h1b.shape[0]
    nfc = D_FF // FC
    grid_spec = pltpu.PrefetchScalarGridSpec(
        num_scalar_prefetch=4,
        grid=(rows // TM_FFN, nfc),
        in_specs=[
            pl.BlockSpec((TM_FFN, D_MODEL), lambda i, c, *_: (i, 0)),
            pl.BlockSpec((HALO, D_MODEL), lambda i, c, pg, pm, nx, fl: (pg[i], 0)),
            pl.BlockSpec((HALO, D_MODEL), lambda i, c, pg, pm, nx, fl: (pm[i], 0)),
            pl.BlockSpec((HALO, D_MODEL), lambda i, c, pg, pm, nx, fl: (nx[i], 0)),
            pl.BlockSpec((D_MODEL, FC), lambda i, c, *_: (0, c)),
            pl.BlockSpec((D_MODEL, FC), lambda i, c, *_: (0, nfc + c)),
            pl.BlockSpec((3, FC), lambda i, c, *_: (0, c)),
            pl.BlockSpec((1, FC), lambda i, c, *_: (0, c)),
        ],
        out_specs=pl.BlockSpec((TM_FFN, FC), lambda i, c, *_: (i, c)),
        scratch_shapes=[
            pltpu.VMEM((TM_FFN + 2 * HALO, D_MODEL), BF16),
            pltpu.VMEM((FC // FC_SUB, TM_FFN + 2 * HALO, FC_SUB), F32),
        ],
    )
    return pl.pallas_call(
        _ffn_in_kernel,
        grid_spec=grid_spec,
        out_shape=jax.ShapeDtypeStruct((rows, D_FF), BF16),
        compiler_params=pltpu.CompilerParams(
            dimension_semantics=("arbitrary", "arbitrary"), vmem_limit_bytes=VMEM_LIMIT),
        name="ffn_in",
    )(pg, pm, nx, fl, g_h1b, g_h1b, m_h1b, g_h1b, w_in_bf, w_in_bf, conv_w, conv_b)


def _ffn_out_kernel(act_ref, h_ref, wo_ref, g2_ref, b2_ref, *outs, tiles_first):
    y = _layer_norm(DEEPNORM_ALPHA * h_ref[...] + _dot(act_ref[...], wo_ref[...]),
                    g2_ref[...], b2_ref[...])
    if len(outs) == 1:
        outs[0][...] = y
    else:
        i = pl.program_id(0)

        @pl.when(i < tiles_first)
        def _():
            outs[0][...] = y

        @pl.when(i >= tiles_first)
        def _():
            outs[1][...] = y


def _ffn_out(act, g_h1, w_out_bf, g2, b2, out_rows):
    rows = act.shape[0]
    tm = TM_OUT
    tiles_first = out_rows[0] // tm
    cmap = lambda i: (0, 0)
    vec = pl.BlockSpec((1, D_MODEL), cmap)
    if len(out_rows) == 1:
        out_specs = [pl.BlockSpec((tm, D_MODEL), lambda i: (i, 0))]
    else:
        out_specs = [
            pl.BlockSpec((tm, D_MODEL), lambda i: (jnp.minimum(i, tiles_first - 1), 0)),
            pl.BlockSpec((tm, D_MODEL), lambda i: (jnp.maximum(i - tiles_first, 0), 0)),
        ]
    return pl.pallas_call(
        functools.partial(_ffn_out_kernel, tiles_first=tiles_first),
        grid=(rows // tm,),
        in_specs=[
            pl.BlockSpec((tm, D_FF), lambda i: (i, 0)),
            pl.BlockSpec((tm, D_MODEL), lambda i: (i, 0)),
            pl.BlockSpec((D_FF, D_MODEL), cmap, pipeline_mode=pl.Buffered(1)),
            vec, vec,
        ],
        out_specs=out_specs,
        out_shape=[jax.ShapeDtypeStruct((r, D_MODEL), F32) for r in out_rows],
        compiler_params=pltpu.CompilerParams(
            dimension_semantics=("arbitrary",), vmem_limit_bytes=VMEM_LIMIT),
        name="ffn_out_ln2",
    )(act, g_h1, w_out_bf, g2, b2)


def _attn_tables(seq_lens):
    rec = []
    base = 0
    for s, t_len in enumerate(seq_lens):
        rows = t_len // GRID_W
        for qb in range(rows // QB_ROWS):
            w0 = min(max(qb * QB_ROWS - WIN_H // 2, 0), rows - 2 * QB_ROWS)
            variant = 0 if qb == 0 else (2 if qb == rows // QB_ROWS - 1 else 1)
            rec.append((variant, (base + qb * QB_TOK) // QB_TOK, (base + w0 * GRID_W) // KV_BLK, s))
        base += t_len
    rec.sort(key=lambda x: x[0])
    arr = np.asarray(rec, np.int32)
    return tuple(jnp.asarray(arr[:, c]) for c in (1, 2, 0, 3))


def _rwkv_tables(seq_lens, reverse):
    gblk, mblk, flags = [], [], []
    base = 0
    for s, t_len in enumerate(seq_lens):
        n = t_len // CHUNK
        first_blk = base // CHUNK
        steps = [(first_blk, 2)] + [(first_blk + c, 0) for c in range(n)]
        if reverse:
            steps = steps[::-1]
        for idx, (blk, fl) in enumerate(steps):
            gblk.append(blk)
            mblk.append(s)
            flags.append(fl | (1 if idx == 0 else 0))
        base += t_len
    return tuple(jnp.asarray(np.asarray(a, np.int32)) for a in (gblk, mblk, flags))


def _ffn_tables(seq_lens):
    pg, pm, nx, fl = [], [], [], []
    base = 0
    total = sum(seq_lens)
    for s, t_len in enumerate(seq_lens):
        for t in range(t_len // TM_FFN):
            r0 = base + t * TM_FFN
            first = t == 0
            last = t == t_len // TM_FFN - 1
            pg.append(max(r0 // HALO - 1, 0))
            pm.append((s * CHUNK + CHUNK - HALO) // HALO)
            nx.append(min((r0 + TM_FFN) // HALO, total // HALO - 1))
            fl.append((1 if first else 0) | (2 if last else 0))
        base += t_len
    return tuple(jnp.asarray(np.asarray(a, np.int32)) for a in (pg, pm, nx, fl))


def _pad_cols(w, n):
    return jnp.pad(w, ((0, 0), (0, n - w.shape[1])))


def _encode_all(xs, meta_tokens, emb_ln_g, emb_ln_b, w_in, attn_rpb, rwkv_mu, rwkv_w0, rwkv_w2,
                rwkv_a0, rwkv_a2, rwkv_g2, rwkv_k_k, rwkv_k_a, rwkv_r_k, rwkv_lnx_g, rwkv_lnx_b,
                w_out, ln1_g, ln1_b, ffn_w_in, ffn_conv_w, ffn_conv_b, ffn_w_out, ln2_g, ln2_b):
    seq_lens = []
    for x in xs:
        seq_lens += [x.shape[1]] * x.shape[0]
    nseq = len(seq_lens)
    x2d = [x.reshape(-1, D_MODEL) for x in xs]
    meta_blk = jnp.concatenate([jnp.zeros((PAD_ROWS, D_MODEL), F32), meta_tokens.astype(F32)], axis=0)
    x_m = jnp.tile(meta_blk, (nseq, 1))

    o1 = N_QKV
    o2 = o1 + 3 * D_RWKV
    o3 = o2 + R_DECAY
    o4 = o3 + R_ICLR
    wl = w_in[0]
    w_qkv = wl[:, :o1].astype(BF16)
    w_rw = jnp.concatenate(
        [wl[:, o1:o2], _pad_cols(wl[:, o2:o3], R_LOW), _pad_cols(wl[:, o3:o4], R_LOW), wl[:, o4:]],
        axis=1).astype(BF16)
    eg = emb_ln_g.reshape(1, D_MODEL)
    eb = emb_ln_b.reshape(1, D_MODEL)
    mu = rwkv_mu[0]
    o5 = 3 * D_RWKV
    mu_p = jnp.concatenate(
        [mu[:, :o5], _pad_cols(mu[:, o5:o5 + R_DECAY], R_LOW), _pad_cols(mu[:, o5 + R_DECAY:], R_LOW)],
        axis=1)
    pad_rows = lambda w: jnp.pad(w, ((0, R_LOW - w.shape[0]), (0, 0))).astype(BF16)
    row = lambda v: v.reshape(1, -1)

    def dir_params(d):
        return dict(mu=mu_p[d:d + 1], w0=row(rwkv_w0[0, d]), w2=pad_rows(rwkv_w2[0, d]),
                    a0=row(rwkv_a0[0, d]), a2=pad_rows(rwkv_a2[0, d]),
                    k_k=row(rwkv_k_k[0]), k_a=row(rwkv_k_a[0]), r_k=row(rwkv_r_k[0]),
                    g2=rwkv_g2[0].astype(BF16), lnx_g=row(rwkv_lnx_g[0]), lnx_b=row(rwkv_lnx_b[0]))

    w_out_bf = w_out[0].astype(BF16)
    ffn_w_in_bf = ffn_w_in[0].astype(BF16)
    ffn_w_out_bf = ffn_w_out[0].astype(BF16)
    bias = _attn_bias_table(attn_rpb[0])

    g_qkv = _inproj(x2d, eg, eb, w_qkv, BF16, True, TM)
    g_prw = _inproj(x2d, eg, eb, w_rw, F32, False, TM)
    m_qkv = _inproj([x_m], eg, eb, w_qkv, BF16, True, x_m.shape[0])
    m_prw = _inproj([x_m], eg, eb, w_rw, F32, False, x_m.shape[0])

    g_oattn = _attention(g_qkv, m_qkv, bias, _attn_tables(seq_lens))
    m_oattn = _meta_attention(m_qkv)
    g_yb, m_yb = _rwkv_scan(g_prw, m_prw, dir_params(0), _rwkv_tables(seq_lens, False), False)
    g_orwkv, m_orwkv = _rwkv_scan(g_prw, m_prw, dir_params(1), _rwkv_tables(seq_lens, True), True,
                                  prev=(g_yb, m_yb))

    g1 = ln1_g.reshape(1, D_MODEL)
    b1 = ln1_b.reshape(1, D_MODEL)
    g_h1, g_h1b = _outproj(g_oattn, g_orwkv, x2d, eg, eb, w_out_bf, g1, b1, TM)
    _, m_h1b = _outproj(m_oattn, m_orwkv, [x_m], eg, eb, w_out_bf, g1, b1, x_m.shape[0])

    out_rows = [x.shape[0] * x.shape[1] for x in xs]
    act = _ffn_in(g_h1b, m_h1b, ffn_w_in_bf, ffn_conv_w[0], ffn_conv_b.reshape(1, D_FF),
                  _ffn_tables(seq_lens))
    ys = _ffn_out(act, g_h1, ffn_w_out_bf, ln2_g.reshape(1, D_MODEL), ln2_b.reshape(1, D_MODEL),
                  out_rows)
    return [y.reshape(x.shape) for y, x in zip(ys, xs)]


def kernel(x_prompt, x_sample, meta_tokens, emb_ln_g, emb_ln_b, w_in, attn_rpb, rwkv_mu, rwkv_w0, rwkv_w2, rwkv_a0, rwkv_a2, rwkv_g2, rwkv_k_k, rwkv_k_a, rwkv_r_k, rwkv_lnx_g, rwkv_lnx_b, w_out, ln1_g, ln1_b, ffn_w_in, ffn_conv_w, ffn_conv_b, ffn_w_out, ln2_g, ln2_b):
    y_prompt, y_sample = _encode_all(
        [x_prompt, x_sample], meta_tokens, emb_ln_g, emb_ln_b, w_in, attn_rpb, rwkv_mu, rwkv_w0,
        rwkv_w2, rwkv_a0, rwkv_a2, rwkv_g2, rwkv_k_k, rwkv_k_a, rwkv_r_k, rwkv_lnx_g, rwkv_lnx_b,
        w_out, ln1_g, ln1_b, ffn_w_in, ffn_conv_w, ffn_conv_b, ffn_w_out, ln2_g, ln2_b)
    return (y_prompt, y_sample)
```

```python
import functools

import numpy as np
import jax
import jax.numpy as jnp
from jax import lax
from jax.experimental import pallas as pl
from jax.experimental.pallas import tpu as pltpu

D_MODEL = 2048
N_META = 16
GRID_W = 64
WIN_H = 8
WIN_W = 16
D_ATTN = 1024
HD_ATTN = 128
H_ATTN = D_ATTN // HD_ATTN
D_RWKV = 1024
HD_RWKV = 64
R_DECAY = 96
R_ICLR = 96
R_GATE = 256
D_FF = 5632
DEEPNORM_ALPHA = 2.0 ** 0.25
LN_EPS = 1e-5
GN_EPS = 64e-5
NEG_INF = -1e30

LANE = 128
CHUNK = 64
PAD_ROWS = CHUNK - N_META
R_LOW = 128
N_SHIFT_P = 3 * D_RWKV + 2 * R_LOW
N_RW = N_SHIFT_P + R_GATE
N_QKV = 3 * D_ATTN
N_PAIR = D_RWKV // LANE
QB_ROWS = 8
QB_TOK = QB_ROWS * GRID_W
KV_BLK = 256
WIN_TOK = 4 * KV_BLK
TM = 512
TM_FFN = 1024
TM_SLAB = 512
TM_OUT = 256
FC = 512
FC_SUB = 256
HALO = 16
VMEM_LIMIT = 56 * 1024 * 1024

F32 = jnp.float32
BF16 = jnp.bfloat16


def _layer_norm(x, g, b):
    mu = jnp.mean(x, axis=-1, keepdims=True)
    xc = x - mu
    var = jnp.mean(xc * xc, axis=-1, keepdims=True)
    return xc * lax.rsqrt(var + LN_EPS) * g + b


def _dot(a, b):
    return jnp.dot(a, b, preferred_element_type=F32)


def _dot_nt(a, b):
    return lax.dot_general(a, b, (((1,), (1,)), ((), ())), preferred_element_type=F32)


def _bmm(a, b):
    return jnp.einsum("bik,bkj->bij", a, b, preferred_element_type=F32)


def _bmm_nt(a, b):
    return jnp.einsum("bik,bjk->bij", a, b, preferred_element_type=F32)


def _bmm_tn(a, b):
    return jnp.einsum("bki,bkj->bij", a, b, preferred_element_type=F32)


def _select_rows(x_refs, tiles_first):
    if len(x_refs) == 1:
        return x_refs[0][...]
    return jnp.where(pl.program_id(0) < tiles_first, x_refs[0][...], x_refs[1][...])


def _row_specs(xs, tm, width):
    if len(xs) == 1:
        return [pl.BlockSpec((tm, width), lambda i: (i, 0))], 0
    tiles_first = xs[0].shape[0] // tm
    return [pl.BlockSpec((tm, width), lambda i: (jnp.minimum(i, tiles_first - 1), 0)),
            pl.BlockSpec((tm, width), lambda i: (jnp.maximum(i - tiles_first, 0), 0))], tiles_first


def _inproj_kernel(*refs, n_x, tiles_first, q_scale):
    x_refs = refs[:n_x]
    g_ref, b_ref, w_ref, o_ref = refs[n_x:]
    h = _layer_norm(_select_rows(x_refs, tiles_first), g_ref[...], b_ref[...]).astype(BF16)
    acc = _dot(h, w_ref[...])
    if q_scale:
        o_ref[:, :D_ATTN] = (acc[:, :D_ATTN] * (HD_ATTN ** -0.5)).astype(o_ref.dtype)
        o_ref[:, D_ATTN:] = acc[:, D_ATTN:].astype(o_ref.dtype)
    else:
        o_ref[...] = acc.astype(o_ref.dtype)


def _inproj(xs, ln_g, ln_b, w, out_dtype, q_scale, tm):
    rows = sum(x.shape[0] for x in xs)
    n_out = w.shape[1]
    x_specs, tiles_first = _row_specs(xs, tm, D_MODEL)
    cmap = lambda i: (0, 0)
    return pl.pallas_call(
        functools.partial(_inproj_kernel, n_x=len(xs), tiles_first=tiles_first, q_scale=q_scale),
        grid=(rows // tm,),
        in_specs=x_specs + [
            pl.BlockSpec((1, D_MODEL), cmap),
            pl.BlockSpec((1, D_MODEL), cmap),
            pl.BlockSpec((D_MODEL, n_out), cmap, pipeline_mode=pl.Buffered(1)),
        ],
        out_specs=pl.BlockSpec((tm, n_out), lambda i: (i, 0)),
        out_shape=jax.ShapeDtypeStruct((rows, n_out), out_dtype),
        compiler_params=pltpu.CompilerParams(
            dimension_semantics=("arbitrary",), vmem_limit_bytes=VMEM_LIMIT),
        name="inproj_qkv" if q_scale else "inproj_rwkv",
    )(*xs, ln_g, ln_b, w)


def _attn_kernel(qblk_s, kvblk_s, var_s, mseq_s, q_ref, k0, k1, k2, k3, v0, v1, v2, v3,
                 mk_ref, mv_ref, bias_ref, o_ref):
    del qblk_s, kvblk_s, var_s, mseq_s
    q = q_ref[...]
    ks = (k0, k1, k2, k3)
    vs = (v0, v1, v2, v3)
    s = [_dot_nt(q, ks[c][...]) + bias_ref[0, 0, :, c * KV_BLK:(c + 1) * KV_BLK] for c in range(4)]
    km = mk_ref[PAD_ROWS:, :]
    vm = mv_ref[PAD_ROWS:, :]
    sm = _dot_nt(q, km)
    m = jnp.max(sm, axis=-1, keepdims=True)
    for c in range(4):
        m = jnp.maximum(m, jnp.max(s[c], axis=-1, keepdims=True))
    pm = jnp.exp(sm - m)
    l = jnp.sum(pm, axis=-1, keepdims=True)
    o = _dot(pm.astype(BF16), vm)
    for c in range(4):
        p = jnp.exp(s[c] - m)
        l = l + jnp.sum(p, axis=-1, keepdims=True)
        o = o + _dot(p.astype(BF16), vs[c][...])
    o_ref[...] = (o / l).astype(BF16)


def _attention(g_qkv, m_qkv, bias, tabs):
    qblk, kvblk, var, mseq = tabs
    nsteps = qblk.shape[0]
    rows = g_qkv.shape[0]

    def kv_spec(col0, c):
        return pl.BlockSpec((KV_BLK, HD_ATTN),
                            lambda h, t, qb, kb, vr, ms: (kb[t] + c, col0 + h))

    in_specs = [pl.BlockSpec((QB_TOK, HD_ATTN), lambda h, t, qb, kb, vr, ms: (qb[t], h))]
    in_specs += [kv_spec(H_ATTN, c) for c in range(4)]
    in_specs += [kv_spec(2 * H_ATTN, c) for c in range(4)]
    in_specs += [
        pl.BlockSpec((CHUNK, HD_ATTN), lambda h, t, qb, kb, vr, ms: (ms[t], H_ATTN + h)),
        pl.BlockSpec((CHUNK, HD_ATTN), lambda h, t, qb, kb, vr, ms: (ms[t], 2 * H_ATTN + h)),
        pl.BlockSpec((1, 1, QB_TOK, WIN_TOK), lambda h, t, qb, kb, vr, ms: (vr[t], h, 0, 0)),
    ]
    grid_spec = pltpu.PrefetchScalarGridSpec(
        num_scalar_prefetch=4,
        grid=(H_ATTN, nsteps),
        in_specs=in_specs,
        out_specs=pl.BlockSpec((QB_TOK, HD_ATTN), lambda h, t, qb, kb, vr, ms: (qb[t], h)),
    )
    return pl.pallas_call(
        _attn_kernel,
        grid_spec=grid_spec,
        out_shape=jax.ShapeDtypeStruct((rows, D_ATTN), BF16),
        compiler_params=pltpu.CompilerParams(
            dimension_semantics=("arbitrary", "arbitrary"), vmem_limit_bytes=VMEM_LIMIT),
        name="nbr_attention",
    )(qblk, kvblk, var, mseq, g_qkv, *([g_qkv] * 8), m_qkv, m_qkv, bias)


def _meta_attn_kernel(q_ref, k_ref, v_ref, o_ref):
    q = q_ref[...]
    km = k_ref[PAD_ROWS:, :]
    vm = v_ref[PAD_ROWS:, :]
    s = _dot_nt(q, km)
    m = jnp.max(s, axis=-1, keepdims=True)
    p = jnp.exp(s - m)
    l = jnp.sum(p, axis=-1, keepdims=True)
    o_ref[...] = (_dot(p.astype(BF16), vm) / l).astype(BF16)


def _meta_attention(m_qkv):
    nseq = m_qkv.shape[0] // CHUNK
    return pl.pallas_call(
        _meta_attn_kernel,
        grid=(nseq, H_ATTN),
        in_specs=[
            pl.BlockSpec((CHUNK, HD_ATTN), lambda s, h: (s, h)),
            pl.BlockSpec((CHUNK, HD_ATTN), lambda s, h: (s, H_ATTN + h)),
            pl.BlockSpec((CHUNK, HD_ATTN), lambda s, h: (s, 2 * H_ATTN + h)),
        ],
        out_specs=pl.BlockSpec((CHUNK, HD_ATTN), lambda s, h: (s, h)),
        out_shape=jax.ShapeDtypeStruct((m_qkv.shape[0], D_ATTN), BF16),
        compiler_params=pltpu.CompilerParams(dimension_semantics=("arbitrary", "arbitrary")),
        name="meta_attention",
    )(m_qkv, m_qkv, m_qkv)


def _attn_bias_table(rpb):
    qc = np.arange(GRID_W)
    kc = np.arange(GRID_W)
    c0 = np.clip(qc - WIN_W // 2, 0, GRID_W - WIN_W)
    colmask = (kc[None, :] >= c0[:, None]) & (kc[None, :] < c0[:, None] + WIN_W)
    dc = np.clip(kc[None, :] - qc[:, None], -(WIN_W - 1), WIN_W - 1) + (WIN_W - 1)
    n_dr = 2 * WIN_H - 1
    n_dc = 2 * WIN_W - 1
    dc_onehot = (dc[None] == np.arange(n_dc)[:, None, None]).astype(np.float32)
    toeplitz = jnp.einsum("hrd,dqk->hrqk", rpb, jnp.asarray(dc_onehot),
                          precision=lax.Precision.HIGHEST)
    i = np.arange(QB_ROWS)[:, None]
    j = np.arange(2 * QB_ROWS)[None, :]
    dr_onehot = np.zeros((3, QB_ROWS, 2 * QB_ROWS, n_dr), np.float32)
    rowvalid_all = np.zeros((3, QB_ROWS, 2 * QB_ROWS), bool)
    for variant in range(3):
        if variant == 0:
            krel = j + 0 * i
            r0rel = np.maximum(i - WIN_H // 2, 0) + 0 * j
        elif variant == 1:
            krel = j - WIN_H // 2 + 0 * i
            r0rel = i - WIN_H // 2 + 0 * j
        else:
            krel = j - WIN_H + 0 * i
            r0rel = np.minimum(i - WIN_H // 2, 0) + 0 * j
        rowvalid = (krel >= r0rel) & (krel < r0rel + WIN_H)
        dr = np.clip(krel - i + (WIN_H - 1), 0, 2 * WIN_H - 2)
        dr_onehot[variant] = (dr[:, :, None] == np.arange(n_dr)) & rowvalid[:, :, None]
        rowvalid_all[variant] = rowvalid
    b = jnp.einsum("vijr,hrqk->vhiqjk", jnp.asarray(dr_onehot), toeplitz,
                   precision=lax.Precision.HIGHEST)
    valid = jnp.logical_and(jnp.asarray(rowvalid_all)[:, None, :, None, :, None],
                            jnp.asarray(colmask)[None, None, None, :, None, :])
    b = jnp.where(valid, b, NEG_INF)
    return b.reshape(3, H_ATTN, QB_TOK, WIN_TOK)


def _split_bf16(x, n):
    parts = []
    rem = x
    for _ in range(n):
        p = rem.astype(BF16)
        parts.append(p)
        rem = rem - p.astype(F32)
    return parts


def _seg_sum(x, e2):
    stacked = jnp.concatenate([x[:, j * LANE:(j + 1) * LANE] for j in range(N_PAIR)], axis=0)
    hi, lo = _split_bf16(stacked, 2)
    res = _dot(hi, e2) + _dot(lo, e2)
    return jnp.concatenate([res[j * CHUNK:(j + 1) * CHUNK] for j in range(N_PAIR)], axis=1)


_OP_A, _OP_R, _OP_BH, _OP_KH, _OP_V, _OP_B, _OP_K = range(7)
N_OPND = 7


def _rwkv_kernel(gblk_s, mblk_s, flag_s, *refs, reverse):
    del gblk_s, mblk_s
    if reverse:
        (gp_ref, mp_ref, mu_ref, w0_ref, w2_ref, a0_ref, a2_ref, kk_ref, ka_ref, rk_ref,
         gyb_ref, myb_ref, g2_ref, lg_ref, lb_ref, og_ref, om_ref,
         state, carry, opnd, aux, cend, opnd_n, aux_n, cend_n) = refs
    else:
        (gp_ref, mp_ref, mu_ref, w0_ref, w2_ref, a0_ref, a2_ref, kk_ref, ka_ref, rk_ref,
         og_ref, om_ref, state, carry, opnd, aux, cend, opnd_n, aux_n, cend_n) = refs
    step = pl.program_id(0)
    fl = flag_s[step]
    fl_prev = flag_s[jnp.maximum(step - 1, 0)]
    is_first = (fl & 1) == 1
    is_meta = (fl & 2) == 2
    prev_first = (fl_prev & 1) == 1
    prev_meta = (fl_prev & 2) == 2

    @pl.when(step == 0)
    def _():
        opnd[...] = jnp.zeros_like(opnd)
        aux[...] = jnp.zeros_like(aux)
        cend[...] = jnp.zeros_like(cend)

    @pl.when(is_first)
    def _():
        carry[...] = jnp.zeros_like(carry)

    @pl.when(prev_first)
    def _():
        state[...] = jnp.zeros_like(state)

    ii = lax.broadcasted_iota(jnp.int32, (LANE, LANE), 0)
    jj = lax.broadcasted_iota(jnp.int32, (LANE, LANE), 1)
    same = (ii // CHUNK) == (jj // CHUNK)
    e2 = jnp.where(same, 1.0, 0.0).astype(BF16)

    ti2 = ii % CHUNK
    tj2 = jj % CHUNK
    if reverse:
        strict = jnp.logical_and(same, tj2 > ti2)
        incl = jnp.logical_and(same, tj2 >= ti2)
    else:
        strict = jnp.logical_and(same, tj2 < ti2)
        incl = jnp.logical_and(same, tj2 <= ti2)
    eye = jnp.where(ii == jj, 1.0, 0.0).astype(F32)
    lane64 = lax.broadcasted_iota(jnp.int32, (1, CHUNK, LANE), 2) < HD_RWKV

    def tiles(x):
        return jnp.stack([x[:, j * LANE:(j + 1) * LANE] for j in range(N_PAIR)], axis=0)

    def stack_masked(x):
        zero = jnp.zeros_like(x)
        return jnp.concatenate([jnp.where(lane64, x, zero), jnp.where(lane64, zero, x)], axis=1)

    def stack_dup(x):
        return jnp.concatenate([x, x], axis=1)

    def prepare():
        praw = jnp.where(is_meta, mp_ref[...], gp_ref[...])
        row = lax.broadcasted_iota(jnp.int32, (CHUNK, 1), 0)
        pad = jnp.logical_and(is_meta, row < PAD_ROWS)
        p_sh = jnp.where(pad, 0.0, praw[:, :N_SHIFT_P])
        gd = praw[:, N_SHIFT_P:]
        if reverse:
            rolled = pltpu.roll(p_sh, CHUNK - 1, axis=0)
            nb = jnp.where(row == CHUNK - 1, carry[0:1, :], rolled)
            carry[...] = p_sh[0:8, :]
        else:
            rolled = pltpu.roll(p_sh, 1, axis=0)
            nb = jnp.where(row == 0, carry[7:8, :], rolled)
            carry[...] = p_sh[CHUNK - 8:CHUNK, :]
        f = p_sh + (nb - p_sh) * mu_ref[...]
        r = f[:, 0:D_RWKV]
        k = f[:, D_RWKV:2 * D_RWKV]
        v = f[:, 2 * D_RWKV:3 * D_RWKV]
        wd = f[:, 3 * D_RWKV:3 * D_RWKV + R_LOW]
        ad = f[:, 3 * D_RWKV + R_LOW:3 * D_RWKV + 2 * R_LOW]
        opnd_n[_OP_V] = v.astype(BF16)
        yield
        wl = w0_ref[...] + _dot(jnp.tanh(wd).astype(BF16), w2_ref[...])
        nwl = -wl
        softplus = jnp.maximum(nwl, 0.0) + jnp.log(1.0 + jnp.exp(-jnp.abs(nwl)))
        lw = -jnp.exp(-softplus - 0.5)
        yield
        a = 1.0 / (1.0 + jnp.exp(-(a0_ref[...] + _dot(ad.astype(BF16), a2_ref[...]))))
        kmod = k * (1.0 + (a - 1.0) * ka_ref[...])
        yield
        kkr = k * kk_ref[...]
        kk = kkr / jnp.maximum(jnp.sqrt(_seg_sum(kkr * kkr, e2)), 1e-12)
        yield
        aux_n[0] = _seg_sum(r * kmod * rk_ref[...], e2) * v
        if reverse:
            aux_n[1] = _dot((1.0 / (1.0 + jnp.exp(-gd))).astype(BF16), g2_ref[...])
        yield
        ti = lax.broadcasted_iota(jnp.int32, (CHUNK, CHUNK), 0)
        tj = lax.broadcasted_iota(jnp.int32, (CHUNK, CHUNK), 1)
        tri = jnp.where((tj >= ti) if reverse else (tj <= ti), 1.0, 0.0).astype(BF16)
        cum = sum(_dot(tri, part) for part in _split_bf16(lw, 3))
        cum_end = cum[0:1, :] if reverse else cum[CHUNK - 1:CHUNK, :]
        cend_n[0:1, :] = cum_end
        yield
        opnd_n[_OP_R] = (r * jnp.exp(cum)).astype(BF16)
        yield
        e_neg = jnp.exp(-cum)
        bb = kk * a
        opnd_n[_OP_K] = (kmod * e_neg).astype(BF16)
        opnd_n[_OP_B] = (bb * e_neg).astype(BF16)
        yield
        opnd_n[_OP_A] = (-kk * jnp.exp(cum - lw)).astype(BF16)
        yield
        e_end = jnp.exp(cum_end - cum)
        opnd_n[_OP_KH] = (kmod * e_end).astype(BF16)
        opnd_n[_OP_BH] = (bb * e_end).astype(BF16)

    prep_steps = prepare()

    def emit_prep():
        next(prep_steps, None)

    v3 = tiles(opnd[_OP_V])
    a_s = stack_masked(tiles(opnd[_OP_A]))
    r_s = stack_masked(tiles(opnd[_OP_R]))
    bh_s = stack_masked(tiles(opnd[_OP_BH]))
    kh_s = stack_masked(tiles(opnd[_OP_KH]))
    v_s = stack_masked(v3)
    b_d = stack_dup(tiles(opnd[_OP_B]))
    k_d = stack_dup(tiles(opnd[_OP_K]))
    v_d = stack_dup(v3)
    cum_end_prev = cend[0:1, :]
    emit_prep()
    m1 = _bmm_nt(jnp.concatenate([a_s, r_s], axis=1),
                 jnp.concatenate([b_d, k_d], axis=1))
    emit_prep()
    n_ab = jnp.where(strict, m1[:, :LANE, :LANE], 0.0)
    a_ak = jnp.where(strict, m1[:, :LANE, LANE:], 0.0)
    a_rb = jnp.where(incl, m1[:, LANE:, :LANE], 0.0)
    a_rk = jnp.where(incl, m1[:, LANE:, LANE:], 0.0)
    tinv = eye + n_ab
    npow = n_ab
    for _ in range(5):
        npb = npow.astype(BF16)
        npow = _bmm(npb, npb)
        emit_prep()
        tinv = tinv + _bmm(tinv.astype(BF16), npow.astype(BF16))
    gy = _bmm(jnp.concatenate([a_ak, a_rk], axis=1).astype(BF16), v_d)
    emit_prep()
    g_s = jnp.where(same, gy[:, :LANE], 0.0)
    y0_s = jnp.where(same, gy[:, LANE:], 0.0)
    tz = _bmm(tinv.astype(BF16), jnp.concatenate([a_s, g_s.astype(BF16)], axis=2))
    emit_prep()
    w_s = tz[:, :, :LANE]
    u0_s = tz[:, :, LANE:]
    s_add = _bmm_tn(kh_s, v_s)
    s_old = state[...]
    xr = _bmm(jnp.concatenate([w_s.astype(BF16), r_s], axis=1), s_old.astype(BF16))
    emit_prep()
    u_s = xr[:, :LANE] + u0_s
    y_s = xr[:, LANE:] + _bmm(a_rb.astype(BF16), u_s.astype(BF16)) + y0_s
    emit_prep()
    y3 = y_s[:, :CHUNK] + y_s[:, CHUNK:]
    g_end = jnp.exp(jnp.swapaxes(jnp.broadcast_to(tiles(cum_end_prev), (N_PAIR, LANE, LANE)), 1, 2))
    state[...] = g_end * s_old + _bmm_tn(bh_s, u_s.astype(BF16)) + s_add
    y = jnp.concatenate([y3[j] for j in range(N_PAIR)], axis=1)

    if not reverse:
        res = jnp.concatenate([y, aux[0]], axis=1)
    else:
        yb = jnp.where(prev_meta, myb_ref[...], gyb_ref[...])
        yy = y + yb[:, :D_RWKV]
        mean = _seg_sum(yy, e2) * (1.0 / HD_RWKV)
        yc = yy - mean
        var = _seg_sum(yc * yc, e2) * (1.0 / HD_RWKV)
        yn = yc * lax.rsqrt(var + GN_EPS) * lg_ref[...] + lb_ref[...]
        res = ((yn + yb[:, D_RWKV:] + aux[0]) * aux[1]).astype(BF16)

    for _ in prep_steps:
        pass
    opnd[...] = opnd_n[...]
    aux[...] = aux_n[...]
    cend[0:1, :] = cend_n[0:1, :]

    @pl.when(prev_meta)
    def _():
        om_ref[...] = res

    @pl.when(jnp.logical_not(prev_meta))
    def _():
        og_ref[...] = res


def _rwkv_scan(g_prw, m_prw, params, tabs, reverse, prev=None):
    gblk, mblk, flags = tabs
    nsteps = gblk.shape[0]
    gmap = lambda t, gb, mb, fl: (gb[t], 0)
    mmap = lambda t, gb, mb, fl: (mb[t], 0)
    gmap_prev = lambda t, gb, mb, fl: (gb[jnp.maximum(t - 1, 0)], 0)
    mmap_prev = lambda t, gb, mb, fl: (mb[jnp.maximum(t - 1, 0)], 0)
    cmap = lambda t, gb, mb, fl: (0, 0)
    vec = lambda n: pl.BlockSpec((1, n), cmap)
    in_specs = [
        pl.BlockSpec((CHUNK, N_RW), gmap),
        pl.BlockSpec((CHUNK, N_RW), mmap),
        vec(N_SHIFT_P), vec(D_RWKV),
        pl.BlockSpec((R_LOW, D_RWKV), cmap), vec(D_RWKV),
        pl.BlockSpec((R_LOW, D_RWKV), cmap), vec(D_RWKV), vec(D_RWKV), vec(D_RWKV),
    ]
    args = [g_prw, m_prw, params["mu"], params["w0"], params["w2"], params["a0"], params["a2"],
            params["k_k"], params["k_a"], params["r_k"]]
    if reverse:
        g_yb, m_yb = prev
        in_specs += [
            pl.BlockSpec((CHUNK, 2 * D_RWKV), gmap_prev),
            pl.BlockSpec((CHUNK, 2 * D_RWKV), mmap_prev),
            pl.BlockSpec((R_GATE, D_RWKV), cmap), vec(D_RWKV), vec(D_RWKV),
        ]
        args += [g_yb, m_yb, params["g2"], params["lnx_g"], params["lnx_b"]]
        width, dt, n_aux = D_RWKV, BF16, 2
    else:
        width, dt, n_aux = 2 * D_RWKV, F32, 1
    grid_spec = pltpu.PrefetchScalarGridSpec(
        num_scalar_prefetch=3,
        grid=(nsteps,),
        in_specs=in_specs,
        out_specs=[pl.BlockSpec((CHUNK, width), gmap_prev),
                   pl.BlockSpec((CHUNK, width), mmap_prev)],
        scratch_shapes=[
            pltpu.VMEM((N_PAIR, LANE, LANE), F32),
            pltpu.VMEM((8, N_SHIFT_P), F32),
            pltpu.VMEM((N_OPND, CHUNK, D_RWKV), BF16),
            pltpu.VMEM((n_aux, CHUNK, D_RWKV), F32),
            pltpu.VMEM((8, D_RWKV), F32),
            pltpu.VMEM((N_OPND, CHUNK, D_RWKV), BF16),
            pltpu.VMEM((n_aux, CHUNK, D_RWKV), F32),
            pltpu.VMEM((8, D_RWKV), F32),
        ],
    )
    return pl.pallas_call(
        functools.partial(_rwkv_kernel, reverse=reverse),
        grid_spec=grid_spec,
        out_shape=[jax.ShapeDtypeStruct((g_prw.shape[0], width), dt),
                   jax.ShapeDtypeStruct((m_prw.shape[0], width), dt)],
        compiler_params=pltpu.CompilerParams(
            dimension_semantics=("arbitrary",), vmem_limit_bytes=VMEM_LIMIT),
        name="rwkv_bwd" if reverse else "rwkv_fwd",
    )(gblk, mblk, flags, *args)


def _outproj_kernel(*refs, n_x, tiles_first):
    x_refs = refs[:n_x]
    oa_ref, or_ref, eg_ref, eb_ref, wa_ref, wr_ref, g1_ref, b1_ref, h1_ref, h1b_ref = refs[n_x:]
    h0 = _layer_norm(_select_rows(x_refs, tiles_first), eg_ref[...], eb_ref[...])
    mix = _dot(oa_ref[...], wa_ref[...]) + _dot(or_ref[...], wr_ref[...])
    h1 = _layer_norm(DEEPNORM_ALPHA * h0 + mix, g1_ref[...], b1_ref[...])
    h1_ref[...] = h1
    h1b_ref[...] = h1.astype(BF16)


def _outproj(o_attn, o_rwkv, xs, eg, eb, w_out_bf, g1, b1, tm):
    rows = o_attn.shape[0]
    x_specs, tiles_first = _row_specs(xs, tm, D_MODEL)
    cmap = lambda i: (0, 0)
    vec = pl.BlockSpec((1, D_MODEL), cmap)
    return pl.pallas_call(
        functools.partial(_outproj_kernel, n_x=len(xs), tiles_first=tiles_first),
        grid=(rows // tm,),
        in_specs=x_specs + [
            pl.BlockSpec((tm, D_ATTN), lambda i: (i, 0)),
            pl.BlockSpec((tm, D_RWKV), lambda i: (i, 0)),
            vec, vec,
            pl.BlockSpec((D_ATTN, D_MODEL), lambda i: (0, 0), pipeline_mode=pl.Buffered(1)),
            pl.BlockSpec((D_RWKV, D_MODEL), lambda i: (1, 0), pipeline_mode=pl.Buffered(1)),
            vec, vec,
        ],
        out_specs=[pl.BlockSpec((tm, D_MODEL), lambda i: (i, 0)),
                   pl.BlockSpec((tm, D_MODEL), lambda i: (i, 0))],
        out_shape=[jax.ShapeDtypeStruct((rows, D_MODEL), F32),
                   jax.ShapeDtypeStruct((rows, D_MODEL), BF16)],
        compiler_params=pltpu.CompilerParams(
            dimension_semantics=("arbitrary",), vmem_limit_bytes=VMEM_LIMIT),
        name="outproj_ln1",
    )(*xs, o_attn, o_rwkv, eg, eb, w_out_bf, w_out_bf, g1, b1)


def _ffn_in_kernel(pg_s, pm_s, nx_s, fl_s, h_ref, pg_ref, pm_ref, nx_ref, wg_ref, wu_ref,
                   cw_ref, cb_ref, act_ref, lhs, gate_s, up_s):
    del pg_s, pm_s, nx_s
    i = pl.program_id(0)
    c = pl.program_id(1)
    fl = fl_s[i]

    @pl.when(c == 0)
    def _():
        lhs[0:HALO, :] = jnp.where((fl & 1) == 1, pm_ref[...], pg_ref[...])
        lhs[HALO:HALO + TM_FFN, :] = h_ref[...]
        lhs[HALO + TM_FFN:, :] = jnp.where((fl & 2) == 2, jnp.zeros_like(nx_ref), nx_ref[...])

    units = [(s, k) for s in range(TM_FFN // TM_SLAB) for k in range(FC // FC_SUB)]

    def matmuls(u):
        s, k = units[u]
        cols = slice(k * FC_SUB, (k + 1) * FC_SUB)
        r0 = s * TM_SLAB
        gate_s[u] = _dot(lhs[r0:r0 + TM_SLAB + 2 * HALO, :], wg_ref[:, cols])
        up_s[u] = _dot(lhs[r0 + HALO:r0 + HALO + TM_SLAB, :], wu_ref[:, cols])

    def activation(u):
        s, k = units[u]
        cols = slice(k * FC_SUB, (k + 1) * FC_SUB)
        gate = (gate_s[u, HALO - 1:HALO - 1 + TM_SLAB, :] * cw_ref[0:1, cols]
                + gate_s[u, HALO:HALO + TM_SLAB, :] * cw_ref[1:2, cols]
                + gate_s[u, HALO + 1:HALO + 1 + TM_SLAB, :] * cw_ref[2:3, cols] + cb_ref[:, cols])
        act = 0.5 * gate * (1.0 + lax.erf(gate * 0.7071067811865476)) * up_s[u]
        act_ref[s * TM_SLAB:(s + 1) * TM_SLAB, cols] = act.astype(BF16)

    matmuls(0)
    for u in range(len(units)):
        if u + 1 < len(units):
            matmuls(u + 1)
        activation(u)


def _ffn_in(g_h1b, m_h1b, w_in_bf, conv_w, conv_b, tabs):
    pg, pm, nx, fl = tabs
    rows = g_h1b.shape[0]
    nfc = D_FF // FC
    n_units = (TM_FFN // TM_SLAB) * (FC // FC_SUB)
    grid_spec = pltpu.PrefetchScalarGridSpec(
        num_scalar_prefetch=4,
        grid=(rows // TM_FFN, nfc),
        in_specs=[
            pl.BlockSpec((TM_FFN, D_MODEL), lambda i, c, *_: (i, 0)),
            pl.BlockSpec((HALO, D_MODEL), lambda i, c, pg, pm, nx, fl: (pg[i], 0)),
            pl.BlockSpec((HALO, D_MODEL), lambda i, c, pg, pm, nx, fl: (pm[i], 0)),
            pl.BlockSpec((HALO, D_MODEL), lambda i, c, pg, pm, nx, fl: (nx[i], 0)),
            pl.BlockSpec((D_MODEL, FC), lambda i, c, *_: (0, c)),
            pl.BlockSpec((D_MODEL, FC), lambda i, c, *_: (0, nfc + c)),
            pl.BlockSpec((3, FC), lambda i, c, *_: (0, c)),
            pl.BlockSpec((1, FC), lambda i, c, *_: (0, c)),
        ],
        out_specs=pl.BlockSpec((TM_FFN, FC), lambda i, c, *_: (i, c)),
        scratch_shapes=[
            pltpu.VMEM((TM_FFN + 2 * HALO, D_MODEL), BF16),
            pltpu.VMEM((n_units, TM_SLAB + 2 * HALO, FC_SUB), F32),
            pltpu.VMEM((n_units, TM_SLAB, FC_SUB), F32),
        ],
    )
    return pl.pallas_call(
        _ffn_in_kernel,
        grid_spec=grid_spec,
        out_shape=jax.ShapeDtypeStruct((rows, D_FF), BF16),
        compiler_params=pltpu.CompilerParams(
            dimension_semantics=("arbitrary", "arbitrary"), vmem_limit_bytes=VMEM_LIMIT),
        name="ffn_in",
    )(pg, pm, nx, fl, g_h1b, g_h1b, m_h1b, g_h1b, w_in_bf, w_in_bf, conv_w, conv_b)


def _ffn_out_kernel(act_ref, h_ref, wo_ref, g2_ref, b2_ref, *outs, tiles_first):
    y = _layer_norm(DEEPNORM_ALPHA * h_ref[...] + _dot(act_ref[...], wo_ref[...]),
                    g2_ref[...], b2_ref[...])
    if len(outs) == 1:
        outs[0][...] = y
    else:
        i = pl.program_id(0)

        @pl.when(i < tiles_first)
        def _():
            outs[0][...] = y

        @pl.when(i >= tiles_first)
        def _():
            outs[1][...] = y


def _ffn_out(act, g_h1, w_out_bf, g2, b2, out_rows):
    rows = act.shape[0]
    tm = TM_OUT
    tiles_first = out_rows[0] // tm
    cmap = lambda i: (0, 0)
    vec = pl.BlockSpec((1, D_MODEL), cmap)
    if len(out_rows) == 1:
        out_specs = [pl.BlockSpec((tm, D_MODEL), lambda i: (i, 0))]
    else:
        out_specs = [
            pl.BlockSpec((tm, D_MODEL), lambda i: (jnp.minimum(i, tiles_first - 1), 0)),
            pl.BlockSpec((tm, D_MODEL), lambda i: (jnp.maximum(i - tiles_first, 0), 0)),
        ]
    return pl.pallas_call(
        functools.partial(_ffn_out_kernel, tiles_first=tiles_first),
        grid=(rows // tm,),
        in_specs=[
            pl.BlockSpec((tm, D_FF), lambda i: (i, 0)),
            pl.BlockSpec((tm, D_MODEL), lambda i: (i, 0)),
            pl.BlockSpec((D_FF, D_MODEL), cmap, pipeline_mode=pl.Buffered(1)),
            vec, vec,
        ],
        out_specs=out_specs,
        out_shape=[jax.ShapeDtypeStruct((r, D_MODEL), F32) for r in out_rows],
        compiler_params=pltpu.CompilerParams(
            dimension_semantics=("arbitrary",), vmem_limit_bytes=VMEM_LIMIT),
        name="ffn_out_ln2",
    )(act, g_h1, w_out_bf, g2, b2)


def _attn_tables(seq_lens):
    rec = []
    base = 0
    for s, t_len in enumerate(seq_lens):
        rows = t_len // GRID_W
        for qb in range(rows // QB_ROWS):
            w0 = min(max(qb * QB_ROWS - WIN_H // 2, 0), rows - 2 * QB_ROWS)
            variant = 0 if qb == 0 else (2 if qb == rows // QB_ROWS - 1 else 1)
            rec.append((variant, (base + qb * QB_TOK) // QB_TOK, (base + w0 * GRID_W) // KV_BLK, s))
        base += t_len
    rec.sort(key=lambda x: x[0])
    arr = np.asarray(rec, np.int32)
    return tuple(jnp.asarray(arr[:, c]) for c in (1, 2, 0, 3))


def _rwkv_tables(seq_lens, reverse):
    gblk, mblk, flags = [], [], []
    base = 0
    for s, t_len in enumerate(seq_lens):
        n = t_len // CHUNK
        first_blk = base // CHUNK
        steps = [(first_blk, 2)] + [(first_blk + c, 0) for c in range(n)]
        if reverse:
            steps = steps[::-1]
        for idx, (blk, fl) in enumerate(steps):
            gblk.append(blk)
            mblk.append(s)
            flags.append(fl | (1 if idx == 0 else 0))
        base += t_len
    gblk.append(gblk[-1])
    mblk.append(mblk[-1])
    flags.append(0)
    return tuple(jnp.asarray(np.asarray(a, np.int32)) for a in (gblk, mblk, flags))


def _ffn_tables(seq_lens):
    pg, pm, nx, fl = [], [], [], []
    base = 0
    total = sum(seq_lens)
    for s, t_len in enumerate(seq_lens):
        for t in range(t_len // TM_FFN):
            r0 = base + t * TM_FFN
            first = t == 0
            last = t == t_len // TM_FFN - 1
            pg.append(max(r0 // HALO - 1, 0))
            pm.append((s * CHUNK + CHUNK - HALO) // HALO)
            nx.append(min((r0 + TM_FFN) // HALO, total // HALO - 1))
            fl.append((1 if first else 0) | (2 if last else 0))
        base += t_len
    return tuple(jnp.asarray(np.asarray(a, np.int32)) for a in (pg, pm, nx, fl))


def _pad_cols(w, n):
    return jnp.pad(w, ((0, 0), (0, n - w.shape[1])))


def _encode_all(xs, meta_tokens, emb_ln_g, emb_ln_b, w_in, attn_rpb, rwkv_mu, rwkv_w0, rwkv_w2,
                rwkv_a0, rwkv_a2, rwkv_g2, rwkv_k_k, rwkv_k_a, rwkv_r_k, rwkv_lnx_g, rwkv_lnx_b,
                w_out, ln1_g, ln1_b, ffn_w_in, ffn_conv_w, ffn_conv_b, ffn_w_out, ln2_g, ln2_b):
    seq_lens = []
    for x in xs:
        seq_lens += [x.shape[1]] * x.shape[0]
    nseq = len(seq_lens)
    x2d = [x.reshape(-1, D_MODEL) for x in xs]
    meta_blk = jnp.concatenate([jnp.zeros((PAD_ROWS, D_MODEL), F32), meta_tokens.astype(F32)], axis=0)
    x_m = jnp.tile(meta_blk, (nseq, 1))

    o1 = N_QKV
    o2 = o1 + 3 * D_RWKV
    o3 = o2 + R_DECAY
    o4 = o3 + R_ICLR
    wl = w_in[0]
    w_qkv = wl[:, :o1].astype(BF16)
    w_rw = jnp.concatenate(
        [wl[:, o1:o2], _pad_cols(wl[:, o2:o3], R_LOW), _pad_cols(wl[:, o3:o4], R_LOW), wl[:, o4:]],
        axis=1).astype(BF16)
    eg = emb_ln_g.reshape(1, D_MODEL)
    eb = emb_ln_b.reshape(1, D_MODEL)
    mu = rwkv_mu[0]
    o5 = 3 * D_RWKV
    mu_p = jnp.concatenate(
        [mu[:, :o5], _pad_cols(mu[:, o5:o5 + R_DECAY], R_LOW), _pad_cols(mu[:, o5 + R_DECAY:], R_LOW)],
        axis=1)
    pad_rows = lambda w: jnp.pad(w, ((0, R_LOW - w.shape[0]), (0, 0))).astype(BF16)
    row = lambda v: v.reshape(1, -1)

    def dir_params(d):
        return dict(mu=mu_p[d:d + 1], w0=row(rwkv_w0[0, d]), w2=pad_rows(rwkv_w2[0, d]),
                    a0=row(rwkv_a0[0, d]), a2=pad_rows(rwkv_a2[0, d]),
                    k_k=row(rwkv_k_k[0]), k_a=row(rwkv_k_a[0]), r_k=row(rwkv_r_k[0]),
                    g2=rwkv_g2[0].astype(BF16), lnx_g=row(rwkv_lnx_g[0]), lnx_b=row(rwkv_lnx_b[0]))

    w_out_bf = w_out[0].astype(BF16)
    ffn_w_in_bf = ffn_w_in[0].astype(BF16)
    ffn_w_out_bf = ffn_w_out[0].astype(BF16)
    bias = _attn_bias_table(attn_rpb[0])

    g_qkv = _inproj(x2d, eg, eb, w_qkv, BF16, True, TM)
    g_prw = _inproj(x2d, eg, eb, w_rw, F32, False, TM)
    m_qkv = _inproj([x_m], eg, eb, w_qkv, BF16, True, x_m.shape[0])
    m_prw = _inproj([x_m], eg, eb, w_rw, F32, False, x_m.shape[0])

    g_oattn = _attention(g_qkv, m_qkv, bias, _attn_tables(seq_lens))
    m_oattn = _meta_attention(m_qkv)
    g_yb, m_yb = _rwkv_scan(g_prw, m_prw, dir_params(0), _rwkv_tables(seq_lens, False), False)
    g_orwkv, m_orwkv = _rwkv_scan(g_prw, m_prw, dir_params(1), _rwkv_tables(seq_lens, True), True,
                                  prev=(g_yb, m_yb))

    g1 = ln1_g.reshape(1, D_MODEL)
    b1 = ln1_b.reshape(1, D_MODEL)
    g_h1, g_h1b = _outproj(g_oattn, g_orwkv, x2d, eg, eb, w_out_bf, g1, b1, TM)
    _, m_h1b = _outproj(m_oattn, m_orwkv, [x_m], eg, eb, w_out_bf, g1, b1, x_m.shape[0])

    out_rows = [x.shape[0] * x.shape[1] for x in xs]
    act = _ffn_in(g_h1b, m_h1b, ffn_w_in_bf, ffn_conv_w[0], ffn_conv_b.reshape(1, D_FF),
                  _ffn_tables(seq_lens))
    ys = _ffn_out(act, g_h1, ffn_w_out_bf, ln2_g.reshape(1, D_MODEL), ln2_b.reshape(1, D_MODEL),
                  out_rows)
    return [y.reshape(x.shape) for y, x in zip(ys, xs)]


def kernel(x_prompt, x_sample, meta_tokens, emb_ln_g, emb_ln_b, w_in, attn_rpb, rwkv_mu, rwkv_w0, rwkv_w2, rwkv_a0, rwkv_a2, rwkv_g2, rwkv_k_k, rwkv_k_a, rwkv_r_k, rwkv_lnx_g, rwkv_lnx_b, w_out, ln1_g, ln1_b, ffn_w_in, ffn_conv_w, ffn_conv_b, ffn_w_out, ln2_g, ln2_b):
    y_prompt, y_sample = _encode_all(
        [x_prompt, x_sample], meta_tokens, emb_ln_g, emb_ln_b, w_in, attn_rpb, rwkv_mu, rwkv_w0,
        rwkv_w2, rwkv_a0, rwkv_a2, rwkv_g2, rwkv_k_k, rwkv_k_a, rwkv_r_k, rwkv_lnx_g, rwkv_lnx_b,
        w_out, ln1_g, ln1_b, ffn_w_in, ffn_conv_w, ffn_conv_b, ffn_w_out, ln2_g, ln2_b)
    return (y_prompt, y_sample)
```

```python
import functools

import numpy as np
import jax
import jax.numpy as jnp
from jax import lax
from jax.experimental import pallas as pl
from jax.experimental.pallas import tpu as pltpu

D_MODEL = 2048
N_META = 16
GRID_W = 64
WIN_H = 8
WIN_W = 16
D_ATTN = 1024
HD_ATTN = 128
H_ATTN = D_ATTN // HD_ATTN
D_RWKV = 1024
HD_RWKV = 64
R_DECAY = 96
R_ICLR = 96
R_GATE = 256
D_FF = 5632
DEEPNORM_ALPHA = 2.0 ** 0.25
LN_EPS = 1e-5
GN_EPS = 64e-5
NEG_INF = -1e30

LANE = 128
CHUNK = 64
PAD_ROWS = CHUNK - N_META
R_LOW = 128
N_SHIFT_P = 3 * D_RWKV + 2 * R_LOW
N_RW = N_SHIFT_P + R_GATE
N_QKV = 3 * D_ATTN
N_PAIR = D_RWKV // LANE
QB_ROWS = 8
QB_TOK = QB_ROWS * GRID_W
KV_BLK = 256
WIN_TOK = 4 * KV_BLK
TM = 512
TM_FFN = 1024
TM_SLAB = 512
TM_OUT = 256
SLAB_OUT = 128
N_SLAB = 4
FC = 512
FC_SUB = 256
HALO = 16
VMEM_LIMIT = 60 * 1024 * 1024

F32 = jnp.float32
BF16 = jnp.bfloat16


def _layer_norm(x, g, b):
    mu = jnp.mean(x, axis=-1, keepdims=True)
    xc = x - mu
    var = jnp.mean(xc * xc, axis=-1, keepdims=True)
    return xc * lax.rsqrt(var + LN_EPS) * g + b


def _dot(a, b):
    return jnp.dot(a, b, preferred_element_type=F32)


def _dot_nt(a, b):
    return lax.dot_general(a, b, (((1,), (1,)), ((), ())), preferred_element_type=F32)


def _bmm(a, b):
    return jnp.einsum("bik,bkj->bij", a, b, preferred_element_type=F32)


def _bmm_nt(a, b):
    return jnp.einsum("bik,bjk->bij", a, b, preferred_element_type=F32)


def _bmm_tn(a, b):
    return jnp.einsum("bki,bkj->bij", a, b, preferred_element_type=F32)


def _select_rows(x_refs, tiles_first, rows):
    if len(x_refs) == 1:
        return x_refs[0][rows, :]
    return jnp.where(pl.program_id(0) < tiles_first, x_refs[0][rows, :], x_refs[1][rows, :])


def _emit_skewed(n_slabs, stages):
    depth = max(lag for lag, _ in stages)
    for k in range(n_slabs + depth):
        for lag, fn in stages:
            if 0 <= k - lag < n_slabs:
                fn(k - lag)


def _row_specs(xs, tm, width):
    if len(xs) == 1:
        return [pl.BlockSpec((tm, width), lambda i: (i, 0))], 0
    tiles_first = xs[0].shape[0] // tm
    return [pl.BlockSpec((tm, width), lambda i: (jnp.minimum(i, tiles_first - 1), 0)),
            pl.BlockSpec((tm, width), lambda i: (jnp.maximum(i - tiles_first, 0), 0))], tiles_first


def _inproj_kernel(*refs, n_x, tiles_first, q_scale):
    x_refs = refs[:n_x]
    g_ref, b_ref, w_ref, o_ref = refs[n_x:]
    slab = o_ref.shape[0] // N_SLAB
    rows = lambda s: slice(s * slab, (s + 1) * slab)
    h = {}

    def norm(s):
        h[s] = _layer_norm(_select_rows(x_refs, tiles_first, rows(s)), g_ref[...], b_ref[...]).astype(BF16)

    def project(s):
        acc = _dot(h.pop(s), w_ref[...])
        if q_scale:
            o_ref[rows(s), :D_ATTN] = (acc[:, :D_ATTN] * (HD_ATTN ** -0.5)).astype(o_ref.dtype)
            o_ref[rows(s), D_ATTN:] = acc[:, D_ATTN:].astype(o_ref.dtype)
        else:
            o_ref[rows(s), :] = acc.astype(o_ref.dtype)

    _emit_skewed(N_SLAB, [(1, project), (0, norm)])


def _inproj(xs, ln_g, ln_b, w, out_dtype, q_scale, tm):
    rows = sum(x.shape[0] for x in xs)
    n_out = w.shape[1]
    x_specs, tiles_first = _row_specs(xs, tm, D_MODEL)
    cmap = lambda i: (0, 0)
    return pl.pallas_call(
        functools.partial(_inproj_kernel, n_x=len(xs), tiles_first=tiles_first, q_scale=q_scale),
        grid=(rows // tm,),
        in_specs=x_specs + [
            pl.BlockSpec((1, D_MODEL), cmap),
            pl.BlockSpec((1, D_MODEL), cmap),
            pl.BlockSpec((D_MODEL, n_out), cmap, pipeline_mode=pl.Buffered(1)),
        ],
        out_specs=pl.BlockSpec((tm, n_out), lambda i: (i, 0)),
        out_shape=jax.ShapeDtypeStruct((rows, n_out), out_dtype),
        compiler_params=pltpu.CompilerParams(
            dimension_semantics=("arbitrary",), vmem_limit_bytes=VMEM_LIMIT),
        name="inproj_qkv" if q_scale else "inproj_rwkv",
    )(*xs, ln_g, ln_b, w)


def _window_rows(variant, i):
    half = WIN_H // 2
    if variant == 0:
        off, r0 = 0, max(i - half, 0)
    elif variant == 1:
        off, r0 = -half, i - half
    else:
        off, r0 = -WIN_H, min(i - half, 0)
    return [j for j in range(2 * QB_ROWS) if r0 <= j + off < r0 + WIN_H], off


def _attn_tile_plan(variant):
    plan = []
    for i in range(QB_ROWS):
        rows, off = _window_rows(variant, i)
        tiles = []
        for jt in range(QB_ROWS):
            first, second = 2 * jt in rows, 2 * jt + 1 in rows
            if not (first or second):
                continue
            kind = 0 if (first and second) else (1 if second else 2)
            e = 2 * jt + off - i + WIN_H
            assert (not first or 0 <= e - 1 <= 2 * WIN_H - 2) and (not second or 0 <= e <= 2 * WIN_H - 2)
            tiles.append((jt, kind, e))
        plan.append(tiles)
    return plan


def _attn_kernel(qblk_s, kvblk_s, var_s, mseq_s, q_ref, k_ref, v_ref,
                 mk_ref, mv_ref, bt_ref, o_ref, p_scr):
    del qblk_s, kvblk_s, mseq_s
    variant = var_s[pl.program_id(1)]
    q = q_ref[...]
    s_win = _dot_nt(q, k_ref[...])
    km = mk_ref[PAD_ROWS:, :]
    vm = mv_ref[PAD_ROWS:, :]
    sm = _dot_nt(q, km)

    def softmax_pv(plan):
        def score_tile(i, jt, kind, e):
            rows = slice(i * GRID_W, (i + 1) * GRID_W)
            return s_win[rows, jt * LANE:(jt + 1) * LANE] + bt_ref[0, kind, e]

        tile_max = []
        for i, tiles in enumerate(plan):
            t = None
            for jt, kind, e in tiles:
                x = score_tile(i, jt, kind, e)
                t = x if t is None else jnp.maximum(t, x)
            tile_max.append(t)
        m = jnp.maximum(jnp.max(jnp.concatenate(tile_max, axis=0), axis=-1, keepdims=True),
                        jnp.max(sm, axis=-1, keepdims=True))
        pm = jnp.exp(sm - m)
        p_scr[...] = jnp.zeros_like(p_scr)
        tile_sum = []
        for i, tiles in enumerate(plan):
            rows = slice(i * GRID_W, (i + 1) * GRID_W)
            mi = m[rows]
            t = None
            for jt, kind, e in tiles:
                p = jnp.exp(score_tile(i, jt, kind, e) - mi)
                p_scr[rows, jt * LANE:(jt + 1) * LANE] = p.astype(BF16)
                t = p if t is None else t + p
            tile_sum.append(t)
        l = (jnp.sum(jnp.concatenate(tile_sum, axis=0), axis=-1, keepdims=True)
             + jnp.sum(pm, axis=-1, keepdims=True))
        o = _dot(pm.astype(BF16), vm)
        o = o + _dot(p_scr[...], v_ref[...])
        o_ref[...] = (o / l).astype(BF16)

    for v in range(3):
        pl.when(variant == v)(functools.partial(softmax_pv, _attn_tile_plan(v)))


def _attention(g_qkv, m_qkv, bias, tabs):
    qblk, kvblk, var, mseq = tabs
    nsteps = qblk.shape[0]
    rows = g_qkv.shape[0]

    def window_spec(col0):
        return pl.BlockSpec((pl.Element(WIN_TOK), pl.Element(HD_ATTN)),
                            lambda h, t, qb, kb, vr, ms: (kb[t] * KV_BLK, (col0 + h) * HD_ATTN))

    in_specs = [pl.BlockSpec((QB_TOK, HD_ATTN), lambda h, t, qb, kb, vr, ms: (qb[t], h)),
                window_spec(H_ATTN), window_spec(2 * H_ATTN)]
    in_specs += [
        pl.BlockSpec((CHUNK, HD_ATTN), lambda h, t, qb, kb, vr, ms: (ms[t], H_ATTN + h)),
        pl.BlockSpec((CHUNK, HD_ATTN), lambda h, t, qb, kb, vr, ms: (ms[t], 2 * H_ATTN + h)),
        pl.BlockSpec((1, 3, 2 * WIN_H, GRID_W, LANE), lambda h, t, qb, kb, vr, ms: (h, 0, 0, 0, 0)),
    ]
    grid_spec = pltpu.PrefetchScalarGridSpec(
        num_scalar_prefetch=4,
        grid=(H_ATTN, nsteps),
        in_specs=in_specs,
        out_specs=pl.BlockSpec((QB_TOK, HD_ATTN), lambda h, t, qb, kb, vr, ms: (qb[t], h)),
        scratch_shapes=[pltpu.VMEM((QB_TOK, WIN_TOK), BF16)],
    )
    return pl.pallas_call(
        _attn_kernel,
        grid_spec=grid_spec,
        out_shape=jax.ShapeDtypeStruct((rows, D_ATTN), BF16),
        compiler_params=pltpu.CompilerParams(
            dimension_semantics=("arbitrary", "arbitrary"), vmem_limit_bytes=VMEM_LIMIT),
        name="nbr_attention",
    )(qblk, kvblk, var, mseq, g_qkv, g_qkv, g_qkv, m_qkv, m_qkv, bias)


def _meta_attn_kernel(q_ref, k_ref, v_ref, o_ref):
    q = q_ref[...]
    km = k_ref[PAD_ROWS:, :]
    vm = v_ref[PAD_ROWS:, :]
    s = _dot_nt(q, km)
    m = jnp.max(s, axis=-1, keepdims=True)
    p = jnp.exp(s - m)
    l = jnp.sum(p, axis=-1, keepdims=True)
    o_ref[...] = (_dot(p.astype(BF16), vm) / l).astype(BF16)


def _meta_attention(m_qkv):
    nseq = m_qkv.shape[0] // CHUNK
    return pl.pallas_call(
        _meta_attn_kernel,
        grid=(nseq, H_ATTN),
        in_specs=[
            pl.BlockSpec((CHUNK, HD_ATTN), lambda s, h: (s, h)),
            pl.BlockSpec((CHUNK, HD_ATTN), lambda s, h: (s, H_ATTN + h)),
            pl.BlockSpec((CHUNK, HD_ATTN), lambda s, h: (s, 2 * H_ATTN + h)),
        ],
        out_specs=pl.BlockSpec((CHUNK, HD_ATTN), lambda s, h: (s, h)),
        out_shape=jax.ShapeDtypeStruct((m_qkv.shape[0], D_ATTN), BF16),
        compiler_params=pltpu.CompilerParams(dimension_semantics=("arbitrary", "arbitrary")),
        name="meta_attention",
    )(m_qkv, m_qkv, m_qkv)


def _attn_bias_tiles(rpb):
    qc = np.arange(GRID_W)
    kc = np.arange(GRID_W)
    c0 = np.clip(qc - WIN_W // 2, 0, GRID_W - WIN_W)
    colmask = (kc[None, :] >= c0[:, None]) & (kc[None, :] < c0[:, None] + WIN_W)
    dc = np.clip(kc[None, :] - qc[:, None], -(WIN_W - 1), WIN_W - 1) + (WIN_W - 1)
    n_dr = 2 * WIN_H - 1
    n_dc = 2 * WIN_W - 1
    dc_onehot = (dc[None] == np.arange(n_dc)[:, None, None]).astype(np.float32)
    toeplitz = jnp.einsum("hrd,dqk->hrqk", rpb, jnp.asarray(dc_onehot),
                          precision=lax.Precision.HIGHEST)
    toeplitz = jnp.where(jnp.asarray(colmask)[None, None], toeplitz, NEG_INF)
    masked = jnp.full((H_ATTN, 1, GRID_W, GRID_W), NEG_INF, F32)
    padded = jnp.concatenate([masked, toeplitz, masked], axis=1)
    first = padded[:, 0:n_dr + 1]
    second = padded[:, 1:n_dr + 2]
    off = jnp.full_like(first, NEG_INF)
    kinds = [jnp.concatenate([a, b], axis=-1) for a, b in ((first, second), (off, second), (first, off))]
    return jnp.stack(kinds, axis=1)


def _split_bf16(x, n):
    parts = []
    rem = x
    for _ in range(n):
        p = rem.astype(BF16)
        parts.append(p)
        rem = rem - p.astype(F32)
    return parts


def _seg_sum(x, e2):
    stacked = jnp.concatenate([x[:, j * LANE:(j + 1) * LANE] for j in range(N_PAIR)], axis=0)
    hi, lo = _split_bf16(stacked, 2)
    res = _dot(hi, e2) + _dot(lo, e2)
    return jnp.concatenate([res[j * CHUNK:(j + 1) * CHUNK] for j in range(N_PAIR)], axis=1)


_OP_A, _OP_R, _OP_BH, _OP_KH, _OP_V, _OP_B, _OP_K = range(7)
N_OPND = 7


def _rwkv_kernel(gblk_s, mblk_s, flag_s, *refs, reverse):
    del gblk_s, mblk_s
    if reverse:
        (gp_ref, mp_ref, mu_ref, w0_ref, w2_ref, a0_ref, a2_ref, kk_ref, ka_ref, rk_ref,
         gyb_ref, myb_ref, g2_ref, lg_ref, lb_ref, og_ref, om_ref,
         state, carry, opnd, aux, cend, opnd_n, aux_n, cend_n) = refs
    else:
        (gp_ref, mp_ref, mu_ref, w0_ref, w2_ref, a0_ref, a2_ref, kk_ref, ka_ref, rk_ref,
         og_ref, om_ref, state, carry, opnd, aux, cend, opnd_n, aux_n, cend_n) = refs
    step = pl.program_id(0)
    fl = flag_s[step]
    fl_prev = flag_s[jnp.maximum(step - 1, 0)]
    is_first = (fl & 1) == 1
    is_meta = (fl & 2) == 2
    prev_first = (fl_prev & 1) == 1
    prev_meta = (fl_prev & 2) == 2

    @pl.when(step == 0)
    def _():
        opnd[...] = jnp.zeros_like(opnd)
        aux[...] = jnp.zeros_like(aux)
        cend[...] = jnp.zeros_like(cend)

    @pl.when(is_first)
    def _():
        carry[...] = jnp.zeros_like(carry)

    @pl.when(prev_first)
    def _():
        state[...] = jnp.zeros_like(state)

    ii = lax.broadcasted_iota(jnp.int32, (LANE, LANE), 0)
    jj = lax.broadcasted_iota(jnp.int32, (LANE, LANE), 1)
    same = (ii // CHUNK) == (jj // CHUNK)
    e2 = jnp.where(same, 1.0, 0.0).astype(BF16)

    ti2 = ii % CHUNK
    tj2 = jj % CHUNK
    if reverse:
        strict = jnp.logical_and(same, tj2 > ti2)
        incl = jnp.logical_and(same, tj2 >= ti2)
    else:
        strict = jnp.logical_and(same, tj2 < ti2)
        incl = jnp.logical_and(same, tj2 <= ti2)
    eye = jnp.where(ii == jj, 1.0, 0.0).astype(F32)
    lane64 = lax.broadcasted_iota(jnp.int32, (1, CHUNK, LANE), 2) < HD_RWKV

    def tiles(x):
        return jnp.stack([x[:, j * LANE:(j + 1) * LANE] for j in range(N_PAIR)], axis=0)

    def stack_masked(x):
        zero = jnp.zeros_like(x)
        return jnp.concatenate([jnp.where(lane64, x, zero), jnp.where(lane64, zero, x)], axis=1)

    def stack_dup(x):
        return jnp.concatenate([x, x], axis=1)

    def prepare():
        praw = jnp.where(is_meta, mp_ref[...], gp_ref[...])
        row = lax.broadcasted_iota(jnp.int32, (CHUNK, 1), 0)
        pad = jnp.logical_and(is_meta, row < PAD_ROWS)
        p_sh = jnp.where(pad, 0.0, praw[:, :N_SHIFT_P])
        gd = praw[:, N_SHIFT_P:]
        if reverse:
            rolled = pltpu.roll(p_sh, CHUNK - 1, axis=0)
            nb = jnp.where(row == CHUNK - 1, carry[0:1, :], rolled)
            carry[...] = p_sh[0:8, :]
        else:
            rolled = pltpu.roll(p_sh, 1, axis=0)
            nb = jnp.where(row == 0, carry[7:8, :], rolled)
            carry[...] = p_sh[CHUNK - 8:CHUNK, :]
        f = p_sh + (nb - p_sh) * mu_ref[...]
        r = f[:, 0:D_RWKV]
        k = f[:, D_RWKV:2 * D_RWKV]
        v = f[:, 2 * D_RWKV:3 * D_RWKV]
        wd = f[:, 3 * D_RWKV:3 * D_RWKV + R_LOW]
        ad = f[:, 3 * D_RWKV + R_LOW:3 * D_RWKV + 2 * R_LOW]
        opnd_n[_OP_V] = v.astype(BF16)
        yield
        wl = w0_ref[...] + _dot(jnp.tanh(wd).astype(BF16), w2_ref[...])
        nwl = -wl
        softplus = jnp.maximum(nwl, 0.0) + jnp.log(1.0 + jnp.exp(-jnp.abs(nwl)))
        lw = -jnp.exp(-softplus - 0.5)
        yield
        a = 1.0 / (1.0 + jnp.exp(-(a0_ref[...] + _dot(ad.astype(BF16), a2_ref[...]))))
        kmod = k * (1.0 + (a - 1.0) * ka_ref[...])
        yield
        kkr = k * kk_ref[...]
        kk = kkr / jnp.maximum(jnp.sqrt(_seg_sum(kkr * kkr, e2)), 1e-12)
        yield
        aux_n[0] = _seg_sum(r * kmod * rk_ref[...], e2) * v
        if reverse:
            aux_n[1] = _dot((1.0 / (1.0 + jnp.exp(-gd))).astype(BF16), g2_ref[...])
        yield
        ti = lax.broadcasted_iota(jnp.int32, (CHUNK, CHUNK), 0)
        tj = lax.broadcasted_iota(jnp.int32, (CHUNK, CHUNK), 1)
        tri = jnp.where((tj >= ti) if reverse else (tj <= ti), 1.0, 0.0).astype(BF16)
        cum = sum(_dot(tri, part) for part in _split_bf16(lw, 3))
        cum_end = cum[0:1, :] if reverse else cum[CHUNK - 1:CHUNK, :]
        cend_n[0:1, :] = cum_end
        yield
        opnd_n[_OP_R] = (r * jnp.exp(cum)).astype(BF16)
        yield
        e_neg = jnp.exp(-cum)
        bb = kk * a
        opnd_n[_OP_K] = (kmod * e_neg).astype(BF16)
        opnd_n[_OP_B] = (bb * e_neg).astype(BF16)
        yield
        opnd_n[_OP_A] = (-kk * jnp.exp(cum - lw)).astype(BF16)
        yield
        e_end = jnp.exp(cum_end - cum)
        opnd_n[_OP_KH] = (kmod * e_end).astype(BF16)
        opnd_n[_OP_BH] = (bb * e_end).astype(BF16)

    prep_steps = prepare()

    def emit_prep():
        next(prep_steps, None)

    v3 = tiles(opnd[_OP_V])
    a_s = stack_masked(tiles(opnd[_OP_A]))
    r_s = stack_masked(tiles(opnd[_OP_R]))
    bh_s = stack_masked(tiles(opnd[_OP_BH]))
    kh_s = stack_masked(tiles(opnd[_OP_KH]))
    v_s = stack_masked(v3)
    b_d = stack_dup(tiles(opnd[_OP_B]))
    k_d = stack_dup(tiles(opnd[_OP_K]))
    v_d = stack_dup(v3)
    cum_end_prev = cend[0:1, :]
    emit_prep()
    m1 = _bmm_nt(jnp.concatenate([a_s, r_s], axis=1),
                 jnp.concatenate([b_d, k_d], axis=1))
    emit_prep()
    n_ab = jnp.where(strict, m1[:, :LANE, :LANE], 0.0)
    a_ak = jnp.where(strict, m1[:, :LANE, LANE:], 0.0)
    a_rb = jnp.where(incl, m1[:, LANE:, :LANE], 0.0)
    a_rk = jnp.where(incl, m1[:, LANE:, LANE:], 0.0)
    tinv = eye + n_ab
    npow = n_ab
    for _ in range(5):
        npb = npow.astype(BF16)
        npow = _bmm(npb, npb)
        emit_prep()
        tinv = tinv + _bmm(tinv.astype(BF16), npow.astype(BF16))
    gy = _bmm(jnp.concatenate([a_ak, a_rk], axis=1).astype(BF16), v_d)
    emit_prep()
    g_s = jnp.where(same, gy[:, :LANE], 0.0)
    y0_s = jnp.where(same, gy[:, LANE:], 0.0)
    tz = _bmm(tinv.astype(BF16), jnp.concatenate([a_s, g_s.astype(BF16)], axis=2))
    emit_prep()
    w_s = tz[:, :, :LANE]
    u0_s = tz[:, :, LANE:]
    s_add = _bmm_tn(kh_s, v_s)
    s_old = state[...]
    xr = _bmm(jnp.concatenate([w_s.astype(BF16), r_s], axis=1), s_old.astype(BF16))
    emit_prep()
    u_s = xr[:, :LANE] + u0_s
    y_s = xr[:, LANE:] + _bmm(a_rb.astype(BF16), u_s.astype(BF16)) + y0_s
    emit_prep()
    y3 = y_s[:, :CHUNK] + y_s[:, CHUNK:]
    g_end = jnp.exp(jnp.swapaxes(jnp.broadcast_to(tiles(cum_end_prev), (N_PAIR, LANE, LANE)), 1, 2))
    state[...] = g_end * s_old + _bmm_tn(bh_s, u_s.astype(BF16)) + s_add
    y = jnp.concatenate([y3[j] for j in range(N_PAIR)], axis=1)

    if not reverse:
        res = jnp.concatenate([y, aux[0]], axis=1)
    else:
        yb = jnp.where(prev_meta, myb_ref[...], gyb_ref[...])
        yy = y + yb[:, :D_RWKV]
        mean = _seg_sum(yy, e2) * (1.0 / HD_RWKV)
        yc = yy - mean
        var = _seg_sum(yc * yc, e2) * (1.0 / HD_RWKV)
        yn = yc * lax.rsqrt(var + GN_EPS) * lg_ref[...] + lb_ref[...]
        res = ((yn + yb[:, D_RWKV:] + aux[0]) * aux[1]).astype(BF16)

    for _ in prep_steps:
        pass
    opnd[...] = opnd_n[...]
    aux[...] = aux_n[...]
    cend[0:1, :] = cend_n[0:1, :]

    @pl.when(prev_meta)
    def _():
        om_ref[...] = res

    @pl.when(jnp.logical_not(prev_meta))
    def _():
        og_ref[...] = res


def _rwkv_scan(g_prw, m_prw, params, tabs, reverse, prev=None):
    gblk, mblk, flags = tabs
    nsteps = gblk.shape[0]
    gmap = lambda t, gb, mb, fl: (gb[t], 0)
    mmap = lambda t, gb, mb, fl: (mb[t], 0)
    gmap_prev = lambda t, gb, mb, fl: (gb[jnp.maximum(t - 1, 0)], 0)
    mmap_prev = lambda t, gb, mb, fl: (mb[jnp.maximum(t - 1, 0)], 0)
    cmap = lambda t, gb, mb, fl: (0, 0)
    vec = lambda n: pl.BlockSpec((1, n), cmap)
    in_specs = [
        pl.BlockSpec((CHUNK, N_RW), gmap),
        pl.BlockSpec((CHUNK, N_RW), mmap),
        vec(N_SHIFT_P), vec(D_RWKV),
        pl.BlockSpec((R_LOW, D_RWKV), cmap), vec(D_RWKV),
        pl.BlockSpec((R_LOW, D_RWKV), cmap), vec(D_RWKV), vec(D_RWKV), vec(D_RWKV),
    ]
    args = [g_prw, m_prw, params["mu"], params["w0"], params["w2"], params["a0"], params["a2"],
            params["k_k"], params["k_a"], params["r_k"]]
    if reverse:
        g_yb, m_yb = prev
        in_specs += [
            pl.BlockSpec((CHUNK, 2 * D_RWKV), gmap_prev),
            pl.BlockSpec((CHUNK, 2 * D_RWKV), mmap_prev),
            pl.BlockSpec((R_GATE, D_RWKV), cmap), vec(D_RWKV), vec(D_RWKV),
        ]
        args += [g_yb, m_yb, params["g2"], params["lnx_g"], params["lnx_b"]]
        width, dt, n_aux = D_RWKV, BF16, 2
    else:
        width, dt, n_aux = 2 * D_RWKV, F32, 1
    grid_spec = pltpu.PrefetchScalarGridSpec(
        num_scalar_prefetch=3,
        grid=(nsteps,),
        in_specs=in_specs,
        out_specs=[pl.BlockSpec((CHUNK, width), gmap_prev),
                   pl.BlockSpec((CHUNK, width), mmap_prev)],
        scratch_shapes=[
            pltpu.VMEM((N_PAIR, LANE, LANE), F32),
            pltpu.VMEM((8, N_SHIFT_P), F32),
            pltpu.VMEM((N_OPND, CHUNK, D_RWKV), BF16),
            pltpu.VMEM((n_aux, CHUNK, D_RWKV), F32),
            pltpu.VMEM((8, D_RWKV), F32),
            pltpu.VMEM((N_OPND, CHUNK, D_RWKV), BF16),
            pltpu.VMEM((n_aux, CHUNK, D_RWKV), F32),
            pltpu.VMEM((8, D_RWKV), F32),
        ],
    )
    return pl.pallas_call(
        functools.partial(_rwkv_kernel, reverse=reverse),
        grid_spec=grid_spec,
        out_shape=[jax.ShapeDtypeStruct((g_prw.shape[0], width), dt),
                   jax.ShapeDtypeStruct((m_prw.shape[0], width), dt)],
        compiler_params=pltpu.CompilerParams(
            dimension_semantics=("arbitrary",), vmem_limit_bytes=VMEM_LIMIT),
        name="rwkv_bwd" if reverse else "rwkv_fwd",
    )(gblk, mblk, flags, *args)


def _outproj_kernel(*refs, n_x, tiles_first):
    x_refs = refs[:n_x]
    oa_ref, or_ref, eg_ref, eb_ref, wa_ref, wr_ref, g1_ref, b1_ref, h1_ref, h1b_ref = refs[n_x:]
    h0 = _layer_norm(_select_rows(x_refs, tiles_first, slice(None)), eg_ref[...], eb_ref[...])
    mix = _dot(oa_ref[...], wa_ref[...]) + _dot(or_ref[...], wr_ref[...])
    h1 = _layer_norm(DEEPNORM_ALPHA * h0 + mix, g1_ref[...], b1_ref[...])
    h1_ref[...] = h1
    h1b_ref[...] = h1.astype(BF16)


def _outproj(o_attn, o_rwkv, xs, eg, eb, w_out_bf, g1, b1, tm):
    rows = o_attn.shape[0]
    x_specs, tiles_first = _row_specs(xs, tm, D_MODEL)
    cmap = lambda i: (0, 0)
    vec = pl.BlockSpec((1, D_MODEL), cmap)
    return pl.pallas_call(
        functools.partial(_outproj_kernel, n_x=len(xs), tiles_first=tiles_first),
        grid=(rows // tm,),
        in_specs=x_specs + [
            pl.BlockSpec((tm, D_ATTN), lambda i: (i, 0)),
            pl.BlockSpec((tm, D_RWKV), lambda i: (i, 0)),
            vec, vec,
            pl.BlockSpec((D_ATTN, D_MODEL), lambda i: (0, 0), pipeline_mode=pl.Buffered(1)),
            pl.BlockSpec((D_RWKV, D_MODEL), lambda i: (1, 0), pipeline_mode=pl.Buffered(1)),
            vec, vec,
        ],
        out_specs=[pl.BlockSpec((tm, D_MODEL), lambda i: (i, 0)),
                   pl.BlockSpec((tm, D_MODEL), lambda i: (i, 0))],
        out_shape=[jax.ShapeDtypeStruct((rows, D_MODEL), F32),
                   jax.ShapeDtypeStruct((rows, D_MODEL), BF16)],
        compiler_params=pltpu.CompilerParams(
            dimension_semantics=("arbitrary",), vmem_limit_bytes=VMEM_LIMIT),
        name="outproj_ln1",
    )(*xs, o_attn, o_rwkv, eg, eb, w_out_bf, w_out_bf, g1, b1)


def _ffn_in_kernel(pg_s, pm_s, nx_s, fl_s, h_ref, pg_ref, pm_ref, nx_ref, wg_ref, wu_ref,
                   cw_ref, cb_ref, act_ref, lhs, gate_s, up_s):
    del pg_s, pm_s, nx_s
    i = pl.program_id(0)
    c = pl.program_id(1)
    fl = fl_s[i]

    @pl.when(c == 0)
    def _():
        lhs[0:HALO, :] = jnp.where((fl & 1) == 1, pm_ref[...], pg_ref[...])
        lhs[HALO:HALO + TM_FFN, :] = h_ref[...]
        lhs[HALO + TM_FFN:, :] = jnp.where((fl & 2) == 2, jnp.zeros_like(nx_ref), nx_ref[...])

    units = [(s, k) for s in range(TM_FFN // TM_SLAB) for k in range(FC // FC_SUB)]

    def matmuls(u):
        s, k = units[u]
        cols = slice(k * FC_SUB, (k + 1) * FC_SUB)
        r0 = s * TM_SLAB
        gate_s[u] = _dot(lhs[r0:r0 + TM_SLAB + 2 * HALO, :], wg_ref[:, cols])
        up_s[u] = _dot(lhs[r0 + HALO:r0 + HALO + TM_SLAB, :], wu_ref[:, cols])

    def activation(u):
        s, k = units[u]
        cols = slice(k * FC_SUB, (k + 1) * FC_SUB)
        gate = (gate_s[u, HALO - 1:HALO - 1 + TM_SLAB, :] * cw_ref[0:1, cols]
                + gate_s[u, HALO:HALO + TM_SLAB, :] * cw_ref[1:2, cols]
                + gate_s[u, HALO + 1:HALO + 1 + TM_SLAB, :] * cw_ref[2:3, cols] + cb_ref[:, cols])
        act = 0.5 * gate * (1.0 + lax.erf(gate * 0.7071067811865476)) * up_s[u]
        act_ref[s * TM_SLAB:(s + 1) * TM_SLAB, cols] = act.astype(BF16)

    matmuls(0)
    for u in range(len(units)):
        if u + 1 < len(units):
            matmuls(u + 1)
        activation(u)


def _ffn_in(g_h1b, m_h1b, w_in_bf, conv_w, conv_b, tabs):
    pg, pm, nx, fl = tabs
    rows = g_h1b.shape[0]
    nfc = D_FF // FC
    n_units = (TM_FFN // TM_SLAB) * (FC // FC_SUB)
    grid_spec = pltpu.PrefetchScalarGridSpec(
        num_scalar_prefetch=4,
        grid=(rows // TM_FFN, nfc),
        in_specs=[
            pl.BlockSpec((TM_FFN, D_MODEL), lambda i, c, *_: (i, 0)),
            pl.BlockSpec((HALO, D_MODEL), lambda i, c, pg, pm, nx, fl: (pg[i], 0)),
            pl.BlockSpec((HALO, D_MODEL), lambda i, c, pg, pm, nx, fl: (pm[i], 0)),
            pl.BlockSpec((HALO, D_MODEL), lambda i, c, pg, pm, nx, fl: (nx[i], 0)),
            pl.BlockSpec((D_MODEL, FC), lambda i, c, *_: (0, c)),
            pl.BlockSpec((D_MODEL, FC), lambda i, c, *_: (0, nfc + c)),
            pl.BlockSpec((3, FC), lambda i, c, *_: (0, c)),
            pl.BlockSpec((1, FC), lambda i, c, *_: (0, c)),
        ],
        out_specs=pl.BlockSpec((TM_FFN, FC), lambda i, c, *_: (i, c)),
        scratch_shapes=[
            pltpu.VMEM((TM_FFN + 2 * HALO, D_MODEL), BF16),
            pltpu.VMEM((n_units, TM_SLAB + 2 * HALO, FC_SUB), F32),
            pltpu.VMEM((n_units, TM_SLAB, FC_SUB), F32),
        ],
    )
    return pl.pallas_call(
        _ffn_in_kernel,
        grid_spec=grid_spec,
        out_shape=jax.ShapeDtypeStruct((rows, D_FF), BF16),
        compiler_params=pltpu.CompilerParams(
            dimension_semantics=("arbitrary", "arbitrary"), vmem_limit_bytes=VMEM_LIMIT),
        name="ffn_in",
    )(pg, pm, nx, fl, g_h1b, g_h1b, m_h1b, g_h1b, w_in_bf, w_in_bf, conv_w, conv_b)


def _ffn_out_kernel(act_ref, h_ref, wo_ref, g2_ref, b2_ref, y_ref):
    n_slab = TM_OUT // SLAB_OUT
    rows = lambda s: slice(s * SLAB_OUT, (s + 1) * SLAB_OUT)
    ffn = {}

    def project(s):
        ffn[s] = _dot(act_ref[rows(s), :], wo_ref[...])

    def norm(s):
        y_ref[rows(s), :] = _layer_norm(DEEPNORM_ALPHA * h_ref[rows(s), :] + ffn.pop(s),
                                        g2_ref[...], b2_ref[...])

    _emit_skewed(n_slab, [(0, project), (1, norm)])


def _ffn_out(act, g_h1, w_out_bf, g2, b2, row0, n_rows):
    tm = TM_OUT
    tile0 = row0 // tm
    cmap = lambda i: (0, 0)
    vec = pl.BlockSpec((1, D_MODEL), cmap)
    return pl.pallas_call(
        _ffn_out_kernel,
        grid=(n_rows // tm,),
        in_specs=[
            pl.BlockSpec((tm, D_FF), lambda i: (tile0 + i, 0)),
            pl.BlockSpec((tm, D_MODEL), lambda i: (tile0 + i, 0)),
            pl.BlockSpec((D_FF, D_MODEL), cmap, pipeline_mode=pl.Buffered(1)),
            vec, vec,
        ],
        out_specs=pl.BlockSpec((tm, D_MODEL), lambda i: (i, 0)),
        out_shape=jax.ShapeDtypeStruct((n_rows, D_MODEL), F32),
        compiler_params=pltpu.CompilerParams(
            dimension_semantics=("arbitrary",), vmem_limit_bytes=VMEM_LIMIT),
        name="ffn_out_ln2",
    )(act, g_h1, w_out_bf, g2, b2)


def _attn_tables(seq_lens):
    rec = []
    base = 0
    for s, t_len in enumerate(seq_lens):
        rows = t_len // GRID_W
        for qb in range(rows // QB_ROWS):
            w0 = min(max(qb * QB_ROWS - WIN_H // 2, 0), rows - 2 * QB_ROWS)
            variant = 0 if qb == 0 else (2 if qb == rows // QB_ROWS - 1 else 1)
            rec.append((variant, (base + qb * QB_TOK) // QB_TOK, (base + w0 * GRID_W) // KV_BLK, s))
        base += t_len
    rec.sort(key=lambda x: x[0])
    arr = np.asarray(rec, np.int32)
    return tuple(jnp.asarray(arr[:, c]) for c in (1, 2, 0, 3))


def _rwkv_tables(seq_lens, reverse):
    gblk, mblk, flags = [], [], []
    base = 0
    for s, t_len in enumerate(seq_lens):
        n = t_len // CHUNK
        first_blk = base // CHUNK
        steps = [(first_blk, 2)] + [(first_blk + c, 0) for c in range(n)]
        if reverse:
            steps = steps[::-1]
        for idx, (blk, fl) in enumerate(steps):
            gblk.append(blk)
            mblk.append(s)
            flags.append(fl | (1 if idx == 0 else 0))
        base += t_len
    gblk.append(gblk[-1])
    mblk.append(mblk[-1])
    flags.append(0)
    return tuple(jnp.asarray(np.asarray(a, np.int32)) for a in (gblk, mblk, flags))


def _ffn_tables(seq_lens):
    pg, pm, nx, fl = [], [], [], []
    base = 0
    total = sum(seq_lens)
    for s, t_len in enumerate(seq_lens):
        for t in range(t_len // TM_FFN):
            r0 = base + t * TM_FFN
            first = t == 0
            last = t == t_len // TM_FFN - 1
            pg.append(max(r0 // HALO - 1, 0))
            pm.append((s * CHUNK + CHUNK - HALO) // HALO)
            nx.append(min((r0 + TM_FFN) // HALO, total // HALO - 1))
            fl.append((1 if first else 0) | (2 if last else 0))
        base += t_len
    return tuple(jnp.asarray(np.asarray(a, np.int32)) for a in (pg, pm, nx, fl))


def _pad_cols(w, n):
    return jnp.pad(w, ((0, 0), (0, n - w.shape[1])))


def _encode_all(xs, meta_tokens, emb_ln_g, emb_ln_b, w_in, attn_rpb, rwkv_mu, rwkv_w0, rwkv_w2,
                rwkv_a0, rwkv_a2, rwkv_g2, rwkv_k_k, rwkv_k_a, rwkv_r_k, rwkv_lnx_g, rwkv_lnx_b,
                w_out, ln1_g, ln1_b, ffn_w_in, ffn_conv_w, ffn_conv_b, ffn_w_out, ln2_g, ln2_b):
    seq_lens = []
    for x in xs:
        seq_lens += [x.shape[1]] * x.shape[0]
    nseq = len(seq_lens)
    x2d = [x.reshape(-1, D_MODEL) for x in xs]
    meta_blk = jnp.concatenate([jnp.zeros((PAD_ROWS, D_MODEL), F32), meta_tokens.astype(F32)], axis=0)
    x_m = jnp.tile(meta_blk, (nseq, 1))

    o1 = N_QKV
    o2 = o1 + 3 * D_RWKV
    o3 = o2 + R_DECAY
    o4 = o3 + R_ICLR
    wl = w_in[0]
    w_qkv = wl[:, :o1].astype(BF16)
    w_rw = jnp.concatenate(
        [wl[:, o1:o2], _pad_cols(wl[:, o2:o3], R_LOW), _pad_cols(wl[:, o3:o4], R_LOW), wl[:, o4:]],
        axis=1).astype(BF16)
    eg = emb_ln_g.reshape(1, D_MODEL)
    eb = emb_ln_b.reshape(1, D_MODEL)
    mu = rwkv_mu[0]
    o5 = 3 * D_RWKV
    mu_p = jnp.concatenate(
        [mu[:, :o5], _pad_cols(mu[:, o5:o5 + R_DECAY], R_LOW), _pad_cols(mu[:, o5 + R_DECAY:], R_LOW)],
        axis=1)
    pad_rows = lambda w: jnp.pad(w, ((0, R_LOW - w.shape[0]), (0, 0))).astype(BF16)
    row = lambda v: v.reshape(1, -1)

    def dir_params(d):
        return dict(mu=mu_p[d:d + 1], w0=row(rwkv_w0[0, d]), w2=pad_rows(rwkv_w2[0, d]),
                    a0=row(rwkv_a0[0, d]), a2=pad_rows(rwkv_a2[0, d]),
                    k_k=row(rwkv_k_k[0]), k_a=row(rwkv_k_a[0]), r_k=row(rwkv_r_k[0]),
                    g2=rwkv_g2[0].astype(BF16), lnx_g=row(rwkv_lnx_g[0]), lnx_b=row(rwkv_lnx_b[0]))

    w_out_bf = w_out[0].astype(BF16)
    ffn_w_in_bf = ffn_w_in[0].astype(BF16)
    ffn_w_out_bf = ffn_w_out[0].astype(BF16)
    bias = _attn_bias_tiles(attn_rpb[0])

    g_qkv = _inproj(x2d, eg, eb, w_qkv, BF16, True, TM)
    g_prw = _inproj(x2d, eg, eb, w_rw, F32, False, TM)
    m_qkv = _inproj([x_m], eg, eb, w_qkv, BF16, True, x_m.shape[0])
    m_prw = _inproj([x_m], eg, eb, w_rw, F32, False, x_m.shape[0])

    g_oattn = _attention(g_qkv, m_qkv, bias, _attn_tables(seq_lens))
    m_oattn = _meta_attention(m_qkv)
    g_yb, m_yb = _rwkv_scan(g_prw, m_prw, dir_params(0), _rwkv_tables(seq_lens, False), False)
    g_orwkv, m_orwkv = _rwkv_scan(g_prw, m_prw, dir_params(1), _rwkv_tables(seq_lens, True), True,
                                  prev=(g_yb, m_yb))

    g1 = ln1_g.reshape(1, D_MODEL)
    b1 = ln1_b.reshape(1, D_MODEL)
    g_h1, g_h1b = _outproj(g_oattn, g_orwkv, x2d, eg, eb, w_out_bf, g1, b1, TM)
    _, m_h1b = _outproj(m_oattn, m_orwkv, [x_m], eg, eb, w_out_bf, g1, b1, x_m.shape[0])

    out_rows = [x.shape[0] * x.shape[1] for x in xs]
    act = _ffn_in(g_h1b, m_h1b, ffn_w_in_bf, ffn_conv_w[0], ffn_conv_b.reshape(1, D_FF),
                  _ffn_tables(seq_lens))
    ys, row0 = [], 0
    for n_rows in out_rows:
        ys.append(_ffn_out(act, g_h1, ffn_w_out_bf, ln2_g.reshape(1, D_MODEL),
                           ln2_b.reshape(1, D_MODEL), row0, n_rows))
        row0 += n_rows
    return [y.reshape(x.shape) for y, x in zip(ys, xs)]


def kernel(x_prompt, x_sample, meta_tokens, emb_ln_g, emb_ln_b, w_in, attn_rpb, rwkv_mu, rwkv_w0, rwkv_w2, rwkv_a0, rwkv_a2, rwkv_g2, rwkv_k_k, rwkv_k_a, rwkv_r_k, rwkv_lnx_g, rwkv_lnx_b, w_out, ln1_g, ln1_b, ffn_w_in, ffn_conv_w, ffn_conv_b, ffn_w_out, ln2_g, ln2_b):
    y_prompt, y_sample = _encode_all(
        [x_prompt, x_sample], meta_tokens, emb_ln_g, emb_ln_b, w_in, attn_rpb, rwkv_mu, rwkv_w0,
        rwkv_w2, rwkv_a0, rwkv_a2, rwkv_g2, rwkv_k_k, rwkv_k_a, rwkv_r_k, rwkv_lnx_g, rwkv_lnx_b,
        w_out, ln1_g, ln1_b, ffn_w_in, ffn_conv_w, ffn_conv_b, ffn_w_out, ln2_g, ln2_b)
    return (y_prompt, y_sample)
```

```python
import functools

import numpy as np
import jax
import jax.numpy as jnp
from jax import lax
from jax.experimental import pallas as pl
from jax.experimental.pallas import tpu as pltpu

D_MODEL = 2048
N_META = 16
GRID_W = 64
WIN_H = 8
WIN_W = 16
D_ATTN = 1024
HD_ATTN = 128
H_ATTN = D_ATTN // HD_ATTN
D_RWKV = 1024
HD_RWKV = 64
R_DECAY = 96
R_ICLR = 96
R_GATE = 256
D_FF = 5632
DEEPNORM_ALPHA = 2.0 ** 0.25
LN_EPS = 1e-5
GN_EPS = 64e-5
NEG_INF = -1e30

LANE = 128
CHUNK = 64
PAD_ROWS = CHUNK - N_META
R_LOW = 128
N_SHIFT_P = 3 * D_RWKV + 2 * R_LOW
N_RW = N_SHIFT_P + R_GATE
N_QKV = 3 * D_ATTN
N_PAIR = D_RWKV // LANE
QB_ROWS = 8
QB_TOK = QB_ROWS * GRID_W
KV_BLK = 256
WIN_TOK = 4 * KV_BLK
TM = 512
TM_FFN = 1024
TM_SLAB = 512
TM_OUT = 256
SLAB_OUT = 128
N_SLAB = 4
FC = 512
FC_SUB = 256
HALO = 16
VMEM_LIMIT = 60 * 1024 * 1024

F32 = jnp.float32
BF16 = jnp.bfloat16


def _layer_norm(x, g, b):
    mu = jnp.mean(x, axis=-1, keepdims=True)
    xc = x - mu
    var = jnp.mean(xc * xc, axis=-1, keepdims=True)
    return xc * lax.rsqrt(var + LN_EPS) * g + b


def _dot(a, b):
    return jnp.dot(a, b, preferred_element_type=F32)


def _dot_nt(a, b):
    return lax.dot_general(a, b, (((1,), (1,)), ((), ())), preferred_element_type=F32)


def _bmm(a, b):
    return jnp.einsum("bik,bkj->bij", a, b, preferred_element_type=F32)


def _bmm_nt(a, b):
    return jnp.einsum("bik,bjk->bij", a, b, preferred_element_type=F32)


def _bmm_tn(a, b):
    return jnp.einsum("bki,bkj->bij", a, b, preferred_element_type=F32)


def _select_rows(x_refs, tiles_first, rows):
    if len(x_refs) == 1:
        return x_refs[0][rows, :]
    return jnp.where(pl.program_id(0) < tiles_first, x_refs[0][rows, :], x_refs[1][rows, :])


def _emit_skewed(n_slabs, stages):
    depth = max(lag for lag, _ in stages)
    for k in range(n_slabs + depth):
        for lag, fn in stages:
            if 0 <= k - lag < n_slabs:
                fn(k - lag)


def _row_specs(xs, tm, width):
    if len(xs) == 1:
        return [pl.BlockSpec((tm, width), lambda i: (i, 0))], 0
    tiles_first = xs[0].shape[0] // tm
    return [pl.BlockSpec((tm, width), lambda i: (jnp.minimum(i, tiles_first - 1), 0)),
            pl.BlockSpec((tm, width), lambda i: (jnp.maximum(i - tiles_first, 0), 0))], tiles_first


def _inproj_kernel(*refs, n_x, tiles_first, q_scale):
    x_refs = refs[:n_x]
    g_ref, b_ref, w_ref, o_ref = refs[n_x:]
    slab = o_ref.shape[0] // N_SLAB
    rows = lambda s: slice(s * slab, (s + 1) * slab)
    h = {}

    def norm(s):
        h[s] = _layer_norm(_select_rows(x_refs, tiles_first, rows(s)), g_ref[...], b_ref[...]).astype(BF16)

    def project(s):
        acc = _dot(h.pop(s), w_ref[...])
        if q_scale:
            o_ref[rows(s), :D_ATTN] = (acc[:, :D_ATTN] * (HD_ATTN ** -0.5)).astype(o_ref.dtype)
            o_ref[rows(s), D_ATTN:] = acc[:, D_ATTN:].astype(o_ref.dtype)
        else:
            o_ref[rows(s), :] = acc.astype(o_ref.dtype)

    _emit_skewed(N_SLAB, [(1, project), (0, norm)])


def _inproj(xs, ln_g, ln_b, w, out_dtype, q_scale, tm):
    rows = sum(x.shape[0] for x in xs)
    n_out = w.shape[1]
    x_specs, tiles_first = _row_specs(xs, tm, D_MODEL)
    cmap = lambda i: (0, 0)
    return pl.pallas_call(
        functools.partial(_inproj_kernel, n_x=len(xs), tiles_first=tiles_first, q_scale=q_scale),
        grid=(rows // tm,),
        in_specs=x_specs + [
            pl.BlockSpec((1, D_MODEL), cmap),
            pl.BlockSpec((1, D_MODEL), cmap),
            pl.BlockSpec((D_MODEL, n_out), cmap, pipeline_mode=pl.Buffered(1)),
        ],
        out_specs=pl.BlockSpec((tm, n_out), lambda i: (i, 0)),
        out_shape=jax.ShapeDtypeStruct((rows, n_out), out_dtype),
        compiler_params=pltpu.CompilerParams(
            dimension_semantics=("arbitrary",), vmem_limit_bytes=VMEM_LIMIT),
        name="inproj_qkv" if q_scale else "inproj_rwkv",
    )(*xs, ln_g, ln_b, w)


def _window_rows(variant, i):
    half = WIN_H // 2
    if variant == 0:
        off, r0 = 0, max(i - half, 0)
    elif variant == 1:
        off, r0 = -half, i - half
    else:
        off, r0 = -WIN_H, min(i - half, 0)
    return [j for j in range(2 * QB_ROWS) if r0 <= j + off < r0 + WIN_H], off


def _attn_tile_plan(variant):
    plan = []
    for i in range(QB_ROWS):
        rows, off = _window_rows(variant, i)
        tiles = []
        for jt in range(QB_ROWS):
            first, second = 2 * jt in rows, 2 * jt + 1 in rows
            if not (first or second):
                continue
            kind = 0 if (first and second) else (1 if second else 2)
            e = 2 * jt + off - i + WIN_H
            assert (not first or 0 <= e - 1 <= 2 * WIN_H - 2) and (not second or 0 <= e <= 2 * WIN_H - 2)
            tiles.append((jt, kind, e))
        plan.append(tiles)
    return plan


def _attn_kernel(qblk_s, kvblk_s, var_s, mseq_s, q_ref, k_ref, v_ref,
                 mk_ref, mv_ref, bt_ref, o_ref, p_scr):
    del qblk_s, kvblk_s, mseq_s
    variant = var_s[pl.program_id(1)]
    q = q_ref[...]
    s_win = _dot_nt(q, k_ref[...])
    km = mk_ref[PAD_ROWS:, :]
    vm = mv_ref[PAD_ROWS:, :]
    sm = _dot_nt(q, km)

    def softmax_pv(plan):
        def score_tile(i, jt, kind, e):
            rows = slice(i * GRID_W, (i + 1) * GRID_W)
            return s_win[rows, jt * LANE:(jt + 1) * LANE] + bt_ref[0, kind, e]

        tile_max = []
        for i, tiles in enumerate(plan):
            t = None
            for jt, kind, e in tiles:
                x = score_tile(i, jt, kind, e)
                t = x if t is None else jnp.maximum(t, x)
            tile_max.append(t)
        m = jnp.maximum(jnp.max(jnp.concatenate(tile_max, axis=0), axis=-1, keepdims=True),
                        jnp.max(sm, axis=-1, keepdims=True))
        pm = jnp.exp(sm - m)
        p_scr[...] = jnp.zeros_like(p_scr)
        tile_sum = []
        for i, tiles in enumerate(plan):
            rows = slice(i * GRID_W, (i + 1) * GRID_W)
            mi = m[rows]
            t = None
            for jt, kind, e in tiles:
                p = jnp.exp(score_tile(i, jt, kind, e) - mi)
                p_scr[rows, jt * LANE:(jt + 1) * LANE] = p.astype(BF16)
                t = p if t is None else t + p
            tile_sum.append(t)
        l = (jnp.sum(jnp.concatenate(tile_sum, axis=0), axis=-1, keepdims=True)
             + jnp.sum(pm, axis=-1, keepdims=True))
        o = _dot(pm.astype(BF16), vm)
        o = o + _dot(p_scr[...], v_ref[...])
        o_ref[...] = (o / l).astype(BF16)

    for v in range(3):
        pl.when(variant == v)(functools.partial(softmax_pv, _attn_tile_plan(v)))


def _attention(g_qkv, m_qkv, bias, tabs):
    qblk, kvblk, var, mseq = tabs
    nsteps = qblk.shape[0]
    rows = g_qkv.shape[0]

    def window_spec(col0):
        return pl.BlockSpec((pl.Element(WIN_TOK), pl.Element(HD_ATTN)),
                            lambda h, t, qb, kb, vr, ms: (kb[t] * KV_BLK, (col0 + h) * HD_ATTN))

    in_specs = [pl.BlockSpec((QB_TOK, HD_ATTN), lambda h, t, qb, kb, vr, ms: (qb[t], h)),
                window_spec(H_ATTN), window_spec(2 * H_ATTN)]
    in_specs += [
        pl.BlockSpec((CHUNK, HD_ATTN), lambda h, t, qb, kb, vr, ms: (ms[t], H_ATTN + h)),
        pl.BlockSpec((CHUNK, HD_ATTN), lambda h, t, qb, kb, vr, ms: (ms[t], 2 * H_ATTN + h)),
        pl.BlockSpec((1, 3, 2 * WIN_H, GRID_W, LANE), lambda h, t, qb, kb, vr, ms: (h, 0, 0, 0, 0)),
    ]
    grid_spec = pltpu.PrefetchScalarGridSpec(
        num_scalar_prefetch=4,
        grid=(H_ATTN, nsteps),
        in_specs=in_specs,
        out_specs=pl.BlockSpec((QB_TOK, HD_ATTN), lambda h, t, qb, kb, vr, ms: (qb[t], h)),
        scratch_shapes=[pltpu.VMEM((QB_TOK, WIN_TOK), BF16)],
    )
    return pl.pallas_call(
        _attn_kernel,
        grid_spec=grid_spec,
        out_shape=jax.ShapeDtypeStruct((rows, D_ATTN), BF16),
        compiler_params=pltpu.CompilerParams(
            dimension_semantics=("arbitrary", "arbitrary"), vmem_limit_bytes=VMEM_LIMIT),
        name="nbr_attention",
    )(qblk, kvblk, var, mseq, g_qkv, g_qkv, g_qkv, m_qkv, m_qkv, bias)


def _meta_attn_kernel(q_ref, k_ref, v_ref, o_ref):
    q = q_ref[...]
    km = k_ref[PAD_ROWS:, :]
    vm = v_ref[PAD_ROWS:, :]
    s = _dot_nt(q, km)
    m = jnp.max(s, axis=-1, keepdims=True)
    p = jnp.exp(s - m)
    l = jnp.sum(p, axis=-1, keepdims=True)
    o_ref[...] = (_dot(p.astype(BF16), vm) / l).astype(BF16)


def _meta_attention(m_qkv):
    nseq = m_qkv.shape[0] // CHUNK
    return pl.pallas_call(
        _meta_attn_kernel,
        grid=(nseq, H_ATTN),
        in_specs=[
            pl.BlockSpec((CHUNK, HD_ATTN), lambda s, h: (s, h)),
            pl.BlockSpec((CHUNK, HD_ATTN), lambda s, h: (s, H_ATTN + h)),
            pl.BlockSpec((CHUNK, HD_ATTN), lambda s, h: (s, 2 * H_ATTN + h)),
        ],
        out_specs=pl.BlockSpec((CHUNK, HD_ATTN), lambda s, h: (s, h)),
        out_shape=jax.ShapeDtypeStruct((m_qkv.shape[0], D_ATTN), BF16),
        compiler_params=pltpu.CompilerParams(dimension_semantics=("arbitrary", "arbitrary")),
        name="meta_attention",
    )(m_qkv, m_qkv, m_qkv)


def _attn_bias_tiles(rpb):
    qc = np.arange(GRID_W)
    kc = np.arange(GRID_W)
    c0 = np.clip(qc - WIN_W // 2, 0, GRID_W - WIN_W)
    colmask = (kc[None, :] >= c0[:, None]) & (kc[None, :] < c0[:, None] + WIN_W)
    dc = np.clip(kc[None, :] - qc[:, None], -(WIN_W - 1), WIN_W - 1) + (WIN_W - 1)
    n_dr = 2 * WIN_H - 1
    n_dc = 2 * WIN_W - 1
    dc_onehot = (dc[None] == np.arange(n_dc)[:, None, None]).astype(np.float32)
    toeplitz = jnp.einsum("hrd,dqk->hrqk", rpb, jnp.asarray(dc_onehot),
                          precision=lax.Precision.HIGHEST)
    toeplitz = jnp.where(jnp.asarray(colmask)[None, None], toeplitz, NEG_INF)
    masked = jnp.full((H_ATTN, 1, GRID_W, GRID_W), NEG_INF, F32)
    padded = jnp.concatenate([masked, toeplitz, masked], axis=1)
    first = padded[:, 0:n_dr + 1]
    second = padded[:, 1:n_dr + 2]
    off = jnp.full_like(first, NEG_INF)
    kinds = [jnp.concatenate([a, b], axis=-1) for a, b in ((first, second), (off, second), (first, off))]
    return jnp.stack(kinds, axis=1)


def _split_bf16(x, n):
    parts = []
    rem = x
    for _ in range(n):
        p = rem.astype(BF16)
        parts.append(p)
        rem = rem - p.astype(F32)
    return parts


def _seg_sum(x, e2):
    stacked = jnp.concatenate([x[:, j * LANE:(j + 1) * LANE] for j in range(N_PAIR)], axis=0)
    hi, lo = _split_bf16(stacked, 2)
    res = _dot(hi, e2) + _dot(lo, e2)
    return jnp.concatenate([res[j * CHUNK:(j + 1) * CHUNK] for j in range(N_PAIR)], axis=1)


_OP_A, _OP_R, _OP_BH, _OP_KH, _OP_V, _OP_B, _OP_K = range(7)
N_OPND = 7


def _rwkv_kernel(gblk_s, mblk_s, flag_s, *refs, reverse):
    del gblk_s, mblk_s
    if reverse:
        (gp_ref, mp_ref, mu_ref, w0_ref, w2_ref, a0_ref, a2_ref, kk_ref, ka_ref, rk_ref,
         gyb_ref, myb_ref, g2_ref, lg_ref, lb_ref, og_ref, om_ref,
         state, carry, opnd, aux, cend, opnd_n, aux_n, cend_n) = refs
    else:
        (gp_ref, mp_ref, mu_ref, w0_ref, w2_ref, a0_ref, a2_ref, kk_ref, ka_ref, rk_ref,
         og_ref, om_ref, state, carry, opnd, aux, cend, opnd_n, aux_n, cend_n) = refs
    step = pl.program_id(0)
    fl = flag_s[step]
    fl_prev = flag_s[jnp.maximum(step - 1, 0)]
    is_first = (fl & 1) == 1
    is_meta = (fl & 2) == 2
    prev_first = (fl_prev & 1) == 1
    prev_meta = (fl_prev & 2) == 2

    @pl.when(step == 0)
    def _():
        opnd[...] = jnp.zeros_like(opnd)
        aux[...] = jnp.zeros_like(aux)
        cend[...] = jnp.zeros_like(cend)

    @pl.when(is_first)
    def _():
        carry[...] = jnp.zeros_like(carry)

    @pl.when(prev_first)
    def _():
        state[...] = jnp.zeros_like(state)

    ii = lax.broadcasted_iota(jnp.int32, (LANE, LANE), 0)
    jj = lax.broadcasted_iota(jnp.int32, (LANE, LANE), 1)
    same = (ii // CHUNK) == (jj // CHUNK)
    e2 = jnp.where(same, 1.0, 0.0).astype(BF16)

    ti2 = ii % CHUNK
    tj2 = jj % CHUNK
    if reverse:
        strict = jnp.logical_and(same, tj2 > ti2)
        incl = jnp.logical_and(same, tj2 >= ti2)
    else:
        strict = jnp.logical_and(same, tj2 < ti2)
        incl = jnp.logical_and(same, tj2 <= ti2)
    eye = jnp.where(ii == jj, 1.0, 0.0).astype(F32)
    lane64 = lax.broadcasted_iota(jnp.int32, (1, CHUNK, LANE), 2) < HD_RWKV

    def tiles(x):
        return jnp.stack([x[:, j * LANE:(j + 1) * LANE] for j in range(N_PAIR)], axis=0)

    def stack_masked(x):
        zero = jnp.zeros_like(x)
        return jnp.concatenate([jnp.where(lane64, x, zero), jnp.where(lane64, zero, x)], axis=1)

    def stack_dup(x):
        return jnp.concatenate([x, x], axis=1)

    def prepare():
        praw = jnp.where(is_meta, mp_ref[...], gp_ref[...])
        row = lax.broadcasted_iota(jnp.int32, (CHUNK, 1), 0)
        pad = jnp.logical_and(is_meta, row < PAD_ROWS)
        p_sh = jnp.where(pad, 0.0, praw[:, :N_SHIFT_P])
        gd = praw[:, N_SHIFT_P:]
        if reverse:
            rolled = pltpu.roll(p_sh, CHUNK - 1, axis=0)
            nb = jnp.where(row == CHUNK - 1, carry[0:1, :], rolled)
            carry[...] = p_sh[0:8, :]
        else:
            rolled = pltpu.roll(p_sh, 1, axis=0)
            nb = jnp.where(row == 0, carry[7:8, :], rolled)
            carry[...] = p_sh[CHUNK - 8:CHUNK, :]
        f = p_sh + (nb - p_sh) * mu_ref[...]
        r = f[:, 0:D_RWKV]
        k = f[:, D_RWKV:2 * D_RWKV]
        v = f[:, 2 * D_RWKV:3 * D_RWKV]
        wd = f[:, 3 * D_RWKV:3 * D_RWKV + R_LOW]
        ad = f[:, 3 * D_RWKV + R_LOW:3 * D_RWKV + 2 * R_LOW]
        opnd_n[_OP_V] = v.astype(BF16)
        yield
        wl = w0_ref[...] + _dot(jnp.tanh(wd).astype(BF16), w2_ref[...])
        nwl = -wl
        softplus = jnp.maximum(nwl, 0.0) + jnp.log(1.0 + jnp.exp(-jnp.abs(nwl)))
        lw = -jnp.exp(-softplus - 0.5)
        yield
        a = 1.0 / (1.0 + jnp.exp(-(a0_ref[...] + _dot(ad.astype(BF16), a2_ref[...]))))
        kmod = k * (1.0 + (a - 1.0) * ka_ref[...])
        yield
        kkr = k * kk_ref[...]
        kk = kkr / jnp.maximum(jnp.sqrt(_seg_sum(kkr * kkr, e2)), 1e-12)
        yield
        aux_n[0] = _seg_sum(r * kmod * rk_ref[...], e2) * v
        if reverse:
            aux_n[1] = _dot((1.0 / (1.0 + jnp.exp(-gd))).astype(BF16), g2_ref[...])
        yield
        ti = lax.broadcasted_iota(jnp.int32, (CHUNK, CHUNK), 0)
        tj = lax.broadcasted_iota(jnp.int32, (CHUNK, CHUNK), 1)
        tri = jnp.where((tj >= ti) if reverse else (tj <= ti), 1.0, 0.0).astype(BF16)
        cum = sum(_dot(tri, part) for part in _split_bf16(lw, 3))
        cum_end = cum[0:1, :] if reverse else cum[CHUNK - 1:CHUNK, :]
        cend_n[0:1, :] = cum_end
        yield
        opnd_n[_OP_R] = (r * jnp.exp(cum)).astype(BF16)
        yield
        e_neg = jnp.exp(-cum)
        bb = kk * a
        opnd_n[_OP_K] = (kmod * e_neg).astype(BF16)
        opnd_n[_OP_B] = (bb * e_neg).astype(BF16)
        yield
        opnd_n[_OP_A] = (-kk * jnp.exp(cum - lw)).astype(BF16)
        yield
        e_end = jnp.exp(cum_end - cum)
        opnd_n[_OP_KH] = (kmod * e_end).astype(BF16)
        opnd_n[_OP_BH] = (bb * e_end).astype(BF16)

    prep_steps = prepare()

    def emit_prep():
        next(prep_steps, None)

    v3 = tiles(opnd[_OP_V])
    a_s = stack_masked(tiles(opnd[_OP_A]))
    r_s = stack_masked(tiles(opnd[_OP_R]))
    bh_s = stack_masked(tiles(opnd[_OP_BH]))
    kh_s = stack_masked(tiles(opnd[_OP_KH]))
    v_s = stack_masked(v3)
    b_d = stack_dup(tiles(opnd[_OP_B]))
    k_d = stack_dup(tiles(opnd[_OP_K]))
    v_d = stack_dup(v3)
    cum_end_prev = cend[0:1, :]
    emit_prep()
    m1 = _bmm_nt(jnp.concatenate([a_s, r_s], axis=1),
                 jnp.concatenate([b_d, k_d], axis=1))
    emit_prep()
    n_ab = jnp.where(strict, m1[:, :LANE, :LANE], 0.0)
    a_ak = jnp.where(strict, m1[:, :LANE, LANE:], 0.0)
    a_rb = jnp.where(incl, m1[:, LANE:, :LANE], 0.0)
    a_rk = jnp.where(incl, m1[:, LANE:, LANE:], 0.0)
    eye_b = eye.astype(BF16)
    tinv = (eye + n_ab).astype(BF16)
    npow = n_ab.astype(BF16)
    for _ in range(5):
        npow = _bmm(npow, npow).astype(BF16)
        emit_prep()
        tinv = _bmm(tinv, eye_b + npow).astype(BF16)
    gy = _bmm(jnp.concatenate([a_ak, a_rk], axis=1).astype(BF16), v_d)
    emit_prep()
    g_s = jnp.where(same, gy[:, :LANE], 0.0)
    y0_s = jnp.where(same, gy[:, LANE:], 0.0)
    tz = _bmm(tinv, jnp.concatenate([a_s, g_s.astype(BF16)], axis=2))
    emit_prep()
    w_s = tz[:, :, :LANE]
    u0_s = tz[:, :, LANE:]
    s_add = _bmm_tn(kh_s, v_s)
    s_old = state[...]
    xr = _bmm(jnp.concatenate([w_s.astype(BF16), r_s], axis=1), s_old.astype(BF16))
    emit_prep()
    u_s = xr[:, :LANE] + u0_s
    y_s = xr[:, LANE:] + _bmm(a_rb.astype(BF16), u_s.astype(BF16)) + y0_s
    emit_prep()
    y3 = y_s[:, :CHUNK] + y_s[:, CHUNK:]
    g_end = jnp.exp(jnp.swapaxes(jnp.broadcast_to(tiles(cum_end_prev), (N_PAIR, LANE, LANE)), 1, 2))
    state[...] = g_end * s_old + _bmm_tn(bh_s, u_s.astype(BF16)) + s_add
    y = jnp.concatenate([y3[j] for j in range(N_PAIR)], axis=1)

    if not reverse:
        res = jnp.concatenate([y, aux[0]], axis=1)
    else:
        yb = jnp.where(prev_meta, myb_ref[...], gyb_ref[...])
        yy = y + yb[:, :D_RWKV]
        mean = _seg_sum(yy, e2) * (1.0 / HD_RWKV)
        yc = yy - mean
        var = _seg_sum(yc * yc, e2) * (1.0 / HD_RWKV)
        yn = yc * lax.rsqrt(var + GN_EPS) * lg_ref[...] + lb_ref[...]
        res = ((yn + yb[:, D_RWKV:] + aux[0]) * aux[1]).astype(BF16)

    for _ in prep_steps:
        pass
    opnd[...] = opnd_n[...]
    aux[...] = aux_n[...]
    cend[0:1, :] = cend_n[0:1, :]

    @pl.when(prev_meta)
    def _():
        om_ref[...] = res

    @pl.when(jnp.logical_not(prev_meta))
    def _():
        og_ref[...] = res


def _rwkv_scan(g_prw, m_prw, params, tabs, reverse, prev=None):
    gblk, mblk, flags = tabs
    nsteps = gblk.shape[0]
    gmap = lambda t, gb, mb, fl: (gb[t], 0)
    mmap = lambda t, gb, mb, fl: (mb[t], 0)
    gmap_prev = lambda t, gb, mb, fl: (gb[jnp.maximum(t - 1, 0)], 0)
    mmap_prev = lambda t, gb, mb, fl: (mb[jnp.maximum(t - 1, 0)], 0)
    cmap = lambda t, gb, mb, fl: (0, 0)
    vec = lambda n: pl.BlockSpec((1, n), cmap)
    in_specs = [
        pl.BlockSpec((CHUNK, N_RW), gmap),
        pl.BlockSpec((CHUNK, N_RW), mmap),
        vec(N_SHIFT_P), vec(D_RWKV),
        pl.BlockSpec((R_LOW, D_RWKV), cmap), vec(D_RWKV),
        pl.BlockSpec((R_LOW, D_RWKV), cmap), vec(D_RWKV), vec(D_RWKV), vec(D_RWKV),
    ]
    args = [g_prw, m_prw, params["mu"], params["w0"], params["w2"], params["a0"], params["a2"],
            params["k_k"], params["k_a"], params["r_k"]]
    if reverse:
        g_yb, m_yb = prev
        in_specs += [
            pl.BlockSpec((CHUNK, 2 * D_RWKV), gmap_prev),
            pl.BlockSpec((CHUNK, 2 * D_RWKV), mmap_prev),
            pl.BlockSpec((R_GATE, D_RWKV), cmap), vec(D_RWKV), vec(D_RWKV),
        ]
        args += [g_yb, m_yb, params["g2"], params["lnx_g"], params["lnx_b"]]
        width, dt, n_aux = D_RWKV, BF16, 2
    else:
        width, dt, n_aux = 2 * D_RWKV, F32, 1
    grid_spec = pltpu.PrefetchScalarGridSpec(
        num_scalar_prefetch=3,
        grid=(nsteps,),
        in_specs=in_specs,
        out_specs=[pl.BlockSpec((CHUNK, width), gmap_prev),
                   pl.BlockSpec((CHUNK, width), mmap_prev)],
        scratch_shapes=[
            pltpu.VMEM((N_PAIR, LANE, LANE), F32),
            pltpu.VMEM((8, N_SHIFT_P), F32),
            pltpu.VMEM((N_OPND, CHUNK, D_RWKV), BF16),
            pltpu.VMEM((n_aux, CHUNK, D_RWKV), F32),
            pltpu.VMEM((8, D_RWKV), F32),
            pltpu.VMEM((N_OPND, CHUNK, D_RWKV), BF16),
            pltpu.VMEM((n_aux, CHUNK, D_RWKV), F32),
            pltpu.VMEM((8, D_RWKV), F32),
        ],
    )
    return pl.pallas_call(
        functools.partial(_rwkv_kernel, reverse=reverse),
        grid_spec=grid_spec,
        out_shape=[jax.ShapeDtypeStruct((g_prw.shape[0], width), dt),
                   jax.ShapeDtypeStruct((m_prw.shape[0], width), dt)],
        compiler_params=pltpu.CompilerParams(
            dimension_semantics=("arbitrary",), vmem_limit_bytes=VMEM_LIMIT),
        name="rwkv_bwd" if reverse else "rwkv_fwd",
    )(gblk, mblk, flags, *args)


def _outproj_kernel(*refs, n_x, tiles_first):
    x_refs = refs[:n_x]
    oa_ref, or_ref, eg_ref, eb_ref, wa_ref, wr_ref, g1_ref, b1_ref, h1_ref, h1b_ref = refs[n_x:]
    h0 = _layer_norm(_select_rows(x_refs, tiles_first, slice(None)), eg_ref[...], eb_ref[...])
    mix = _dot(oa_ref[...], wa_ref[...]) + _dot(or_ref[...], wr_ref[...])
    h1 = _layer_norm(DEEPNORM_ALPHA * h0 + mix, g1_ref[...], b1_ref[...])
    h1_ref[...] = h1
    h1b_ref[...] = h1.astype(BF16)


def _outproj(o_attn, o_rwkv, xs, eg, eb, w_out_bf, g1, b1, tm):
    rows = o_attn.shape[0]
    x_specs, tiles_first = _row_specs(xs, tm, D_MODEL)
    cmap = lambda i: (0, 0)
    vec = pl.BlockSpec((1, D_MODEL), cmap)
    return pl.pallas_call(
        functools.partial(_outproj_kernel, n_x=len(xs), tiles_first=tiles_first),
        grid=(rows // tm,),
        in_specs=x_specs + [
            pl.BlockSpec((tm, D_ATTN), lambda i: (i, 0)),
            pl.BlockSpec((tm, D_RWKV), lambda i: (i, 0)),
            vec, vec,
            pl.BlockSpec((D_ATTN, D_MODEL), lambda i: (0, 0), pipeline_mode=pl.Buffered(1)),
            pl.BlockSpec((D_RWKV, D_MODEL), lambda i: (1, 0), pipeline_mode=pl.Buffered(1)),
            vec, vec,
        ],
        out_specs=[pl.BlockSpec((tm, D_MODEL), lambda i: (i, 0)),
                   pl.BlockSpec((tm, D_MODEL), lambda i: (i, 0))],
        out_shape=[jax.ShapeDtypeStruct((rows, D_MODEL), F32),
                   jax.ShapeDtypeStruct((rows, D_MODEL), BF16)],
        compiler_params=pltpu.CompilerParams(
            dimension_semantics=("arbitrary",), vmem_limit_bytes=VMEM_LIMIT),
        name="outproj_ln1",
    )(*xs, o_attn, o_rwkv, eg, eb, w_out_bf, w_out_bf, g1, b1)


def _ffn_in_kernel(pg_s, pm_s, nx_s, fl_s, h_ref, pg_ref, pm_ref, nx_ref, wg_ref, wu_ref,
                   cw_ref, cb_ref, act_ref, lhs, gate_s, up_s):
    del pg_s, pm_s, nx_s
    i = pl.program_id(0)
    c = pl.program_id(1)
    fl = fl_s[i]

    @pl.when(c == 0)
    def _():
        lhs[0:HALO, :] = jnp.where((fl & 1) == 1, pm_ref[...], pg_ref[...])
        lhs[HALO:HALO + TM_FFN, :] = h_ref[...]
        lhs[HALO + TM_FFN:, :] = jnp.where((fl & 2) == 2, jnp.zeros_like(nx_ref), nx_ref[...])

    units = [(s, k) for s in range(TM_FFN // TM_SLAB) for k in range(FC // FC_SUB)]

    def matmuls(u):
        s, k = units[u]
        cols = slice(k * FC_SUB, (k + 1) * FC_SUB)
        r0 = s * TM_SLAB
        gate_s[u] = _dot(lhs[r0:r0 + TM_SLAB + 2 * HALO, :], wg_ref[:, cols])
        up_s[u] = _dot(lhs[r0 + HALO:r0 + HALO + TM_SLAB, :], wu_ref[:, cols])

    def activation(u):
        s, k = units[u]
        cols = slice(k * FC_SUB, (k + 1) * FC_SUB)
        g_ext = gate_s[u]
        n_ext = TM_SLAB + 2 * HALO
        centre = slice(HALO, HALO + TM_SLAB)
        gate = (pltpu.roll(g_ext, 1, axis=0)[centre] * cw_ref[0:1, cols]
                + g_ext[centre] * cw_ref[1:2, cols]
                + pltpu.roll(g_ext, n_ext - 1, axis=0)[centre] * cw_ref[2:3, cols] + cb_ref[:, cols])
        act = 0.5 * gate * (1.0 + lax.erf(gate * 0.7071067811865476)) * up_s[u]
        act_ref[s * TM_SLAB:(s + 1) * TM_SLAB, cols] = act.astype(BF16)

    matmuls(0)
    for u in range(len(units)):
        if u + 1 < len(units):
            matmuls(u + 1)
        activation(u)


def _ffn_in(g_h1b, m_h1b, w_in_bf, conv_w, conv_b, tabs):
    pg, pm, nx, fl = tabs
    rows = g_h1b.shape[0]
    nfc = D_FF // FC
    n_units = (TM_FFN // TM_SLAB) * (FC // FC_SUB)
    grid_spec = pltpu.PrefetchScalarGridSpec(
        num_scalar_prefetch=4,
        grid=(rows // TM_FFN, nfc),
        in_specs=[
            pl.BlockSpec((TM_FFN, D_MODEL), lambda i, c, *_: (i, 0)),
            pl.BlockSpec((HALO, D_MODEL), lambda i, c, pg, pm, nx, fl: (pg[i], 0)),
            pl.BlockSpec((HALO, D_MODEL), lambda i, c, pg, pm, nx, fl: (pm[i], 0)),
            pl.BlockSpec((HALO, D_MODEL), lambda i, c, pg, pm, nx, fl: (nx[i], 0)),
            pl.BlockSpec((D_MODEL, FC), lambda i, c, *_: (0, c)),
            pl.BlockSpec((D_MODEL, FC), lambda i, c, *_: (0, nfc + c)),
            pl.BlockSpec((3, FC), lambda i, c, *_: (0, c)),
            pl.BlockSpec((1, FC), lambda i, c, *_: (0, c)),
        ],
        out_specs=pl.BlockSpec((TM_FFN, FC), lambda i, c, *_: (i, c)),
        scratch_shapes=[
            pltpu.VMEM((TM_FFN + 2 * HALO, D_MODEL), BF16),
            pltpu.VMEM((n_units, TM_SLAB + 2 * HALO, FC_SUB), F32),
            pltpu.VMEM((n_units, TM_SLAB, FC_SUB), F32),
        ],
    )
    return pl.pallas_call(
        _ffn_in_kernel,
        grid_spec=grid_spec,
        out_shape=jax.ShapeDtypeStruct((rows, D_FF), BF16),
        compiler_params=pltpu.CompilerParams(
            dimension_semantics=("arbitrary", "arbitrary"), vmem_limit_bytes=VMEM_LIMIT),
        name="ffn_in",
    )(pg, pm, nx, fl, g_h1b, g_h1b, m_h1b, g_h1b, w_in_bf, w_in_bf, conv_w, conv_b)


def _ffn_out_kernel(act_ref, h_ref, wo_ref, g2_ref, b2_ref, y_ref):
    n_slab = TM_OUT // SLAB_OUT
    rows = lambda s: slice(s * SLAB_OUT, (s + 1) * SLAB_OUT)
    ffn = {}

    def project(s):
        ffn[s] = _dot(act_ref[rows(s), :], wo_ref[...])

    def norm(s):
        y_ref[rows(s), :] = _layer_norm(DEEPNORM_ALPHA * h_ref[rows(s), :] + ffn.pop(s),
                                        g2_ref[...], b2_ref[...])

    _emit_skewed(n_slab, [(0, project), (1, norm)])


def _ffn_out(act, g_h1, w_out_bf, g2, b2, row0, n_rows):
    tm = TM_OUT
    tile0 = row0 // tm
    cmap = lambda i: (0, 0)
    vec = pl.BlockSpec((1, D_MODEL), cmap)
    return pl.pallas_call(
        _ffn_out_kernel,
        grid=(n_rows // tm,),
        in_specs=[
            pl.BlockSpec((tm, D_FF), lambda i: (tile0 + i, 0)),
            pl.BlockSpec((tm, D_MODEL), lambda i: (tile0 + i, 0)),
            pl.BlockSpec((D_FF, D_MODEL), cmap, pipeline_mode=pl.Buffered(1)),
            vec, vec,
        ],
        out_specs=pl.BlockSpec((tm, D_MODEL), lambda i: (i, 0)),
        out_shape=jax.ShapeDtypeStruct((n_rows, D_MODEL), F32),
        compiler_params=pltpu.CompilerParams(
            dimension_semantics=("arbitrary",), vmem_limit_bytes=VMEM_LIMIT),
        name="ffn_out_ln2",
    )(act, g_h1, w_out_bf, g2, b2)


def _attn_tables(seq_lens):
    rec = []
    base = 0
    for s, t_len in enumerate(seq_lens):
        rows = t_len // GRID_W
        for qb in range(rows // QB_ROWS):
            w0 = min(max(qb * QB_ROWS - WIN_H // 2, 0), rows - 2 * QB_ROWS)
            variant = 0 if qb == 0 else (2 if qb == rows // QB_ROWS - 1 else 1)
            rec.append((variant, (base + qb * QB_TOK) // QB_TOK, (base + w0 * GRID_W) // KV_BLK, s))
        base += t_len
    rec.sort(key=lambda x: x[0])
    arr = np.asarray(rec, np.int32)
    return tuple(jnp.asarray(arr[:, c]) for c in (1, 2, 0, 3))


def _rwkv_tables(seq_lens, reverse):
    gblk, mblk, flags = [], [], []
    base = 0
    for s, t_len in enumerate(seq_lens):
        n = t_len // CHUNK
        first_blk = base // CHUNK
        steps = [(first_blk, 2)] + [(first_blk + c, 0) for c in range(n)]
        if reverse:
            steps = steps[::-1]
        for idx, (blk, fl) in enumerate(steps):
            gblk.append(blk)
            mblk.append(s)
            flags.append(fl | (1 if idx == 0 else 0))
        base += t_len
    gblk.append(gblk[-1])
    mblk.append(mblk[-1])
    flags.append(0)
    return tuple(jnp.asarray(np.asarray(a, np.int32)) for a in (gblk, mblk, flags))


def _ffn_tables(seq_lens):
    pg, pm, nx, fl = [], [], [], []
    base = 0
    total = sum(seq_lens)
    for s, t_len in enumerate(seq_lens):
        for t in range(t_len // TM_FFN):
            r0 = base + t * TM_FFN
            first = t == 0
            last = t == t_len // TM_FFN - 1
            pg.append(max(r0 // HALO - 1, 0))
            pm.append((s * CHUNK + CHUNK - HALO) // HALO)
            nx.append(min((r0 + TM_FFN) // HALO, total // HALO - 1))
            fl.append((1 if first else 0) | (2 if last else 0))
        base += t_len
    return tuple(jnp.asarray(np.asarray(a, np.int32)) for a in (pg, pm, nx, fl))


def _pad_cols(w, n):
    return jnp.pad(w, ((0, 0), (0, n - w.shape[1])))


def _encode_all(xs, meta_tokens, emb_ln_g, emb_ln_b, w_in, attn_rpb, rwkv_mu, rwkv_w0, rwkv_w2,
                rwkv_a0, rwkv_a2, rwkv_g2, rwkv_k_k, rwkv_k_a, rwkv_r_k, rwkv_lnx_g, rwkv_lnx_b,
                w_out, ln1_g, ln1_b, ffn_w_in, ffn_conv_w, ffn_conv_b, ffn_w_out, ln2_g, ln2_b):
    seq_lens = []
    for x in xs:
        seq_lens += [x.shape[1]] * x.shape[0]
    nseq = len(seq_lens)
    x2d = [x.reshape(-1, D_MODEL) for x in xs]
    meta_blk = jnp.concatenate([jnp.zeros((PAD_ROWS, D_MODEL), F32), meta_tokens.astype(F32)], axis=0)
    x_m = jnp.tile(meta_blk, (nseq, 1))

    o1 = N_QKV
    o2 = o1 + 3 * D_RWKV
    o3 = o2 + R_DECAY
    o4 = o3 + R_ICLR
    wl = w_in[0]
    w_qkv = wl[:, :o1].astype(BF16)
    w_rw = jnp.concatenate(
        [wl[:, o1:o2], _pad_cols(wl[:, o2:o3], R_LOW), _pad_cols(wl[:, o3:o4], R_LOW), wl[:, o4:]],
        axis=1).astype(BF16)
    eg = emb_ln_g.reshape(1, D_MODEL)
    eb = emb_ln_b.reshape(1, D_MODEL)
    mu = rwkv_mu[0]
    o5 = 3 * D_RWKV
    mu_p = jnp.concatenate(
        [mu[:, :o5], _pad_cols(mu[:, o5:o5 + R_DECAY], R_LOW), _pad_cols(mu[:, o5 + R_DECAY:], R_LOW)],
        axis=1)
    pad_rows = lambda w: jnp.pad(w, ((0, R_LOW - w.shape[0]), (0, 0))).astype(BF16)
    row = lambda v: v.reshape(1, -1)

    def dir_params(d):
        return dict(mu=mu_p[d:d + 1], w0=row(rwkv_w0[0, d]), w2=pad_rows(rwkv_w2[0, d]),
                    a0=row(rwkv_a0[0, d]), a2=pad_rows(rwkv_a2[0, d]),
                    k_k=row(rwkv_k_k[0]), k_a=row(rwkv_k_a[0]), r_k=row(rwkv_r_k[0]),
                    g2=rwkv_g2[0].astype(BF16), lnx_g=row(rwkv_lnx_g[0]), lnx_b=row(rwkv_lnx_b[0]))

    w_out_bf = w_out[0].astype(BF16)
    ffn_w_in_bf = ffn_w_in[0].astype(BF16)
    ffn_w_out_bf = ffn_w_out[0].astype(BF16)
    bias = _attn_bias_tiles(attn_rpb[0])

    g_qkv = _inproj(x2d, eg, eb, w_qkv, BF16, True, TM)
    g_prw = _inproj(x2d, eg, eb, w_rw, F32, False, TM)
    m_qkv = _inproj([x_m], eg, eb, w_qkv, BF16, True, x_m.shape[0])
    m_prw = _inproj([x_m], eg, eb, w_rw, F32, False, x_m.shape[0])

    g_oattn = _attention(g_qkv, m_qkv, bias, _attn_tables(seq_lens))
    m_oattn = _meta_attention(m_qkv)
    g_yb, m_yb = _rwkv_scan(g_prw, m_prw, dir_params(0), _rwkv_tables(seq_lens, False), False)
    g_orwkv, m_orwkv = _rwkv_scan(g_prw, m_prw, dir_params(1), _rwkv_tables(seq_lens, True), True,
                                  prev=(g_yb, m_yb))

    g1 = ln1_g.reshape(1, D_MODEL)
    b1 = ln1_b.reshape(1, D_MODEL)
    g_h1, g_h1b = _outproj(g_oattn, g_orwkv, x2d, eg, eb, w_out_bf, g1, b1, TM)
    _, m_h1b = _outproj(m_oattn, m_orwkv, [x_m], eg, eb, w_out_bf, g1, b1, x_m.shape[0])

    out_rows = [x.shape[0] * x.shape[1] for x in xs]
    act = _ffn_in(g_h1b, m_h1b, ffn_w_in_bf, ffn_conv_w[0], ffn_conv_b.reshape(1, D_FF),
                  _ffn_tables(seq_lens))
    ys, row0 = [], 0
    for n_rows in out_rows:
        ys.append(_ffn_out(act, g_h1, ffn_w_out_bf, ln2_g.reshape(1, D_MODEL),
                           ln2_b.reshape(1, D_MODEL), row0, n_rows))
        row0 += n_rows
    return [y.reshape(x.shape) for y, x in zip(ys, xs)]


def kernel(x_prompt, x_sample, meta_tokens, emb_ln_g, emb_ln_b, w_in, attn_rpb, rwkv_mu, rwkv_w0, rwkv_w2, rwkv_a0, rwkv_a2, rwkv_g2, rwkv_k_k, rwkv_k_a, rwkv_r_k, rwkv_lnx_g, rwkv_lnx_b, w_out, ln1_g, ln1_b, ffn_w_in, ffn_conv_w, ffn_conv_b, ffn_w_out, ln2_g, ln2_b):
    y_prompt, y_sample = _encode_all(
        [x_prompt, x_sample], meta_tokens, emb_ln_g, emb_ln_b, w_in, attn_rpb, rwkv_mu, rwkv_w0,
        rwkv_w2, rwkv_a0, rwkv_a2, rwkv_g2, rwkv_k_k, rwkv_k_a, rwkv_r_k, rwkv_lnx_g, rwkv_lnx_b,
        w_out, ln1_g, ln1_b, ffn_w_in, ffn_conv_w, ffn_conv_b, ffn_w_out, ln2_g, ln2_b)
    return (y_prompt, y_sample)
```

```python
import functools

import numpy as np
import jax
import jax.numpy as jnp
from jax import lax
from jax.experimental import pallas as pl
from jax.experimental.pallas import tpu as pltpu

D_MODEL = 2048
N_META = 16
GRID_W = 64
WIN_H = 8
WIN_W = 16
D_ATTN = 1024
HD_ATTN = 128
H_ATTN = D_ATTN // HD_ATTN
D_RWKV = 1024
HD_RWKV = 64
R_DECAY = 96
R_ICLR = 96
R_GATE = 256
D_FF = 5632
DEEPNORM_ALPHA = 2.0 ** 0.25
LN_EPS = 1e-5
GN_EPS = 64e-5
NEG_INF = -1e30

LANE = 128
CHUNK = 64
PAD_ROWS = CHUNK - N_META
R_LOW = 128
N_SHIFT_P = 3 * D_RWKV + 2 * R_LOW
N_RW = N_SHIFT_P + R_GATE
N_QKV = 3 * D_ATTN
N_PAIR = D_RWKV // LANE
QB_ROWS = 8
QB_TOK = QB_ROWS * GRID_W
KV_BLK = 256
WIN_TOK = 4 * KV_BLK
TM = 512
TM_FFN = 1024
TM_SLAB = 512
TM_OUT = 256
SLAB_OUT = 128
N_SLAB = 4
FC = 512
FC_SUB = 256
HALO = 16
VMEM_LIMIT = 60 * 1024 * 1024

F32 = jnp.float32
BF16 = jnp.bfloat16


def _layer_norm(x, g, b):
    mu = jnp.mean(x, axis=-1, keepdims=True)
    xc = x - mu
    var = jnp.mean(xc * xc, axis=-1, keepdims=True)
    return xc * lax.rsqrt(var + LN_EPS) * g + b


def _dot(a, b):
    return jnp.dot(a, b, preferred_element_type=F32)


def _dot_nt(a, b):
    return lax.dot_general(a, b, (((1,), (1,)), ((), ())), preferred_element_type=F32)


def _bmm(a, b):
    return jnp.einsum("bik,bkj->bij", a, b, preferred_element_type=F32)


def _bmm_nt(a, b):
    return jnp.einsum("bik,bjk->bij", a, b, preferred_element_type=F32)


def _bmm_tn(a, b):
    return jnp.einsum("bki,bkj->bij", a, b, preferred_element_type=F32)


def _select_rows(x_refs, tiles_first, rows):
    if len(x_refs) == 1:
        return x_refs[0][rows, :]
    return jnp.where(pl.program_id(0) < tiles_first, x_refs[0][rows, :], x_refs[1][rows, :])


def _emit_skewed(n_slabs, stages):
    depth = max(lag for lag, _ in stages)
    for k in range(n_slabs + depth):
        for lag, fn in stages:
            if 0 <= k - lag < n_slabs:
                fn(k - lag)


def _row_specs(xs, tm, width):
    if len(xs) == 1:
        return [pl.BlockSpec((tm, width), lambda i: (i, 0))], 0
    tiles_first = xs[0].shape[0] // tm
    return [pl.BlockSpec((tm, width), lambda i: (jnp.minimum(i, tiles_first - 1), 0)),
            pl.BlockSpec((tm, width), lambda i: (jnp.maximum(i - tiles_first, 0), 0))], tiles_first


def _inproj_kernel(*refs, n_x, tiles_first, q_scale):
    x_refs = refs[:n_x]
    g_ref, b_ref, w_ref, o_ref = refs[n_x:n_x + 4]
    h0_ref = refs[n_x + 4] if q_scale else None
    slab = o_ref.shape[0] // N_SLAB
    rows = lambda s: slice(s * slab, (s + 1) * slab)
    h = {}

    def norm(s):
        h0 = _layer_norm(_select_rows(x_refs, tiles_first, rows(s)), g_ref[...], b_ref[...])
        if h0_ref is not None:
            h0_ref[rows(s), :] = h0
        h[s] = h0.astype(BF16)

    def project(s):
        acc = _dot(h.pop(s), w_ref[...])
        if q_scale:
            o_ref[rows(s), :D_ATTN] = (acc[:, :D_ATTN] * (HD_ATTN ** -0.5)).astype(o_ref.dtype)
            o_ref[rows(s), D_ATTN:] = acc[:, D_ATTN:].astype(o_ref.dtype)
        else:
            o_ref[rows(s), :] = acc.astype(o_ref.dtype)

    _emit_skewed(N_SLAB, [(1, project), (0, norm)])


def _inproj(xs, ln_g, ln_b, w, out_dtype, q_scale, tm):
    rows = sum(x.shape[0] for x in xs)
    n_out = w.shape[1]
    x_specs, tiles_first = _row_specs(xs, tm, D_MODEL)
    cmap = lambda i: (0, 0)
    out_specs = pl.BlockSpec((tm, n_out), lambda i: (i, 0))
    out_shape = jax.ShapeDtypeStruct((rows, n_out), out_dtype)
    if q_scale:
        out_specs = [out_specs, pl.BlockSpec((tm, D_MODEL), lambda i: (i, 0))]
        out_shape = [out_shape, jax.ShapeDtypeStruct((rows, D_MODEL), F32)]
    return pl.pallas_call(
        functools.partial(_inproj_kernel, n_x=len(xs), tiles_first=tiles_first, q_scale=q_scale),
        grid=(rows // tm,),
        in_specs=x_specs + [
            pl.BlockSpec((1, D_MODEL), cmap),
            pl.BlockSpec((1, D_MODEL), cmap),
            pl.BlockSpec((D_MODEL, n_out), cmap, pipeline_mode=pl.Buffered(1)),
        ],
        out_specs=out_specs,
        out_shape=out_shape,
        compiler_params=pltpu.CompilerParams(
            dimension_semantics=("arbitrary",), vmem_limit_bytes=VMEM_LIMIT),
        name="inproj_qkv" if q_scale else "inproj_rwkv",
    )(*xs, ln_g, ln_b, w)


def _window_rows(variant, i):
    half = WIN_H // 2
    if variant == 0:
        off, r0 = 0, max(i - half, 0)
    elif variant == 1:
        off, r0 = -half, i - half
    else:
        off, r0 = -WIN_H, min(i - half, 0)
    return [j for j in range(2 * QB_ROWS) if r0 <= j + off < r0 + WIN_H], off


def _attn_tile_plan(variant):
    plan = []
    for i in range(QB_ROWS):
        rows, off = _window_rows(variant, i)
        tiles = []
        for jt in range(QB_ROWS):
            first, second = 2 * jt in rows, 2 * jt + 1 in rows
            if not (first or second):
                continue
            kind = 0 if (first and second) else (1 if second else 2)
            e = 2 * jt + off - i + WIN_H
            assert (not first or 0 <= e - 1 <= 2 * WIN_H - 2) and (not second or 0 <= e <= 2 * WIN_H - 2)
            tiles.append((jt, kind, e))
        plan.append(tiles)
    return plan


def _attn_kernel(qblk_s, kvblk_s, var_s, mseq_s, q_ref, k_ref, v_ref,
                 mk_ref, mv_ref, bt_ref, o_ref, p_scr):
    del qblk_s, kvblk_s, mseq_s
    variant = var_s[pl.program_id(1)]
    q = q_ref[...]
    s_win = _dot_nt(q, k_ref[...])
    km = mk_ref[PAD_ROWS:, :]
    vm = mv_ref[PAD_ROWS:, :]
    sm = _dot_nt(q, km)

    def softmax_pv(plan):
        def score_tile(i, jt, kind, e):
            rows = slice(i * GRID_W, (i + 1) * GRID_W)
            return s_win[rows, jt * LANE:(jt + 1) * LANE] + bt_ref[0, kind, e]

        tile_max = []
        for i, tiles in enumerate(plan):
            t = None
            for jt, kind, e in tiles:
                x = score_tile(i, jt, kind, e)
                t = x if t is None else jnp.maximum(t, x)
            tile_max.append(t)
        m = jnp.maximum(jnp.max(jnp.concatenate(tile_max, axis=0), axis=-1, keepdims=True),
                        jnp.max(sm, axis=-1, keepdims=True))
        pm = jnp.exp(sm - m)
        p_scr[...] = jnp.zeros_like(p_scr)
        tile_sum = []
        for i, tiles in enumerate(plan):
            rows = slice(i * GRID_W, (i + 1) * GRID_W)
            mi = m[rows]
            t = None
            for jt, kind, e in tiles:
                p = jnp.exp(score_tile(i, jt, kind, e) - mi)
                p_scr[rows, jt * LANE:(jt + 1) * LANE] = p.astype(BF16)
                t = p if t is None else t + p
            tile_sum.append(t)
        l = (jnp.sum(jnp.concatenate(tile_sum, axis=0), axis=-1, keepdims=True)
             + jnp.sum(pm, axis=-1, keepdims=True))
        o = _dot(pm.astype(BF16), vm)
        o = o + _dot(p_scr[...], v_ref[...])
        o_ref[...] = (o / l).astype(BF16)

    for v in range(3):
        pl.when(variant == v)(functools.partial(softmax_pv, _attn_tile_plan(v)))


def _attention(g_qkv, m_qkv, bias, tabs):
    qblk, kvblk, var, mseq = tabs
    nsteps = qblk.shape[0]
    rows = g_qkv.shape[0]

    def window_spec(col0):
        return pl.BlockSpec((pl.Element(WIN_TOK), pl.Element(HD_ATTN)),
                            lambda h, t, qb, kb, vr, ms: (kb[t] * KV_BLK, (col0 + h) * HD_ATTN))

    in_specs = [pl.BlockSpec((QB_TOK, HD_ATTN), lambda h, t, qb, kb, vr, ms: (qb[t], h)),
                window_spec(H_ATTN), window_spec(2 * H_ATTN)]
    in_specs += [
        pl.BlockSpec((CHUNK, HD_ATTN), lambda h, t, qb, kb, vr, ms: (ms[t], H_ATTN + h)),
        pl.BlockSpec((CHUNK, HD_ATTN), lambda h, t, qb, kb, vr, ms: (ms[t], 2 * H_ATTN + h)),
        pl.BlockSpec((1, 3, 2 * WIN_H, GRID_W, LANE), lambda h, t, qb, kb, vr, ms: (h, 0, 0, 0, 0)),
    ]
    grid_spec = pltpu.PrefetchScalarGridSpec(
        num_scalar_prefetch=4,
        grid=(H_ATTN, nsteps),
        in_specs=in_specs,
        out_specs=pl.BlockSpec((QB_TOK, HD_ATTN), lambda h, t, qb, kb, vr, ms: (qb[t], h)),
        scratch_shapes=[pltpu.VMEM((QB_TOK, WIN_TOK), BF16)],
    )
    return pl.pallas_call(
        _attn_kernel,
        grid_spec=grid_spec,
        out_shape=jax.ShapeDtypeStruct((rows, D_ATTN), BF16),
        compiler_params=pltpu.CompilerParams(
            dimension_semantics=("arbitrary", "arbitrary"), vmem_limit_bytes=VMEM_LIMIT),
        name="nbr_attention",
    )(qblk, kvblk, var, mseq, g_qkv, g_qkv, g_qkv, m_qkv, m_qkv, bias)


def _meta_attn_kernel(qkv_ref, o_ref):
    for h in range(H_ATTN):
        cols = lambda part: slice((part * H_ATTN + h) * HD_ATTN, (part * H_ATTN + h + 1) * HD_ATTN)
        q = qkv_ref[:, cols(0)]
        km = qkv_ref[PAD_ROWS:, cols(1)]
        vm = qkv_ref[PAD_ROWS:, cols(2)]
        s = _dot_nt(q, km)
        m = jnp.max(s, axis=-1, keepdims=True)
        p = jnp.exp(s - m)
        l = jnp.sum(p, axis=-1, keepdims=True)
        o_ref[:, h * HD_ATTN:(h + 1) * HD_ATTN] = (_dot(p.astype(BF16), vm) / l).astype(BF16)


def _meta_attention(m_qkv):
    nseq = m_qkv.shape[0] // CHUNK
    return pl.pallas_call(
        _meta_attn_kernel,
        grid=(nseq,),
        in_specs=[pl.BlockSpec((CHUNK, N_QKV), lambda s: (s, 0))],
        out_specs=pl.BlockSpec((CHUNK, D_ATTN), lambda s: (s, 0)),
        out_shape=jax.ShapeDtypeStruct((m_qkv.shape[0], D_ATTN), BF16),
        compiler_params=pltpu.CompilerParams(dimension_semantics=("arbitrary",)),
        name="meta_attention",
    )(m_qkv)


def _attn_bias_tiles(rpb):
    qc = np.arange(GRID_W)
    kc = np.arange(GRID_W)
    c0 = np.clip(qc - WIN_W // 2, 0, GRID_W - WIN_W)
    colmask = (kc[None, :] >= c0[:, None]) & (kc[None, :] < c0[:, None] + WIN_W)
    dc = np.clip(kc[None, :] - qc[:, None], -(WIN_W - 1), WIN_W - 1) + (WIN_W - 1)
    n_dr = 2 * WIN_H - 1
    n_dc = 2 * WIN_W - 1
    dc_onehot = (dc[None] == np.arange(n_dc)[:, None, None]).astype(np.float32)
    toeplitz = jnp.einsum("hrd,dqk->hrqk", rpb, jnp.asarray(dc_onehot),
                          precision=lax.Precision.HIGHEST)
    toeplitz = jnp.where(jnp.asarray(colmask)[None, None], toeplitz, NEG_INF)
    masked = jnp.full((H_ATTN, 1, GRID_W, GRID_W), NEG_INF, F32)
    padded = jnp.concatenate([masked, toeplitz, masked], axis=1)
    first = padded[:, 0:n_dr + 1]
    second = padded[:, 1:n_dr + 2]
    off = jnp.full_like(first, NEG_INF)
    kinds = [jnp.concatenate([a, b], axis=-1) for a, b in ((first, second), (off, second), (first, off))]
    return jnp.stack(kinds, axis=1)


def _split_bf16(x, n):
    parts = []
    rem = x
    for _ in range(n):
        p = rem.astype(BF16)
        parts.append(p)
        rem = rem - p.astype(F32)
    return parts


def _seg_sum(x, e2):
    stacked = jnp.concatenate([x[:, j * LANE:(j + 1) * LANE] for j in range(N_PAIR)], axis=0)
    hi, lo = _split_bf16(stacked, 2)
    res = _dot(hi, e2) + _dot(lo, e2)
    return jnp.concatenate([res[j * CHUNK:(j + 1) * CHUNK] for j in range(N_PAIR)], axis=1)


_OP_A, _OP_R, _OP_BH, _OP_KH, _OP_V, _OP_B, _OP_K = range(7)
N_OPND = 7


def _rwkv_kernel(gblk_s, mblk_s, flag_s, *refs, reverse):
    del gblk_s, mblk_s
    if reverse:
        (gp_ref, mp_ref, mu_ref, w0_ref, w2_ref, a0_ref, a2_ref, kk_ref, ka_ref, rk_ref,
         gyb_ref, myb_ref, g2_ref, lg_ref, lb_ref, og_ref, om_ref,
         state, carry, opnd, aux, cend, opnd_n, aux_n, cend_n) = refs
    else:
        (gp_ref, mp_ref, mu_ref, w0_ref, w2_ref, a0_ref, a2_ref, kk_ref, ka_ref, rk_ref,
         og_ref, om_ref, state, carry, opnd, aux, cend, opnd_n, aux_n, cend_n) = refs
    step = pl.program_id(0)
    fl = flag_s[step]
    fl_prev = flag_s[jnp.maximum(step - 1, 0)]
    is_first = (fl & 1) == 1
    is_meta = (fl & 2) == 2
    prev_first = (fl_prev & 1) == 1
    prev_meta = (fl_prev & 2) == 2

    @pl.when(step == 0)
    def _():
        opnd[...] = jnp.zeros_like(opnd)
        aux[...] = jnp.zeros_like(aux)
        cend[...] = jnp.zeros_like(cend)

    @pl.when(is_first)
    def _():
        carry[...] = jnp.zeros_like(carry)

    @pl.when(prev_first)
    def _():
        state[...] = jnp.zeros_like(state)

    ii = lax.broadcasted_iota(jnp.int32, (LANE, LANE), 0)
    jj = lax.broadcasted_iota(jnp.int32, (LANE, LANE), 1)
    same = (ii // CHUNK) == (jj // CHUNK)
    e2 = jnp.where(same, 1.0, 0.0).astype(BF16)

    ti2 = ii % CHUNK
    tj2 = jj % CHUNK
    if reverse:
        strict = jnp.logical_and(same, tj2 > ti2)
        incl = jnp.logical_and(same, tj2 >= ti2)
    else:
        strict = jnp.logical_and(same, tj2 < ti2)
        incl = jnp.logical_and(same, tj2 <= ti2)
    eye = jnp.where(ii == jj, 1.0, 0.0).astype(F32)
    lane64 = lax.broadcasted_iota(jnp.int32, (1, CHUNK, LANE), 2) < HD_RWKV

    def tiles(x):
        return jnp.stack([x[:, j * LANE:(j + 1) * LANE] for j in range(N_PAIR)], axis=0)

    def stack_masked(x):
        zero = jnp.zeros_like(x)
        return jnp.concatenate([jnp.where(lane64, x, zero), jnp.where(lane64, zero, x)], axis=1)

    def stack_dup(x):
        return jnp.concatenate([x, x], axis=1)

    def prepare():
        praw = jnp.where(is_meta, mp_ref[...], gp_ref[...])
        row = lax.broadcasted_iota(jnp.int32, (CHUNK, 1), 0)
        pad = jnp.logical_and(is_meta, row < PAD_ROWS)
        p_sh = jnp.where(pad, 0.0, praw[:, :N_SHIFT_P])
        gd = praw[:, N_SHIFT_P:]
        if reverse:
            rolled = pltpu.roll(p_sh, CHUNK - 1, axis=0)
            nb = jnp.where(row == CHUNK - 1, carry[0:1, :], rolled)
            carry[...] = p_sh[0:8, :]
        else:
            rolled = pltpu.roll(p_sh, 1, axis=0)
            nb = jnp.where(row == 0, carry[7:8, :], rolled)
            carry[...] = p_sh[CHUNK - 8:CHUNK, :]
        f = p_sh + (nb - p_sh) * mu_ref[...]
        r = f[:, 0:D_RWKV]
        k = f[:, D_RWKV:2 * D_RWKV]
        v = f[:, 2 * D_RWKV:3 * D_RWKV]
        wd = f[:, 3 * D_RWKV:3 * D_RWKV + R_LOW]
        ad = f[:, 3 * D_RWKV + R_LOW:3 * D_RWKV + 2 * R_LOW]
        opnd_n[_OP_V] = v.astype(BF16)
        yield
        wl = w0_ref[...] + _dot(jnp.tanh(wd).astype(BF16), w2_ref[...])
        nwl = -wl
        softplus = jnp.maximum(nwl, 0.0) + jnp.log(1.0 + jnp.exp(-jnp.abs(nwl)))
        lw = -jnp.exp(-softplus - 0.5)
        yield
        a = 1.0 / (1.0 + jnp.exp(-(a0_ref[...] + _dot(ad.astype(BF16), a2_ref[...]))))
        kmod = k * (1.0 + (a - 1.0) * ka_ref[...])
        yield
        kkr = k * kk_ref[...]
        kk = kkr / jnp.maximum(jnp.sqrt(_seg_sum(kkr * kkr, e2)), 1e-12)
        yield
        aux_n[0] = _seg_sum(r * kmod * rk_ref[...], e2) * v
        if reverse:
            aux_n[1] = _dot((1.0 / (1.0 + jnp.exp(-gd))).astype(BF16), g2_ref[...])
        yield
        ti = lax.broadcasted_iota(jnp.int32, (CHUNK, CHUNK), 0)
        tj = lax.broadcasted_iota(jnp.int32, (CHUNK, CHUNK), 1)
        tri = jnp.where((tj >= ti) if reverse else (tj <= ti), 1.0, 0.0).astype(BF16)
        cum = sum(_dot(tri, part) for part in _split_bf16(lw, 3))
        cum_end = cum[0:1, :] if reverse else cum[CHUNK - 1:CHUNK, :]
        cend_n[0:1, :] = cum_end
        yield
        opnd_n[_OP_R] = (r * jnp.exp(cum)).astype(BF16)
        yield
        e_neg = jnp.exp(-cum)
        bb = kk * a
        opnd_n[_OP_K] = (kmod * e_neg).astype(BF16)
        opnd_n[_OP_B] = (bb * e_neg).astype(BF16)
        yield
        opnd_n[_OP_A] = (-kk * jnp.exp(cum - lw)).astype(BF16)
        yield
        e_end = jnp.exp(cum_end - cum)
        opnd_n[_OP_KH] = (kmod * e_end).astype(BF16)
        opnd_n[_OP_BH] = (bb * e_end).astype(BF16)

    prep_steps = prepare()

    def emit_prep():
        next(prep_steps, None)

    v3 = tiles(opnd[_OP_V])
    a_s = stack_masked(tiles(opnd[_OP_A]))
    r_s = stack_masked(tiles(opnd[_OP_R]))
    bh_s = stack_masked(tiles(opnd[_OP_BH]))
    kh_s = stack_masked(tiles(opnd[_OP_KH]))
    v_s = stack_masked(v3)
    b_d = stack_dup(tiles(opnd[_OP_B]))
    k_d = stack_dup(tiles(opnd[_OP_K]))
    v_d = stack_dup(v3)
    cum_end_prev = cend[0:1, :]
    emit_prep()
    m1 = _bmm_nt(jnp.concatenate([a_s, r_s], axis=1),
                 jnp.concatenate([b_d, k_d], axis=1))
    emit_prep()
    n_ab = jnp.where(strict, m1[:, :LANE, :LANE], 0.0)
    a_ak = jnp.where(strict, m1[:, :LANE, LANE:], 0.0)
    a_rb = jnp.where(incl, m1[:, LANE:, :LANE], 0.0)
    a_rk = jnp.where(incl, m1[:, LANE:, LANE:], 0.0)
    eye_b = eye.astype(BF16)
    tinv = (eye + n_ab).astype(BF16)
    npow = n_ab.astype(BF16)
    for _ in range(5):
        npow = _bmm(npow, npow).astype(BF16)
        emit_prep()
        tinv = _bmm(tinv, eye_b + npow).astype(BF16)
    gy = _bmm(jnp.concatenate([a_ak, a_rk], axis=1).astype(BF16), v_d)
    emit_prep()
    g_s = jnp.where(same, gy[:, :LANE], 0.0)
    y0_s = jnp.where(same, gy[:, LANE:], 0.0)
    tz = _bmm(tinv, jnp.concatenate([a_s, g_s.astype(BF16)], axis=2))
    emit_prep()
    w_s = tz[:, :, :LANE]
    u0_s = tz[:, :, LANE:]
    s_add = _bmm_tn(kh_s, v_s)
    s_old = state[...]
    xr = _bmm(jnp.concatenate([w_s.astype(BF16), r_s], axis=1), s_old.astype(BF16))
    emit_prep()
    u_s = xr[:, :LANE] + u0_s
    y_s = xr[:, LANE:] + _bmm(a_rb.astype(BF16), u_s.astype(BF16)) + y0_s
    emit_prep()
    y3 = y_s[:, :CHUNK] + y_s[:, CHUNK:]
    g_end = jnp.exp(jnp.swapaxes(jnp.broadcast_to(tiles(cum_end_prev), (N_PAIR, LANE, LANE)), 1, 2))
    state[...] = g_end * s_old + _bmm_tn(bh_s, u_s.astype(BF16)) + s_add
    y = jnp.concatenate([y3[j] for j in range(N_PAIR)], axis=1)

    if not reverse:
        res = jnp.concatenate([y, aux[0]], axis=1)
    else:
        yb = jnp.where(prev_meta, myb_ref[...], gyb_ref[...])
        yy = y + yb[:, :D_RWKV]
        mean = _seg_sum(yy, e2) * (1.0 / HD_RWKV)
        yc = yy - mean
        var = _seg_sum(yc * yc, e2) * (1.0 / HD_RWKV)
        yn = yc * lax.rsqrt(var + GN_EPS) * lg_ref[...] + lb_ref[...]
        res = ((yn + yb[:, D_RWKV:] + aux[0]) * aux[1]).astype(BF16)

    for _ in prep_steps:
        pass
    opnd[...] = opnd_n[...]
    aux[...] = aux_n[...]
    cend[0:1, :] = cend_n[0:1, :]

    @pl.when(prev_meta)
    def _():
        om_ref[...] = res

    @pl.when(jnp.logical_not(prev_meta))
    def _():
        og_ref[...] = res


def _rwkv_scan(g_prw, m_prw, params, tabs, reverse, prev=None):
    gblk, mblk, flags = tabs
    nsteps = gblk.shape[0]
    gmap = lambda t, gb, mb, fl: (gb[t], 0)
    mmap = lambda t, gb, mb, fl: (mb[t], 0)
    gmap_prev = lambda t, gb, mb, fl: (gb[jnp.maximum(t - 1, 0)], 0)
    mmap_prev = lambda t, gb, mb, fl: (mb[jnp.maximum(t - 1, 0)], 0)
    cmap = lambda t, gb, mb, fl: (0, 0)
    vec = lambda n: pl.BlockSpec((1, n), cmap)
    in_specs = [
        pl.BlockSpec((CHUNK, N_RW), gmap),
        pl.BlockSpec((CHUNK, N_RW), mmap),
        vec(N_SHIFT_P), vec(D_RWKV),
        pl.BlockSpec((R_LOW, D_RWKV), cmap), vec(D_RWKV),
        pl.BlockSpec((R_LOW, D_RWKV), cmap), vec(D_RWKV), vec(D_RWKV), vec(D_RWKV),
    ]
    args = [g_prw, m_prw, params["mu"], params["w0"], params["w2"], params["a0"], params["a2"],
            params["k_k"], params["k_a"], params["r_k"]]
    if reverse:
        g_yb, m_yb = prev
        in_specs += [
            pl.BlockSpec((CHUNK, 2 * D_RWKV), gmap_prev),
            pl.BlockSpec((CHUNK, 2 * D_RWKV), mmap_prev),
            pl.BlockSpec((R_GATE, D_RWKV), cmap), vec(D_RWKV), vec(D_RWKV),
        ]
        args += [g_yb, m_yb, params["g2"], params["lnx_g"], params["lnx_b"]]
        width, dt, n_aux = D_RWKV, BF16, 2
    else:
        width, dt, n_aux = 2 * D_RWKV, F32, 1
    grid_spec = pltpu.PrefetchScalarGridSpec(
        num_scalar_prefetch=3,
        grid=(nsteps,),
        in_specs=in_specs,
        out_specs=[pl.BlockSpec((CHUNK, width), gmap_prev),
                   pl.BlockSpec((CHUNK, width), mmap_prev)],
        scratch_shapes=[
            pltpu.VMEM((N_PAIR, LANE, LANE), F32),
            pltpu.VMEM((8, N_SHIFT_P), F32),
            pltpu.VMEM((N_OPND, CHUNK, D_RWKV), BF16),
            pltpu.VMEM((n_aux, CHUNK, D_RWKV), F32),
            pltpu.VMEM((8, D_RWKV), F32),
            pltpu.VMEM((N_OPND, CHUNK, D_RWKV), BF16),
            pltpu.VMEM((n_aux, CHUNK, D_RWKV), F32),
            pltpu.VMEM((8, D_RWKV), F32),
        ],
    )
    return pl.pallas_call(
        functools.partial(_rwkv_kernel, reverse=reverse),
        grid_spec=grid_spec,
        out_shape=[jax.ShapeDtypeStruct((g_prw.shape[0], width), dt),
                   jax.ShapeDtypeStruct((m_prw.shape[0], width), dt)],
        compiler_params=pltpu.CompilerParams(
            dimension_semantics=("arbitrary",), vmem_limit_bytes=VMEM_LIMIT),
        name="rwkv_bwd" if reverse else "rwkv_fwd",
    )(gblk, mblk, flags, *args)


def _outproj_kernel(h0_ref, oa_ref, or_ref, wa_ref, wr_ref, g1_ref, b1_ref, h1_ref, h1b_ref):
    mix = _dot(oa_ref[...], wa_ref[...]) + _dot(or_ref[...], wr_ref[...])
    h1 = _layer_norm(DEEPNORM_ALPHA * h0_ref[...] + mix, g1_ref[...], b1_ref[...])
    h1_ref[...] = h1
    h1b_ref[...] = h1.astype(BF16)


def _outproj(o_attn, o_rwkv, h0, w_out_bf, g1, b1, tm):
    rows = o_attn.shape[0]
    cmap = lambda i: (0, 0)
    vec = pl.BlockSpec((1, D_MODEL), cmap)
    return pl.pallas_call(
        _outproj_kernel,
        grid=(rows // tm,),
        in_specs=[
            pl.BlockSpec((tm, D_MODEL), lambda i: (i, 0)),
            pl.BlockSpec((tm, D_ATTN), lambda i: (i, 0)),
            pl.BlockSpec((tm, D_RWKV), lambda i: (i, 0)),
            pl.BlockSpec((D_ATTN, D_MODEL), lambda i: (0, 0), pipeline_mode=pl.Buffered(1)),
            pl.BlockSpec((D_RWKV, D_MODEL), lambda i: (1, 0), pipeline_mode=pl.Buffered(1)),
            vec, vec,
        ],
        out_specs=[pl.BlockSpec((tm, D_MODEL), lambda i: (i, 0)),
                   pl.BlockSpec((tm, D_MODEL), lambda i: (i, 0))],
        out_shape=[jax.ShapeDtypeStruct((rows, D_MODEL), F32),
                   jax.ShapeDtypeStruct((rows, D_MODEL), BF16)],
        compiler_params=pltpu.CompilerParams(
            dimension_semantics=("arbitrary",), vmem_limit_bytes=VMEM_LIMIT),
        name="outproj_ln1",
    )(h0, o_attn, o_rwkv, w_out_bf, w_out_bf, g1, b1)


def _ffn_in_kernel(pg_s, pm_s, nx_s, fl_s, h_ref, pg_ref, pm_ref, nx_ref, wg_ref, wu_ref,
                   cw_ref, cb_ref, act_ref, lhs, gate_s, up_s):
    del pg_s, pm_s, nx_s
    i = pl.program_id(0)
    c = pl.program_id(1)
    fl = fl_s[i]

    @pl.when(c == 0)
    def _():
        lhs[0:HALO, :] = jnp.where((fl & 1) == 1, pm_ref[...], pg_ref[...])
        lhs[HALO:HALO + TM_FFN, :] = h_ref[...]
        lhs[HALO + TM_FFN:, :] = jnp.where((fl & 2) == 2, jnp.zeros_like(nx_ref), nx_ref[...])

    units = [(s, k) for s in range(TM_FFN // TM_SLAB) for k in range(FC // FC_SUB)]

    def matmuls(u):
        s, k = units[u]
        cols = slice(k * FC_SUB, (k + 1) * FC_SUB)
        r0 = s * TM_SLAB
        gate_s[u] = _dot(lhs[r0:r0 + TM_SLAB + 2 * HALO, :], wg_ref[:, cols])
        up_s[u] = _dot(lhs[r0 + HALO:r0 + HALO + TM_SLAB, :], wu_ref[:, cols])

    def activation(u):
        s, k = units[u]
        cols = slice(k * FC_SUB, (k + 1) * FC_SUB)
        g_ext = gate_s[u]
        n_ext = TM_SLAB + 2 * HALO
        centre = slice(HALO, HALO + TM_SLAB)
        gate = (pltpu.roll(g_ext, 1, axis=0)[centre] * cw_ref[0:1, cols]
                + g_ext[centre] * cw_ref[1:2, cols]
                + pltpu.roll(g_ext, n_ext - 1, axis=0)[centre] * cw_ref[2:3, cols] + cb_ref[:, cols])
        act = 0.5 * gate * (1.0 + lax.erf(gate * 0.7071067811865476)) * up_s[u]
        act_ref[s * TM_SLAB:(s + 1) * TM_SLAB, cols] = act.astype(BF16)

    matmuls(0)
    for u in range(len(units)):
        if u + 1 < len(units):
            matmuls(u + 1)
        activation(u)


def _ffn_in(g_h1b, m_h1b, w_in_bf, conv_w, conv_b, tabs):
    pg, pm, nx, fl = tabs
    rows = g_h1b.shape[0]
    nfc = D_FF // FC
    n_units = (TM_FFN // TM_SLAB) * (FC // FC_SUB)
    grid_spec = pltpu.PrefetchScalarGridSpec(
        num_scalar_prefetch=4,
        grid=(rows // TM_FFN, nfc),
        in_specs=[
            pl.BlockSpec((TM_FFN, D_MODEL), lambda i, c, *_: (i, 0)),
            pl.BlockSpec((HALO, D_MODEL), lambda i, c, pg, pm, nx, fl: (pg[i], 0)),
            pl.BlockSpec((HALO, D_MODEL), lambda i, c, pg, pm, nx, fl: (pm[i], 0)),
            pl.BlockSpec((HALO, D_MODEL), lambda i, c, pg, pm, nx, fl: (nx[i], 0)),
            pl.BlockSpec((D_MODEL, FC), lambda i, c, *_: (0, c)),
            pl.BlockSpec((D_MODEL, FC), lambda i, c, *_: (0, nfc + c)),
            pl.BlockSpec((3, FC), lambda i, c, *_: (0, c)),
            pl.BlockSpec((1, FC), lambda i, c, *_: (0, c)),
        ],
        out_specs=pl.BlockSpec((TM_FFN, FC), lambda i, c, *_: (i, c)),
        scratch_shapes=[
            pltpu.VMEM((TM_FFN + 2 * HALO, D_MODEL), BF16),
            pltpu.VMEM((n_units, TM_SLAB + 2 * HALO, FC_SUB), F32),
            pltpu.VMEM((n_units, TM_SLAB, FC_SUB), F32),
        ],
    )
    return pl.pallas_call(
        _ffn_in_kernel,
        grid_spec=grid_spec,
        out_shape=jax.ShapeDtypeStruct((rows, D_FF), BF16),
        compiler_params=pltpu.CompilerParams(
            dimension_semantics=("arbitrary", "arbitrary"), vmem_limit_bytes=VMEM_LIMIT),
        name="ffn_in",
    )(pg, pm, nx, fl, g_h1b, g_h1b, m_h1b, g_h1b, w_in_bf, w_in_bf, conv_w, conv_b)


def _ffn_out_kernel(act_ref, h_ref, wo_ref, g2_ref, b2_ref, y_ref):
    n_slab = TM_OUT // SLAB_OUT
    rows = lambda s: slice(s * SLAB_OUT, (s + 1) * SLAB_OUT)
    ffn = {}

    def project(s):
        ffn[s] = _dot(act_ref[rows(s), :], wo_ref[...])

    def norm(s):
        y_ref[rows(s), :] = _layer_norm(DEEPNORM_ALPHA * h_ref[rows(s), :] + ffn.pop(s),
                                        g2_ref[...], b2_ref[...])

    _emit_skewed(n_slab, [(0, project), (1, norm)])


def _ffn_out(act, g_h1, w_out_bf, g2, b2, row0, n_rows):
    tm = TM_OUT
    tile0 = row0 // tm
    cmap = lambda i: (0, 0)
    vec = pl.BlockSpec((1, D_MODEL), cmap)
    return pl.pallas_call(
        _ffn_out_kernel,
        grid=(n_rows // tm,),
        in_specs=[
            pl.BlockSpec((tm, D_FF), lambda i: (tile0 + i, 0)),
            pl.BlockSpec((tm, D_MODEL), lambda i: (tile0 + i, 0)),
            pl.BlockSpec((D_FF, D_MODEL), cmap, pipeline_mode=pl.Buffered(1)),
            vec, vec,
        ],
        out_specs=pl.BlockSpec((tm, D_MODEL), lambda i: (i, 0)),
        out_shape=jax.ShapeDtypeStruct((n_rows, D_MODEL), F32),
        compiler_params=pltpu.CompilerParams(
            dimension_semantics=("arbitrary",), vmem_limit_bytes=VMEM_LIMIT),
        name="ffn_out_ln2",
    )(act, g_h1, w_out_bf, g2, b2)


def _attn_tables(seq_lens):
    rec = []
    base = 0
    for s, t_len in enumerate(seq_lens):
        rows = t_len // GRID_W
        for qb in range(rows // QB_ROWS):
            w0 = min(max(qb * QB_ROWS - WIN_H // 2, 0), rows - 2 * QB_ROWS)
            variant = 0 if qb == 0 else (2 if qb == rows // QB_ROWS - 1 else 1)
            rec.append((variant, (base + qb * QB_TOK) // QB_TOK, (base + w0 * GRID_W) // KV_BLK, s))
        base += t_len
    rec.sort(key=lambda x: x[0])
    arr = np.asarray(rec, np.int32)
    return tuple(jnp.asarray(arr[:, c]) for c in (1, 2, 0, 3))


def _rwkv_tables(seq_lens, reverse):
    gblk, mblk, flags = [], [], []
    base = 0
    for s, t_len in enumerate(seq_lens):
        n = t_len // CHUNK
        first_blk = base // CHUNK
        steps = [(first_blk, 2)] + [(first_blk + c, 0) for c in range(n)]
        if reverse:
            steps = steps[::-1]
        for idx, (blk, fl) in enumerate(steps):
            gblk.append(blk)
            mblk.append(s)
            flags.append(fl | (1 if idx == 0 else 0))
        base += t_len
    gblk.append(gblk[-1])
    mblk.append(mblk[-1])
    flags.append(0)
    return tuple(jnp.asarray(np.asarray(a, np.int32)) for a in (gblk, mblk, flags))


def _ffn_tables(seq_lens):
    pg, pm, nx, fl = [], [], [], []
    base = 0
    total = sum(seq_lens)
    for s, t_len in enumerate(seq_lens):
        for t in range(t_len // TM_FFN):
            r0 = base + t * TM_FFN
            first = t == 0
            last = t == t_len // TM_FFN - 1
            pg.append(max(r0 // HALO - 1, 0))
            pm.append((s * CHUNK + CHUNK - HALO) // HALO)
            nx.append(min((r0 + TM_FFN) // HALO, total // HALO - 1))
            fl.append((1 if first else 0) | (2 if last else 0))
        base += t_len
    return tuple(jnp.asarray(np.asarray(a, np.int32)) for a in (pg, pm, nx, fl))


def _pad_cols(w, n):
    return jnp.pad(w, ((0, 0), (0, n - w.shape[1])))


def _encode_all(xs, meta_tokens, emb_ln_g, emb_ln_b, w_in, attn_rpb, rwkv_mu, rwkv_w0, rwkv_w2,
                rwkv_a0, rwkv_a2, rwkv_g2, rwkv_k_k, rwkv_k_a, rwkv_r_k, rwkv_lnx_g, rwkv_lnx_b,
                w_out, ln1_g, ln1_b, ffn_w_in, ffn_conv_w, ffn_conv_b, ffn_w_out, ln2_g, ln2_b):
    seq_lens = []
    for x in xs:
        seq_lens += [x.shape[1]] * x.shape[0]
    nseq = len(seq_lens)
    x2d = [x.reshape(-1, D_MODEL) for x in xs]
    meta_blk = jnp.concatenate([jnp.zeros((PAD_ROWS, D_MODEL), F32), meta_tokens.astype(F32)], axis=0)
    x_m = jnp.tile(meta_blk, (nseq, 1))

    o1 = N_QKV
    o2 = o1 + 3 * D_RWKV
    o3 = o2 + R_DECAY
    o4 = o3 + R_ICLR
    wl = w_in[0]
    w_qkv = wl[:, :o1].astype(BF16)
    w_rw = jnp.concatenate(
        [wl[:, o1:o2], _pad_cols(wl[:, o2:o3], R_LOW), _pad_cols(wl[:, o3:o4], R_LOW), wl[:, o4:]],
        axis=1).astype(BF16)
    eg = emb_ln_g.reshape(1, D_MODEL)
    eb = emb_ln_b.reshape(1, D_MODEL)
    mu = rwkv_mu[0]
    o5 = 3 * D_RWKV
    mu_p = jnp.concatenate(
        [mu[:, :o5], _pad_cols(mu[:, o5:o5 + R_DECAY], R_LOW), _pad_cols(mu[:, o5 + R_DECAY:], R_LOW)],
        axis=1)
    pad_rows = lambda w: jnp.pad(w, ((0, R_LOW - w.shape[0]), (0, 0))).astype(BF16)
    row = lambda v: v.reshape(1, -1)

    def dir_params(d):
        return dict(mu=mu_p[d:d + 1], w0=row(rwkv_w0[0, d]), w2=pad_rows(rwkv_w2[0, d]),
                    a0=row(rwkv_a0[0, d]), a2=pad_rows(rwkv_a2[0, d]),
                    k_k=row(rwkv_k_k[0]), k_a=row(rwkv_k_a[0]), r_k=row(rwkv_r_k[0]),
                    g2=rwkv_g2[0].astype(BF16), lnx_g=row(rwkv_lnx_g[0]), lnx_b=row(rwkv_lnx_b[0]))

    w_out_bf = w_out[0].astype(BF16)
    ffn_w_in_bf = ffn_w_in[0].astype(BF16)
    ffn_w_out_bf = ffn_w_out[0].astype(BF16)
    bias = _attn_bias_tiles(attn_rpb[0])

    g_qkv, g_h0 = _inproj(x2d, eg, eb, w_qkv, BF16, True, TM)
    g_prw = _inproj(x2d, eg, eb, w_rw, F32, False, TM)
    m_qkv, m_h0 = _inproj([x_m], eg, eb, w_qkv, BF16, True, x_m.shape[0])
    m_prw = _inproj([x_m], eg, eb, w_rw, F32, False, x_m.shape[0])

    g_oattn = _attention(g_qkv, m_qkv, bias, _attn_tables(seq_lens))
    m_oattn = _meta_attention(m_qkv)
    g_yb, m_yb = _rwkv_scan(g_prw, m_prw, dir_params(0), _rwkv_tables(seq_lens, False), False)
    g_orwkv, m_orwkv = _rwkv_scan(g_prw, m_prw, dir_params(1), _rwkv_tables(seq_lens, True), True,
                                  prev=(g_yb, m_yb))

    g1 = ln1_g.reshape(1, D_MODEL)
    b1 = ln1_b.reshape(1, D_MODEL)
    g_h1, g_h1b = _outproj(g_oattn, g_orwkv, g_h0, w_out_bf, g1, b1, TM)
    _, m_h1b = _outproj(m_oattn, m_orwkv, m_h0, w_out_bf, g1, b1, x_m.shape[0])

    out_rows = [x.shape[0] * x.shape[1] for x in xs]
    act = _ffn_in(g_h1b, m_h1b, ffn_w_in_bf, ffn_conv_w[0], ffn_conv_b.reshape(1, D_FF),
                  _ffn_tables(seq_lens))
    ys, row0 = [], 0
    for n_rows in out_rows:
        ys.append(_ffn_out(act, g_h1, ffn_w_out_bf, ln2_g.reshape(1, D_MODEL),
                           ln2_b.reshape(1, D_MODEL), row0, n_rows))
        row0 += n_rows
    return [y.reshape(x.shape) for y, x in zip(ys, xs)]


def kernel(x_prompt, x_sample, meta_tokens, emb_ln_g, emb_ln_b, w_in, attn_rpb, rwkv_mu, rwkv_w0, rwkv_w2, rwkv_a0, rwkv_a2, rwkv_g2, rwkv_k_k, rwkv_k_a, rwkv_r_k, rwkv_lnx_g, rwkv_lnx_b, w_out, ln1_g, ln1_b, ffn_w_in, ffn_conv_w, ffn_conv_b, ffn_w_out, ln2_g, ln2_b):
    y_prompt, y_sample = _encode_all(
        [x_prompt, x_sample], meta_tokens, emb_ln_g, emb_ln_b, w_in, attn_rpb, rwkv_mu, rwkv_w0,
        rwkv_w2, rwkv_a0, rwkv_a2, rwkv_g2, rwkv_k_k, rwkv_k_a, rwkv_r_k, rwkv_lnx_g, rwkv_lnx_b,
        w_out, ln1_g, ln1_b, ffn_w_in, ffn_conv_w, ffn_conv_b, ffn_w_out, ln2_g, ln2_b)
    return (y_prompt, y_sample)
```

```python
import functools

import numpy as np
import jax
import jax.numpy as jnp
from jax import lax
from jax.experimental import pallas as pl
from jax.experimental.pallas import tpu as pltpu

D_MODEL = 2048
N_META = 16
GRID_W = 64
WIN_H = 8
WIN_W = 16
D_ATTN = 1024
HD_ATTN = 128
H_ATTN = D_ATTN // HD_ATTN
D_RWKV = 1024
HD_RWKV = 64
R_DECAY = 96
R_ICLR = 96
R_GATE = 256
D_FF = 5632
DEEPNORM_ALPHA = 2.0 ** 0.25
LN_EPS = 1e-5
GN_EPS = 64e-5
NEG_INF = -1e30

LANE = 128
CHUNK = 64
PAD_ROWS = CHUNK - N_META
R_LOW = 128
N_SHIFT_P = 3 * D_RWKV + 2 * R_LOW
N_RW = N_SHIFT_P + R_GATE
N_QKV = 3 * D_ATTN
N_PAIR = D_RWKV // LANE
QB_ROWS = 8
QB_TOK = QB_ROWS * GRID_W
KV_BLK = 256
WIN_TOK = 4 * KV_BLK
TM = 512
TM_FFN = 1024
TM_SLAB = 512
TM_OUT = 256
SLAB_OUT = 128
N_SLAB = 4
FC = 512
FC_SUB = 256
HALO = 16
VMEM_LIMIT = 60 * 1024 * 1024

F32 = jnp.float32
BF16 = jnp.bfloat16


def _layer_norm(x, g, b):
    mu = jnp.mean(x, axis=-1, keepdims=True)
    xc = x - mu
    var = jnp.mean(xc * xc, axis=-1, keepdims=True)
    return xc * lax.rsqrt(var + LN_EPS) * g + b


def _dot(a, b):
    return jnp.dot(a, b, preferred_element_type=F32)


def _dot_nt(a, b):
    return lax.dot_general(a, b, (((1,), (1,)), ((), ())), preferred_element_type=F32)


def _bmm(a, b):
    return jnp.einsum("bik,bkj->bij", a, b, preferred_element_type=F32)


def _bmm_nt(a, b):
    return jnp.einsum("bik,bjk->bij", a, b, preferred_element_type=F32)


def _bmm_tn(a, b):
    return jnp.einsum("bki,bkj->bij", a, b, preferred_element_type=F32)


def _select_rows(x_refs, tiles_first, rows):
    if len(x_refs) == 1:
        return x_refs[0][rows, :]
    return jnp.where(pl.program_id(0) < tiles_first, x_refs[0][rows, :], x_refs[1][rows, :])


def _emit_skewed(n_slabs, stages):
    depth = max(lag for lag, _ in stages)
    for k in range(n_slabs + depth):
        for lag, fn in stages:
            if 0 <= k - lag < n_slabs:
                fn(k - lag)


def _row_specs(xs, tm, width):
    if len(xs) == 1:
        return [pl.BlockSpec((tm, width), lambda i: (i, 0))], 0
    tiles_first = xs[0].shape[0] // tm
    return [pl.BlockSpec((tm, width), lambda i: (jnp.minimum(i, tiles_first - 1), 0)),
            pl.BlockSpec((tm, width), lambda i: (jnp.maximum(i - tiles_first, 0), 0))], tiles_first


def _inproj_kernel(*refs, n_x, tiles_first, q_scale):
    x_refs = refs[:n_x]
    g_ref, b_ref, w_ref, o_ref = refs[n_x:n_x + 4]
    h0_ref = refs[n_x + 4] if q_scale else None
    slab = o_ref.shape[0] // N_SLAB
    rows = lambda s: slice(s * slab, (s + 1) * slab)
    h = {}

    def norm(s):
        h0 = _layer_norm(_select_rows(x_refs, tiles_first, rows(s)), g_ref[...], b_ref[...])
        if h0_ref is not None:
            h0_ref[rows(s), :] = h0
        h[s] = h0.astype(BF16)

    def project(s):
        acc = _dot(h.pop(s), w_ref[...])
        if q_scale:
            o_ref[rows(s), :D_ATTN] = (acc[:, :D_ATTN] * (HD_ATTN ** -0.5)).astype(o_ref.dtype)
            o_ref[rows(s), D_ATTN:] = acc[:, D_ATTN:].astype(o_ref.dtype)
        else:
            o_ref[rows(s), :] = acc.astype(o_ref.dtype)

    _emit_skewed(N_SLAB, [(1, project), (0, norm)])


def _inproj(xs, ln_g, ln_b, w, out_dtype, q_scale, tm):
    rows = sum(x.shape[0] for x in xs)
    n_out = w.shape[1]
    x_specs, tiles_first = _row_specs(xs, tm, D_MODEL)
    cmap = lambda i: (0, 0)
    out_specs = pl.BlockSpec((tm, n_out), lambda i: (i, 0))
    out_shape = jax.ShapeDtypeStruct((rows, n_out), out_dtype)
    if q_scale:
        out_specs = [out_specs, pl.BlockSpec((tm, D_MODEL), lambda i: (i, 0))]
        out_shape = [out_shape, jax.ShapeDtypeStruct((rows, D_MODEL), F32)]
    return pl.pallas_call(
        functools.partial(_inproj_kernel, n_x=len(xs), tiles_first=tiles_first, q_scale=q_scale),
        grid=(rows // tm,),
        in_specs=x_specs + [
            pl.BlockSpec((1, D_MODEL), cmap),
            pl.BlockSpec((1, D_MODEL), cmap),
            pl.BlockSpec((D_MODEL, n_out), cmap, pipeline_mode=pl.Buffered(1)),
        ],
        out_specs=out_specs,
        out_shape=out_shape,
        compiler_params=pltpu.CompilerParams(
            dimension_semantics=("arbitrary",), vmem_limit_bytes=VMEM_LIMIT),
        name="inproj_qkv" if q_scale else "inproj_rwkv",
    )(*xs, ln_g, ln_b, w)


def _window_rows(variant, i):
    half = WIN_H // 2
    if variant == 0:
        off, r0 = 0, max(i - half, 0)
    elif variant == 1:
        off, r0 = -half, i - half
    else:
        off, r0 = -WIN_H, min(i - half, 0)
    return [j for j in range(2 * QB_ROWS) if r0 <= j + off < r0 + WIN_H], off


def _attn_tile_plan(variant):
    plan = []
    for i in range(QB_ROWS):
        rows, off = _window_rows(variant, i)
        tiles = []
        for jt in range(QB_ROWS):
            first, second = 2 * jt in rows, 2 * jt + 1 in rows
            if not (first or second):
                continue
            kind = 0 if (first and second) else (1 if second else 2)
            e = 2 * jt + off - i + WIN_H
            assert (not first or 0 <= e - 1 <= 2 * WIN_H - 2) and (not second or 0 <= e <= 2 * WIN_H - 2)
            tiles.append((jt, kind, e))
        plan.append(tiles)
    return plan


def _attn_kernel(qblk_s, kvblk_s, var_s, mseq_s, q_ref, k_ref, v_ref,
                 mk_ref, mv_ref, bt_ref, o_ref, p_scr):
    del qblk_s, kvblk_s, mseq_s
    variant = var_s[pl.program_id(1)]
    q = q_ref[...]
    s_win = _dot_nt(q, k_ref[...])
    km = mk_ref[PAD_ROWS:, :]
    vm = mv_ref[PAD_ROWS:, :]
    sm = _dot_nt(q, km)

    def softmax_pv(plan):
        def score_tile(i, jt, kind, e):
            rows = slice(i * GRID_W, (i + 1) * GRID_W)
            return s_win[rows, jt * LANE:(jt + 1) * LANE] + bt_ref[0, kind, e]

        tile_max = []
        for i, tiles in enumerate(plan):
            t = None
            for jt, kind, e in tiles:
                x = score_tile(i, jt, kind, e)
                t = x if t is None else jnp.maximum(t, x)
            tile_max.append(t)
        m = jnp.maximum(jnp.max(jnp.concatenate(tile_max, axis=0), axis=-1, keepdims=True),
                        jnp.max(sm, axis=-1, keepdims=True))
        pm = jnp.exp(sm - m)
        p_scr[...] = jnp.zeros_like(p_scr)
        tile_sum = []
        for i, tiles in enumerate(plan):
            rows = slice(i * GRID_W, (i + 1) * GRID_W)
            mi = m[rows]
            t = None
            for jt, kind, e in tiles:
                p = jnp.exp(score_tile(i, jt, kind, e) - mi)
                p_scr[rows, jt * LANE:(jt + 1) * LANE] = p.astype(BF16)
                t = p if t is None else t + p
            tile_sum.append(t)
        l = (jnp.sum(jnp.concatenate(tile_sum, axis=0), axis=-1, keepdims=True)
             + jnp.sum(pm, axis=-1, keepdims=True))
        o = _dot(pm.astype(BF16), vm)
        o = o + _dot(p_scr[...], v_ref[...])
        o_ref[...] = (o / l).astype(BF16)

    for v in range(3):
        pl.when(variant == v)(functools.partial(softmax_pv, _attn_tile_plan(v)))


def _attention(g_qkv, m_qkv, bias, tabs):
    qblk, kvblk, var, mseq = tabs
    nsteps = qblk.shape[0]
    rows = g_qkv.shape[0]

    def window_spec(col0):
        return pl.BlockSpec((pl.Element(WIN_TOK), pl.Element(HD_ATTN)),
                            lambda h, t, qb, kb, vr, ms: (kb[t] * KV_BLK, (col0 + h) * HD_ATTN))

    in_specs = [pl.BlockSpec((QB_TOK, HD_ATTN), lambda h, t, qb, kb, vr, ms: (qb[t], h)),
                window_spec(H_ATTN), window_spec(2 * H_ATTN)]
    in_specs += [
        pl.BlockSpec((CHUNK, HD_ATTN), lambda h, t, qb, kb, vr, ms: (ms[t], H_ATTN + h)),
        pl.BlockSpec((CHUNK, HD_ATTN), lambda h, t, qb, kb, vr, ms: (ms[t], 2 * H_ATTN + h)),
        pl.BlockSpec((1, 3, 2 * WIN_H, GRID_W, LANE), lambda h, t, qb, kb, vr, ms: (h, 0, 0, 0, 0)),
    ]
    grid_spec = pltpu.PrefetchScalarGridSpec(
        num_scalar_prefetch=4,
        grid=(H_ATTN, nsteps),
        in_specs=in_specs,
        out_specs=pl.BlockSpec((QB_TOK, HD_ATTN), lambda h, t, qb, kb, vr, ms: (qb[t], h)),
        scratch_shapes=[pltpu.VMEM((QB_TOK, WIN_TOK), BF16)],
    )
    return pl.pallas_call(
        _attn_kernel,
        grid_spec=grid_spec,
        out_shape=jax.ShapeDtypeStruct((rows, D_ATTN), BF16),
        compiler_params=pltpu.CompilerParams(
            dimension_semantics=("arbitrary", "arbitrary"), vmem_limit_bytes=VMEM_LIMIT),
        name="nbr_attention",
    )(qblk, kvblk, var, mseq, g_qkv, g_qkv, g_qkv, m_qkv, m_qkv, bias)


def _meta_attn_kernel(qkv_ref, o_ref):
    for h in range(H_ATTN):
        cols = lambda part: slice((part * H_ATTN + h) * HD_ATTN, (part * H_ATTN + h + 1) * HD_ATTN)
        q = qkv_ref[:, cols(0)]
        km = qkv_ref[PAD_ROWS:, cols(1)]
        vm = qkv_ref[PAD_ROWS:, cols(2)]
        s = _dot_nt(q, km)
        m = jnp.max(s, axis=-1, keepdims=True)
        p = jnp.exp(s - m)
        l = jnp.sum(p, axis=-1, keepdims=True)
        o_ref[:, h * HD_ATTN:(h + 1) * HD_ATTN] = (_dot(p.astype(BF16), vm) / l).astype(BF16)


def _meta_attention(m_qkv):
    nseq = m_qkv.shape[0] // CHUNK
    return pl.pallas_call(
        _meta_attn_kernel,
        grid=(nseq,),
        in_specs=[pl.BlockSpec((CHUNK, N_QKV), lambda s: (s, 0))],
        out_specs=pl.BlockSpec((CHUNK, D_ATTN), lambda s: (s, 0)),
        out_shape=jax.ShapeDtypeStruct((m_qkv.shape[0], D_ATTN), BF16),
        compiler_params=pltpu.CompilerParams(dimension_semantics=("arbitrary",)),
        name="meta_attention",
    )(m_qkv)


def _attn_bias_tiles(rpb):
    qc = np.arange(GRID_W)
    kc = np.arange(GRID_W)
    c0 = np.clip(qc - WIN_W // 2, 0, GRID_W - WIN_W)
    colmask = (kc[None, :] >= c0[:, None]) & (kc[None, :] < c0[:, None] + WIN_W)
    dc = np.clip(kc[None, :] - qc[:, None], -(WIN_W - 1), WIN_W - 1) + (WIN_W - 1)
    n_dr = 2 * WIN_H - 1
    n_dc = 2 * WIN_W - 1
    dc_onehot = (dc[None] == np.arange(n_dc)[:, None, None]).astype(np.float32)
    toeplitz = jnp.einsum("hrd,dqk->hrqk", rpb, jnp.asarray(dc_onehot),
                          precision=lax.Precision.HIGHEST)
    toeplitz = jnp.where(jnp.asarray(colmask)[None, None], toeplitz, NEG_INF)
    masked = jnp.full((H_ATTN, 1, GRID_W, GRID_W), NEG_INF, F32)
    padded = jnp.concatenate([masked, toeplitz, masked], axis=1)
    first = padded[:, 0:n_dr + 1]
    second = padded[:, 1:n_dr + 2]
    off = jnp.full_like(first, NEG_INF)
    kinds = [jnp.concatenate([a, b], axis=-1) for a, b in ((first, second), (off, second), (first, off))]
    return jnp.stack(kinds, axis=1)


def _split_bf16(x, n):
    parts = []
    rem = x
    for _ in range(n):
        p = rem.astype(BF16)
        parts.append(p)
        rem = rem - p.astype(F32)
    return parts


def _seg_sum(x, e2):
    stacked = jnp.concatenate([x[:, j * LANE:(j + 1) * LANE] for j in range(N_PAIR)], axis=0)
    hi, lo = _split_bf16(stacked, 2)
    res = _dot(hi, e2) + _dot(lo, e2)
    return jnp.concatenate([res[j * CHUNK:(j + 1) * CHUNK] for j in range(N_PAIR)], axis=1)


_OP_A, _OP_R, _OP_BH, _OP_KH, _OP_V, _OP_B, _OP_K = range(7)
N_OPND = 7


def _rwkv_kernel(gblk_s, mblk_s, flag_s, *refs, reverse):
    del gblk_s, mblk_s
    if reverse:
        (gp_ref, mp_ref, mu_ref, w0_ref, w2_ref, a0_ref, a2_ref, kk_ref, ka_ref, rk_ref,
         gyb_ref, myb_ref, g2_ref, lg_ref, lb_ref, og_ref, om_ref,
         state, carry, opnd, aux, cend, opnd_n, aux_n, cend_n) = refs
    else:
        (gp_ref, mp_ref, mu_ref, w0_ref, w2_ref, a0_ref, a2_ref, kk_ref, ka_ref, rk_ref,
         og_ref, om_ref, state, carry, opnd, aux, cend, opnd_n, aux_n, cend_n) = refs
    step = pl.program_id(0)
    fl = flag_s[step]
    fl_prev = flag_s[jnp.maximum(step - 1, 0)]
    is_first = (fl & 1) == 1
    is_meta = (fl & 2) == 2
    prev_first = (fl_prev & 1) == 1
    prev_meta = (fl_prev & 2) == 2

    @pl.when(step == 0)
    def _():
        opnd[...] = jnp.zeros_like(opnd)
        aux[...] = jnp.zeros_like(aux)
        cend[...] = jnp.zeros_like(cend)

    @pl.when(is_first)
    def _():
        carry[...] = jnp.zeros_like(carry)

    @pl.when(prev_first)
    def _():
        state[...] = jnp.zeros_like(state)

    ii = lax.broadcasted_iota(jnp.int32, (LANE, LANE), 0)
    jj = lax.broadcasted_iota(jnp.int32, (LANE, LANE), 1)
    same = (ii // CHUNK) == (jj // CHUNK)
    e2 = jnp.where(same, 1.0, 0.0).astype(BF16)

    ti2 = ii % CHUNK
    tj2 = jj % CHUNK
    if reverse:
        strict = jnp.logical_and(same, tj2 > ti2)
        incl = jnp.logical_and(same, tj2 >= ti2)
    else:
        strict = jnp.logical_and(same, tj2 < ti2)
        incl = jnp.logical_and(same, tj2 <= ti2)
    eye = jnp.where(ii == jj, 1.0, 0.0).astype(F32)
    lane64 = lax.broadcasted_iota(jnp.int32, (1, CHUNK, LANE), 2) < HD_RWKV

    def tiles(x):
        return jnp.stack([x[:, j * LANE:(j + 1) * LANE] for j in range(N_PAIR)], axis=0)

    def stack_masked(x):
        zero = jnp.zeros_like(x)
        return jnp.concatenate([jnp.where(lane64, x, zero), jnp.where(lane64, zero, x)], axis=1)

    def stack_dup(x):
        return jnp.concatenate([x, x], axis=1)

    def prepare():
        praw = jnp.where(is_meta, mp_ref[...], gp_ref[...])
        row = lax.broadcasted_iota(jnp.int32, (CHUNK, 1), 0)
        pad = jnp.logical_and(is_meta, row < PAD_ROWS)
        p_sh = jnp.where(pad, 0.0, praw[:, :N_SHIFT_P])
        gd = praw[:, N_SHIFT_P:]
        if reverse:
            rolled = pltpu.roll(p_sh, CHUNK - 1, axis=0)
            nb = jnp.where(row == CHUNK - 1, carry[0:1, :], rolled)
            carry[...] = p_sh[0:8, :]
        else:
            rolled = pltpu.roll(p_sh, 1, axis=0)
            nb = jnp.where(row == 0, carry[7:8, :], rolled)
            carry[...] = p_sh[CHUNK - 8:CHUNK, :]
        f = p_sh + (nb - p_sh) * mu_ref[...]
        r = f[:, 0:D_RWKV]
        k = f[:, D_RWKV:2 * D_RWKV]
        v = f[:, 2 * D_RWKV:3 * D_RWKV]
        wd = f[:, 3 * D_RWKV:3 * D_RWKV + R_LOW]
        ad = f[:, 3 * D_RWKV + R_LOW:3 * D_RWKV + 2 * R_LOW]
        opnd_n[_OP_V] = v.astype(BF16)
        yield
        wl = w0_ref[...] + _dot(jnp.tanh(wd).astype(BF16), w2_ref[...])
        nwl = -wl
        softplus = jnp.maximum(nwl, 0.0) + jnp.log(1.0 + jnp.exp(-jnp.abs(nwl)))
        lw = -jnp.exp(-softplus - 0.5)
        yield
        a = 1.0 / (1.0 + jnp.exp(-(a0_ref[...] + _dot(ad.astype(BF16), a2_ref[...]))))
        kmod = k * (1.0 + (a - 1.0) * ka_ref[...])
        yield
        kkr = k * kk_ref[...]
        kk = kkr / jnp.maximum(jnp.sqrt(_seg_sum(kkr * kkr, e2)), 1e-12)
        yield
        aux_n[0] = _seg_sum(r * kmod * rk_ref[...], e2) * v
        if reverse:
            aux_n[1] = _dot((1.0 / (1.0 + jnp.exp(-gd))).astype(BF16), g2_ref[...])
        yield
        ti = lax.broadcasted_iota(jnp.int32, (CHUNK, CHUNK), 0)
        tj = lax.broadcasted_iota(jnp.int32, (CHUNK, CHUNK), 1)
        tri = jnp.where((tj >= ti) if reverse else (tj <= ti), 1.0, 0.0).astype(BF16)
        cum = sum(_dot(tri, part) for part in _split_bf16(lw, 3))
        cum_end = cum[0:1, :] if reverse else cum[CHUNK - 1:CHUNK, :]
        cend_n[0:1, :] = cum_end
        yield
        opnd_n[_OP_R] = (r * jnp.exp(cum)).astype(BF16)
        yield
        e_neg = jnp.exp(-cum)
        bb = kk * a
        opnd_n[_OP_K] = (kmod * e_neg).astype(BF16)
        opnd_n[_OP_B] = (bb * e_neg).astype(BF16)
        yield
        opnd_n[_OP_A] = (-kk * jnp.exp(cum - lw)).astype(BF16)
        yield
        e_end = jnp.exp(cum_end - cum)
        opnd_n[_OP_KH] = (kmod * e_end).astype(BF16)
        opnd_n[_OP_BH] = (bb * e_end).astype(BF16)

    prep_steps = prepare()

    def emit_prep():
        next(prep_steps, None)

    v3 = tiles(opnd[_OP_V])
    a_s = stack_masked(tiles(opnd[_OP_A]))
    r_s = stack_masked(tiles(opnd[_OP_R]))
    bh_s = stack_masked(tiles(opnd[_OP_BH]))
    kh_s = stack_masked(tiles(opnd[_OP_KH]))
    v_s = stack_masked(v3)
    b_d = stack_dup(tiles(opnd[_OP_B]))
    k_d = stack_dup(tiles(opnd[_OP_K]))
    v_d = stack_dup(v3)
    cum_end_prev = cend[0:1, :]
    emit_prep()
    m1 = _bmm_nt(jnp.concatenate([a_s, r_s], axis=1),
                 jnp.concatenate([b_d, k_d], axis=1))
    emit_prep()
    n_ab = jnp.where(strict, m1[:, :LANE, :LANE], 0.0)
    a_ak = jnp.where(strict, m1[:, :LANE, LANE:], 0.0)
    a_rb = jnp.where(incl, m1[:, LANE:, :LANE], 0.0)
    a_rk = jnp.where(incl, m1[:, LANE:, LANE:], 0.0)
    eye_b = eye.astype(BF16)
    tinv = (eye + n_ab).astype(BF16)
    npow = n_ab.astype(BF16)
    for _ in range(5):
        npow = _bmm(npow, npow).astype(BF16)
        emit_prep()
        tinv = _bmm(tinv, eye_b + npow).astype(BF16)
    gy = _bmm(jnp.concatenate([a_ak, a_rk], axis=1).astype(BF16), v_d)
    emit_prep()
    g_s = jnp.where(same, gy[:, :LANE], 0.0)
    y0_s = jnp.where(same, gy[:, LANE:], 0.0)
    tz = _bmm(tinv, jnp.concatenate([a_s, g_s.astype(BF16)], axis=2))
    emit_prep()
    w_s = tz[:, :, :LANE]
    u0_s = tz[:, :, LANE:]
    s_add = _bmm_tn(kh_s, v_s)
    s_old = state[...]
    xr = _bmm(jnp.concatenate([w_s.astype(BF16), r_s], axis=1), s_old.astype(BF16))
    emit_prep()
    u_s = xr[:, :LANE] + u0_s
    y_s = xr[:, LANE:] + _bmm(a_rb.astype(BF16), u_s.astype(BF16)) + y0_s
    emit_prep()
    y3 = y_s[:, :CHUNK] + y_s[:, CHUNK:]
    g_end = jnp.exp(jnp.swapaxes(jnp.broadcast_to(tiles(cum_end_prev), (N_PAIR, LANE, LANE)), 1, 2))
    state[...] = g_end * s_old + _bmm_tn(bh_s, u_s.astype(BF16)) + s_add
    y = jnp.concatenate([y3[j] for j in range(N_PAIR)], axis=1)

    if not reverse:
        res = jnp.concatenate([y, aux[0]], axis=1)
    else:
        yb = jnp.where(prev_meta, myb_ref[...], gyb_ref[...])
        yy = y + yb[:, :D_RWKV]
        mean = _seg_sum(yy, e2) * (1.0 / HD_RWKV)
        yc = yy - mean
        var = _seg_sum(yc * yc, e2) * (1.0 / HD_RWKV)
        yn = yc * lax.rsqrt(var + GN_EPS) * lg_ref[...] + lb_ref[...]
        res = ((yn + yb[:, D_RWKV:] + aux[0]) * aux[1]).astype(BF16)

    for _ in prep_steps:
        pass
    opnd[...] = opnd_n[...]
    aux[...] = aux_n[...]
    cend[0:1, :] = cend_n[0:1, :]

    @pl.when(prev_meta)
    def _():
        om_ref[...] = res

    @pl.when(jnp.logical_not(prev_meta))
    def _():
        og_ref[...] = res


def _rwkv_scan(g_prw, m_prw, params, tabs, reverse, prev=None):
    gblk, mblk, flags = tabs
    nsteps = gblk.shape[0]
    gmap = lambda t, gb, mb, fl: (gb[t], 0)
    mmap = lambda t, gb, mb, fl: (mb[t], 0)
    gmap_prev = lambda t, gb, mb, fl: (gb[jnp.maximum(t - 1, 0)], 0)
    mmap_prev = lambda t, gb, mb, fl: (mb[jnp.maximum(t - 1, 0)], 0)
    cmap = lambda t, gb, mb, fl: (0, 0)
    vec = lambda n: pl.BlockSpec((1, n), cmap)
    in_specs = [
        pl.BlockSpec((CHUNK, N_RW), gmap),
        pl.BlockSpec((CHUNK, N_RW), mmap),
        vec(N_SHIFT_P), vec(D_RWKV),
        pl.BlockSpec((R_LOW, D_RWKV), cmap), vec(D_RWKV),
        pl.BlockSpec((R_LOW, D_RWKV), cmap), vec(D_RWKV), vec(D_RWKV), vec(D_RWKV),
    ]
    args = [g_prw, m_prw, params["mu"], params["w0"], params["w2"], params["a0"], params["a2"],
            params["k_k"], params["k_a"], params["r_k"]]
    if reverse:
        g_yb, m_yb = prev
        in_specs += [
            pl.BlockSpec((CHUNK, 2 * D_RWKV), gmap_prev),
            pl.BlockSpec((CHUNK, 2 * D_RWKV), mmap_prev),
            pl.BlockSpec((R_GATE, D_RWKV), cmap), vec(D_RWKV), vec(D_RWKV),
        ]
        args += [g_yb, m_yb, params["g2"], params["lnx_g"], params["lnx_b"]]
        width, dt, n_aux = D_RWKV, BF16, 2
    else:
        width, dt, n_aux = 2 * D_RWKV, F32, 1
    grid_spec = pltpu.PrefetchScalarGridSpec(
        num_scalar_prefetch=3,
        grid=(nsteps,),
        in_specs=in_specs,
        out_specs=[pl.BlockSpec((CHUNK, width), gmap_prev),
                   pl.BlockSpec((CHUNK, width), mmap_prev)],
        scratch_shapes=[
            pltpu.VMEM((N_PAIR, LANE, LANE), F32),
            pltpu.VMEM((8, N_SHIFT_P), F32),
            pltpu.VMEM((N_OPND, CHUNK, D_RWKV), BF16),
            pltpu.VMEM((n_aux, CHUNK, D_RWKV), F32),
            pltpu.VMEM((8, D_RWKV), F32),
            pltpu.VMEM((N_OPND, CHUNK, D_RWKV), BF16),
            pltpu.VMEM((n_aux, CHUNK, D_RWKV), F32),
            pltpu.VMEM((8, D_RWKV), F32),
        ],
    )
    return pl.pallas_call(
        functools.partial(_rwkv_kernel, reverse=reverse),
        grid_spec=grid_spec,
        out_shape=[jax.ShapeDtypeStruct((g_prw.shape[0], width), dt),
                   jax.ShapeDtypeStruct((m_prw.shape[0], width), dt)],
        compiler_params=pltpu.CompilerParams(
            dimension_semantics=("arbitrary",), vmem_limit_bytes=VMEM_LIMIT),
        name="rwkv_bwd" if reverse else "rwkv_fwd",
    )(gblk, mblk, flags, *args)


def _outproj_kernel(h0_ref, oa_ref, or_ref, wa_ref, wr_ref, g1_ref, b1_ref, h1_ref, h1b_ref):
    mix = _dot(oa_ref[...], wa_ref[...]) + _dot(or_ref[...], wr_ref[...])
    h1 = _layer_norm(DEEPNORM_ALPHA * h0_ref[...] + mix, g1_ref[...], b1_ref[...])
    h1_ref[...] = h1
    h1b_ref[...] = h1.astype(BF16)


def _outproj(o_attn, o_rwkv, h0, w_out_bf, g1, b1, tm):
    rows = o_attn.shape[0]
    cmap = lambda i: (0, 0)
    vec = pl.BlockSpec((1, D_MODEL), cmap)
    return pl.pallas_call(
        _outproj_kernel,
        grid=(rows // tm,),
        in_specs=[
            pl.BlockSpec((tm, D_MODEL), lambda i: (i, 0)),
            pl.BlockSpec((tm, D_ATTN), lambda i: (i, 0)),
            pl.BlockSpec((tm, D_RWKV), lambda i: (i, 0)),
            pl.BlockSpec((D_ATTN, D_MODEL), lambda i: (0, 0), pipeline_mode=pl.Buffered(1)),
            pl.BlockSpec((D_RWKV, D_MODEL), lambda i: (1, 0), pipeline_mode=pl.Buffered(1)),
            vec, vec,
        ],
        out_specs=[pl.BlockSpec((tm, D_MODEL), lambda i: (i, 0)),
                   pl.BlockSpec((tm, D_MODEL), lambda i: (i, 0))],
        out_shape=[jax.ShapeDtypeStruct((rows, D_MODEL), F32),
                   jax.ShapeDtypeStruct((rows, D_MODEL), BF16)],
        compiler_params=pltpu.CompilerParams(
            dimension_semantics=("arbitrary",), vmem_limit_bytes=VMEM_LIMIT),
        name="outproj_ln1",
    )(h0, o_attn, o_rwkv, w_out_bf, w_out_bf, g1, b1)


def _ffn_in_kernel(pg_s, pm_s, nx_s, fl_s, h_ref, pg_ref, pm_ref, nx_ref, wg_ref, wu_ref,
                   cw_ref, cb_ref, act_ref, lhs, gate_s):
    del pg_s, pm_s, nx_s
    i = pl.program_id(0)
    c = pl.program_id(1)
    fl = fl_s[i]

    @pl.when(c == 0)
    def _():
        lhs[0:HALO, :] = jnp.where((fl & 1) == 1, pm_ref[...], pg_ref[...])
        lhs[HALO:HALO + TM_FFN, :] = h_ref[...]
        lhs[HALO + TM_FFN:, :] = jnp.where((fl & 2) == 2, jnp.zeros_like(nx_ref), nx_ref[...])

    units = [(s, k) for s in range(TM_FFN // TM_SLAB) for k in range(FC // FC_SUB)]

    def matmuls(u):
        s, k = units[u]
        cols = slice(k * FC_SUB, (k + 1) * FC_SUB)
        r0 = s * TM_SLAB
        gate_s[u] = _dot(lhs[r0:r0 + TM_SLAB + 2 * HALO, :], wg_ref[:, cols])

    def activation(u):
        s, k = units[u]
        cols = slice(k * FC_SUB, (k + 1) * FC_SUB)
        r0 = s * TM_SLAB
        up = _dot(lhs[r0 + HALO:r0 + HALO + TM_SLAB, :], wu_ref[:, cols])
        g_ext = gate_s[u]
        n_ext = TM_SLAB + 2 * HALO
        centre = slice(HALO, HALO + TM_SLAB)
        gate = (pltpu.roll(g_ext, 1, axis=0)[centre] * cw_ref[0:1, cols]
                + g_ext[centre] * cw_ref[1:2, cols]
                + pltpu.roll(g_ext, n_ext - 1, axis=0)[centre] * cw_ref[2:3, cols] + cb_ref[:, cols])
        act = 0.5 * gate * (1.0 + lax.erf(gate * 0.7071067811865476)) * up
        act_ref[s * TM_SLAB:(s + 1) * TM_SLAB, cols] = act.astype(BF16)

    matmuls(0)
    for u in range(len(units)):
        if u + 1 < len(units):
            matmuls(u + 1)
        activation(u)


def _ffn_in(g_h1b, m_h1b, w_in_bf, conv_w, conv_b, tabs):
    pg, pm, nx, fl = tabs
    rows = g_h1b.shape[0]
    nfc = D_FF // FC
    n_units = (TM_FFN // TM_SLAB) * (FC // FC_SUB)
    grid_spec = pltpu.PrefetchScalarGridSpec(
        num_scalar_prefetch=4,
        grid=(rows // TM_FFN, nfc),
        in_specs=[
            pl.BlockSpec((TM_FFN, D_MODEL), lambda i, c, *_: (i, 0)),
            pl.BlockSpec((HALO, D_MODEL), lambda i, c, pg, pm, nx, fl: (pg[i], 0)),
            pl.BlockSpec((HALO, D_MODEL), lambda i, c, pg, pm, nx, fl: (pm[i], 0)),
            pl.BlockSpec((HALO, D_MODEL), lambda i, c, pg, pm, nx, fl: (nx[i], 0)),
            pl.BlockSpec((D_MODEL, FC), lambda i, c, *_: (0, c)),
            pl.BlockSpec((D_MODEL, FC), lambda i, c, *_: (0, nfc + c)),
            pl.BlockSpec((3, FC), lambda i, c, *_: (0, c)),
            pl.BlockSpec((1, FC), lambda i, c, *_: (0, c)),
        ],
        out_specs=pl.BlockSpec((TM_FFN, FC), lambda i, c, *_: (i, c)),
        scratch_shapes=[
            pltpu.VMEM((TM_FFN + 2 * HALO, D_MODEL), BF16),
            pltpu.VMEM((n_units, TM_SLAB + 2 * HALO, FC_SUB), F32),
        ],
    )
    return pl.pallas_call(
        _ffn_in_kernel,
        grid_spec=grid_spec,
        out_shape=jax.ShapeDtypeStruct((rows, D_FF), BF16),
        compiler_params=pltpu.CompilerParams(
            dimension_semantics=("arbitrary", "arbitrary"), vmem_limit_bytes=VMEM_LIMIT),
        name="ffn_in",
    )(pg, pm, nx, fl, g_h1b, g_h1b, m_h1b, g_h1b, w_in_bf, w_in_bf, conv_w, conv_b)


def _ffn_out_kernel(act_ref, h_ref, wo_ref, g2_ref, b2_ref, y_ref):
    n_slab = TM_OUT // SLAB_OUT
    rows = lambda s: slice(s * SLAB_OUT, (s + 1) * SLAB_OUT)
    ffn = {}

    def project(s):
        ffn[s] = _dot(act_ref[rows(s), :], wo_ref[...])

    def norm(s):
        y_ref[rows(s), :] = _layer_norm(DEEPNORM_ALPHA * h_ref[rows(s), :] + ffn.pop(s),
                                        g2_ref[...], b2_ref[...])

    _emit_skewed(n_slab, [(0, project), (1, norm)])


def _ffn_out(act, g_h1, w_out_bf, g2, b2, row0, n_rows):
    tm = TM_OUT
    tile0 = row0 // tm
    cmap = lambda i: (0, 0)
    vec = pl.BlockSpec((1, D_MODEL), cmap)
    return pl.pallas_call(
        _ffn_out_kernel,
        grid=(n_rows // tm,),
        in_specs=[
            pl.BlockSpec((tm, D_FF), lambda i: (tile0 + i, 0)),
            pl.BlockSpec((tm, D_MODEL), lambda i: (tile0 + i, 0)),
            pl.BlockSpec((D_FF, D_MODEL), cmap, pipeline_mode=pl.Buffered(1)),
            vec, vec,
        ],
        out_specs=pl.BlockSpec((tm, D_MODEL), lambda i: (i, 0)),
        out_shape=jax.ShapeDtypeStruct((n_rows, D_MODEL), F32),
        compiler_params=pltpu.CompilerParams(
            dimension_semantics=("arbitrary",), vmem_limit_bytes=VMEM_LIMIT),
        name="ffn_out_ln2",
    )(act, g_h1, w_out_bf, g2, b2)


def _attn_tables(seq_lens):
    rec = []
    base = 0
    for s, t_len in enumerate(seq_lens):
        rows = t_len // GRID_W
        for qb in range(rows // QB_ROWS):
            w0 = min(max(qb * QB_ROWS - WIN_H // 2, 0), rows - 2 * QB_ROWS)
            variant = 0 if qb == 0 else (2 if qb == rows // QB_ROWS - 1 else 1)
            rec.append((variant, (base + qb * QB_TOK) // QB_TOK, (base + w0 * GRID_W) // KV_BLK, s))
        base += t_len
    rec.sort(key=lambda x: x[0])
    arr = np.asarray(rec, np.int32)
    return tuple(jnp.asarray(arr[:, c]) for c in (1, 2, 0, 3))


def _rwkv_tables(seq_lens, reverse):
    gblk, mblk, flags = [], [], []
    base = 0
    for s, t_len in enumerate(seq_lens):
        n = t_len // CHUNK
        first_blk = base // CHUNK
        steps = [(first_blk, 2)] + [(first_blk + c, 0) for c in range(n)]
        if reverse:
            steps = steps[::-1]
        for idx, (blk, fl) in enumerate(steps):
            gblk.append(blk)
            mblk.append(s)
            flags.append(fl | (1 if idx == 0 else 0))
        base += t_len
    gblk.append(gblk[-1])
    mblk.append(mblk[-1])
    flags.append(0)
    return tuple(jnp.asarray(np.asarray(a, np.int32)) for a in (gblk, mblk, flags))


def _ffn_tables(seq_lens):
    pg, pm, nx, fl = [], [], [], []
    base = 0
    total = sum(seq_lens)
    for s, t_len in enumerate(seq_lens):
        for t in range(t_len // TM_FFN):
            r0 = base + t * TM_FFN
            first = t == 0
            last = t == t_len // TM_FFN - 1
            pg.append(max(r0 // HALO - 1, 0))
            pm.append((s * CHUNK + CHUNK - HALO) // HALO)
            nx.append(min((r0 + TM_FFN) // HALO, total // HALO - 1))
            fl.append((1 if first else 0) | (2 if last else 0))
        base += t_len
    return tuple(jnp.asarray(np.asarray(a, np.int32)) for a in (pg, pm, nx, fl))


def _pad_cols(w, n):
    return jnp.pad(w, ((0, 0), (0, n - w.shape[1])))


def _encode_all(xs, meta_tokens, emb_ln_g, emb_ln_b, w_in, attn_rpb, rwkv_mu, rwkv_w0, rwkv_w2,
                rwkv_a0, rwkv_a2, rwkv_g2, rwkv_k_k, rwkv_k_a, rwkv_r_k, rwkv_lnx_g, rwkv_lnx_b,
                w_out, ln1_g, ln1_b, ffn_w_in, ffn_conv_w, ffn_conv_b, ffn_w_out, ln2_g, ln2_b):
    seq_lens = []
    for x in xs:
        seq_lens += [x.shape[1]] * x.shape[0]
    nseq = len(seq_lens)
    x2d = [x.reshape(-1, D_MODEL) for x in xs]
    meta_blk = jnp.concatenate([jnp.zeros((PAD_ROWS, D_MODEL), F32), meta_tokens.astype(F32)], axis=0)
    x_m = jnp.tile(meta_blk, (nseq, 1))

    o1 = N_QKV
    o2 = o1 + 3 * D_RWKV
    o3 = o2 + R_DECAY
    o4 = o3 + R_ICLR
    wl = w_in[0]
    w_qkv = wl[:, :o1].astype(BF16)
    w_rw = jnp.concatenate(
        [wl[:, o1:o2], _pad_cols(wl[:, o2:o3], R_LOW), _pad_cols(wl[:, o3:o4], R_LOW), wl[:, o4:]],
        axis=1).astype(BF16)
    eg = emb_ln_g.reshape(1, D_MODEL)
    eb = emb_ln_b.reshape(1, D_MODEL)
    mu = rwkv_mu[0]
    o5 = 3 * D_RWKV
    mu_p = jnp.concatenate(
        [mu[:, :o5], _pad_cols(mu[:, o5:o5 + R_DECAY], R_LOW), _pad_cols(mu[:, o5 + R_DECAY:], R_LOW)],
        axis=1)
    pad_rows = lambda w: jnp.pad(w, ((0, R_LOW - w.shape[0]), (0, 0))).astype(BF16)
    row = lambda v: v.reshape(1, -1)

    def dir_params(d):
        return dict(mu=mu_p[d:d + 1], w0=row(rwkv_w0[0, d]), w2=pad_rows(rwkv_w2[0, d]),
                    a0=row(rwkv_a0[0, d]), a2=pad_rows(rwkv_a2[0, d]),
                    k_k=row(rwkv_k_k[0]), k_a=row(rwkv_k_a[0]), r_k=row(rwkv_r_k[0]),
                    g2=rwkv_g2[0].astype(BF16), lnx_g=row(rwkv_lnx_g[0]), lnx_b=row(rwkv_lnx_b[0]))

    w_out_bf = w_out[0].astype(BF16)
    ffn_w_in_bf = ffn_w_in[0].astype(BF16)
    ffn_w_out_bf = ffn_w_out[0].astype(BF16)
    bias = _attn_bias_tiles(attn_rpb[0])

    g_qkv, g_h0 = _inproj(x2d, eg, eb, w_qkv, BF16, True, TM)
    g_prw = _inproj(x2d, eg, eb, w_rw, F32, False, TM)
    m_qkv, m_h0 = _inproj([x_m], eg, eb, w_qkv, BF16, True, x_m.shape[0])
    m_prw = _inproj([x_m], eg, eb, w_rw, F32, False, x_m.shape[0])

    g_oattn = _attention(g_qkv, m_qkv, bias, _attn_tables(seq_lens))
    m_oattn = _meta_attention(m_qkv)
    g_yb, m_yb = _rwkv_scan(g_prw, m_prw, dir_params(0), _rwkv_tables(seq_lens, False), False)
    g_orwkv, m_orwkv = _rwkv_scan(g_prw, m_prw, dir_params(1), _rwkv_tables(seq_lens, True), True,
                                  prev=(g_yb, m_yb))

    g1 = ln1_g.reshape(1, D_MODEL)
    b1 = ln1_b.reshape(1, D_MODEL)
    g_h1, g_h1b = _outproj(g_oattn, g_orwkv, g_h0, w_out_bf, g1, b1, TM)
    _, m_h1b = _outproj(m_oattn, m_orwkv, m_h0, w_out_bf, g1, b1, x_m.shape[0])

    out_rows = [x.shape[0] * x.shape[1] for x in xs]
    act = _ffn_in(g_h1b, m_h1b, ffn_w_in_bf, ffn_conv_w[0], ffn_conv_b.reshape(1, D_FF),
                  _ffn_tables(seq_lens))
    ys, row0 = [], 0
    for n_rows in out_rows:
        ys.append(_ffn_out(act, g_h1, ffn_w_out_bf, ln2_g.reshape(1, D_MODEL),
                           ln2_b.reshape(1, D_MODEL), row0, n_rows))
        row0 += n_rows
    return [y.reshape(x.shape) for y, x in zip(ys, xs)]


def kernel(x_prompt, x_sample, meta_tokens, emb_ln_g, emb_ln_b, w_in, attn_rpb, rwkv_mu, rwkv_w0, rwkv_w2, rwkv_a0, rwkv_a2, rwkv_g2, rwkv_k_k, rwkv_k_a, rwkv_r_k, rwkv_lnx_g, rwkv_lnx_b, w_out, ln1_g, ln1_b, ffn_w_in, ffn_conv_w, ffn_conv_b, ffn_w_out, ln2_g, ln2_b):
    y_prompt, y_sample = _encode_all(
        [x_prompt, x_sample], meta_tokens, emb_ln_g, emb_ln_b, w_in, attn_rpb, rwkv_mu, rwkv_w0,
        rwkv_w2, rwkv_a0, rwkv_a2, rwkv_g2, rwkv_k_k, rwkv_k_a, rwkv_r_k, rwkv_lnx_g, rwkv_lnx_b,
        w_out, ln1_g, ln1_b, ffn_w_in, ffn_conv_w, ffn_conv_b, ffn_w_out, ln2_g, ln2_b)
    return (y_prompt, y_sample)
```

```python
import functools

import numpy as np
import jax
import jax.numpy as jnp
from jax import lax
from jax.experimental import pallas as pl
from jax.experimental.pallas import tpu as pltpu

D_MODEL = 2048
N_META = 16
GRID_W = 64
WIN_H = 8
WIN_W = 16
D_ATTN = 1024
HD_ATTN = 128
H_ATTN = D_ATTN // HD_ATTN
D_RWKV = 1024
HD_RWKV = 64
R_DECAY = 96
R_ICLR = 96
R_GATE = 256
D_FF = 5632
DEEPNORM_ALPHA = 2.0 ** 0.25
LN_EPS = 1e-5
GN_EPS = 64e-5
NEG_INF = -1e30

LANE = 128
CHUNK = 64
PAD_ROWS = CHUNK - N_META
R_LOW = 128
N_SHIFT_P = 3 * D_RWKV + 2 * R_LOW
N_RW = N_SHIFT_P + R_GATE
N_QKV = 3 * D_ATTN
N_PAIR = D_RWKV // LANE
QB_ROWS = 8
QB_TOK = QB_ROWS * GRID_W
KV_BLK = 256
WIN_TOK = 4 * KV_BLK
TM = 512
TM_FFN = 1024
TM_SLAB = 1024
TM_OUT = 256
SLAB_OUT = 128
N_SLAB = 4
FC = 512
FC_SUB = 256
HALO = 16
VMEM_LIMIT = 60 * 1024 * 1024

F32 = jnp.float32
BF16 = jnp.bfloat16


def _layer_norm(x, g, b):
    mu = jnp.mean(x, axis=-1, keepdims=True)
    xc = x - mu
    var = jnp.mean(xc * xc, axis=-1, keepdims=True)
    return xc * lax.rsqrt(var + LN_EPS) * g + b


def _dot(a, b):
    return jnp.dot(a, b, preferred_element_type=F32)


def _dot_nt(a, b):
    return lax.dot_general(a, b, (((1,), (1,)), ((), ())), preferred_element_type=F32)


def _bmm(a, b):
    return jnp.einsum("bik,bkj->bij", a, b, preferred_element_type=F32)


def _bmm_nt(a, b):
    return jnp.einsum("bik,bjk->bij", a, b, preferred_element_type=F32)


def _bmm_tn(a, b):
    return jnp.einsum("bki,bkj->bij", a, b, preferred_element_type=F32)


def _select_rows(x_refs, tiles_first, rows):
    if len(x_refs) == 1:
        return x_refs[0][rows, :]
    return jnp.where(pl.program_id(0) < tiles_first, x_refs[0][rows, :], x_refs[1][rows, :])


def _emit_skewed(n_slabs, stages):
    depth = max(lag for lag, _ in stages)
    for k in range(n_slabs + depth):
        for lag, fn in stages:
            if 0 <= k - lag < n_slabs:
                fn(k - lag)


def _row_specs(xs, tm, width):
    if len(xs) == 1:
        return [pl.BlockSpec((tm, width), lambda i: (i, 0))], 0
    tiles_first = xs[0].shape[0] // tm
    return [pl.BlockSpec((tm, width), lambda i: (jnp.minimum(i, tiles_first - 1), 0)),
            pl.BlockSpec((tm, width), lambda i: (jnp.maximum(i - tiles_first, 0), 0))], tiles_first


def _inproj_kernel(*refs, n_x, tiles_first, q_scale):
    x_refs = refs[:n_x]
    g_ref, b_ref, w_ref, o_ref = refs[n_x:n_x + 4]
    h0_ref = refs[n_x + 4] if q_scale else None
    slab = o_ref.shape[0] // N_SLAB
    rows = lambda s: slice(s * slab, (s + 1) * slab)
    h = {}

    def norm(s):
        h0 = _layer_norm(_select_rows(x_refs, tiles_first, rows(s)), g_ref[...], b_ref[...])
        if h0_ref is not None:
            h0_ref[rows(s), :] = h0
        h[s] = h0.astype(BF16)

    def project(s):
        acc = _dot(h.pop(s), w_ref[...])
        if q_scale:
            o_ref[rows(s), :D_ATTN] = (acc[:, :D_ATTN] * (HD_ATTN ** -0.5)).astype(o_ref.dtype)
            o_ref[rows(s), D_ATTN:] = acc[:, D_ATTN:].astype(o_ref.dtype)
        else:
            o_ref[rows(s), :] = acc.astype(o_ref.dtype)

    _emit_skewed(N_SLAB, [(1, project), (0, norm)])


def _inproj(xs, ln_g, ln_b, w, out_dtype, q_scale, tm):
    rows = sum(x.shape[0] for x in xs)
    n_out = w.shape[1]
    x_specs, tiles_first = _row_specs(xs, tm, D_MODEL)
    cmap = lambda i: (0, 0)
    out_specs = pl.BlockSpec((tm, n_out), lambda i: (i, 0))
    out_shape = jax.ShapeDtypeStruct((rows, n_out), out_dtype)
    if q_scale:
        out_specs = [out_specs, pl.BlockSpec((tm, D_MODEL), lambda i: (i, 0))]
        out_shape = [out_shape, jax.ShapeDtypeStruct((rows, D_MODEL), F32)]
    return pl.pallas_call(
        functools.partial(_inproj_kernel, n_x=len(xs), tiles_first=tiles_first, q_scale=q_scale),
        grid=(rows // tm,),
        in_specs=x_specs + [
            pl.BlockSpec((1, D_MODEL), cmap),
            pl.BlockSpec((1, D_MODEL), cmap),
            pl.BlockSpec((D_MODEL, n_out), cmap, pipeline_mode=pl.Buffered(1)),
        ],
        out_specs=out_specs,
        out_shape=out_shape,
        compiler_params=pltpu.CompilerParams(
            dimension_semantics=("arbitrary",), vmem_limit_bytes=VMEM_LIMIT),
        name="inproj_qkv" if q_scale else "inproj_rwkv",
    )(*xs, ln_g, ln_b, w)


def _window_rows(variant, i):
    half = WIN_H // 2
    if variant == 0:
        off, r0 = 0, max(i - half, 0)
    elif variant == 1:
        off, r0 = -half, i - half
    else:
        off, r0 = -WIN_H, min(i - half, 0)
    return [j for j in range(2 * QB_ROWS) if r0 <= j + off < r0 + WIN_H], off


def _attn_tile_plan(variant):
    plan = []
    for i in range(QB_ROWS):
        rows, off = _window_rows(variant, i)
        tiles = []
        for jt in range(QB_ROWS):
            first, second = 2 * jt in rows, 2 * jt + 1 in rows
            if not (first or second):
                continue
            kind = 0 if (first and second) else (1 if second else 2)
            e = 2 * jt + off - i + WIN_H
            assert (not first or 0 <= e - 1 <= 2 * WIN_H - 2) and (not second or 0 <= e <= 2 * WIN_H - 2)
            tiles.append((jt, kind, e))
        plan.append(tiles)
    return plan


def _attn_kernel(qblk_s, kvblk_s, var_s, mseq_s, q_ref, k_ref, v_ref,
                 mk_ref, mv_ref, bt_ref, o_ref, p_scr):
    del qblk_s, kvblk_s, mseq_s
    variant = var_s[pl.program_id(1)]
    q = q_ref[...]
    s_win = _dot_nt(q, k_ref[...])
    km = mk_ref[PAD_ROWS:, :]
    vm = mv_ref[PAD_ROWS:, :]
    sm = _dot_nt(q, km)

    def softmax_pv(plan):
        def score_tile(i, jt, kind, e):
            rows = slice(i * GRID_W, (i + 1) * GRID_W)
            return s_win[rows, jt * LANE:(jt + 1) * LANE] + bt_ref[0, kind, e]

        tile_max = []
        for i, tiles in enumerate(plan):
            t = None
            for jt, kind, e in tiles:
                x = score_tile(i, jt, kind, e)
                t = x if t is None else jnp.maximum(t, x)
            tile_max.append(t)
        m = jnp.maximum(jnp.max(jnp.concatenate(tile_max, axis=0), axis=-1, keepdims=True),
                        jnp.max(sm, axis=-1, keepdims=True))
        pm = jnp.exp(sm - m)
        p_scr[...] = jnp.zeros_like(p_scr)
        tile_sum = []
        for i, tiles in enumerate(plan):
            rows = slice(i * GRID_W, (i + 1) * GRID_W)
            mi = m[rows]
            t = None
            for jt, kind, e in tiles:
                p = jnp.exp(score_tile(i, jt, kind, e) - mi)
                p_scr[rows, jt * LANE:(jt + 1) * LANE] = p.astype(BF16)
                t = p if t is None else t + p
            tile_sum.append(t)
        l = (jnp.sum(jnp.concatenate(tile_sum, axis=0), axis=-1, keepdims=True)
             + jnp.sum(pm, axis=-1, keepdims=True))
        o = _dot(pm.astype(BF16), vm)
        o = o + _dot(p_scr[...], v_ref[...])
        o_ref[...] = (o / l).astype(BF16)

    for v in range(3):
        pl.when(variant == v)(functools.partial(softmax_pv, _attn_tile_plan(v)))


def _attention(g_qkv, m_qkv, bias, tabs):
    qblk, kvblk, var, mseq = tabs
    nsteps = qblk.shape[0]
    rows = g_qkv.shape[0]

    def window_spec(col0):
        return pl.BlockSpec((pl.Element(WIN_TOK), pl.Element(HD_ATTN)),
                            lambda h, t, qb, kb, vr, ms: (kb[t] * KV_BLK, (col0 + h) * HD_ATTN))

    in_specs = [pl.BlockSpec((QB_TOK, HD_ATTN), lambda h, t, qb, kb, vr, ms: (qb[t], h)),
                window_spec(H_ATTN), window_spec(2 * H_ATTN)]
    in_specs += [
        pl.BlockSpec((CHUNK, HD_ATTN), lambda h, t, qb, kb, vr, ms: (ms[t], H_ATTN + h)),
        pl.BlockSpec((CHUNK, HD_ATTN), lambda h, t, qb, kb, vr, ms: (ms[t], 2 * H_ATTN + h)),
        pl.BlockSpec((1, 3, 2 * WIN_H, GRID_W, LANE), lambda h, t, qb, kb, vr, ms: (h, 0, 0, 0, 0)),
    ]
    grid_spec = pltpu.PrefetchScalarGridSpec(
        num_scalar_prefetch=4,
        grid=(H_ATTN, nsteps),
        in_specs=in_specs,
        out_specs=pl.BlockSpec((QB_TOK, HD_ATTN), lambda h, t, qb, kb, vr, ms: (qb[t], h)),
        scratch_shapes=[pltpu.VMEM((QB_TOK, WIN_TOK), BF16)],
    )
    return pl.pallas_call(
        _attn_kernel,
        grid_spec=grid_spec,
        out_shape=jax.ShapeDtypeStruct((rows, D_ATTN), BF16),
        compiler_params=pltpu.CompilerParams(
            dimension_semantics=("arbitrary", "arbitrary"), vmem_limit_bytes=VMEM_LIMIT),
        name="nbr_attention",
    )(qblk, kvblk, var, mseq, g_qkv, g_qkv, g_qkv, m_qkv, m_qkv, bias)


def _meta_attn_kernel(qkv_ref, o_ref):
    for h in range(H_ATTN):
        cols = lambda part: slice((part * H_ATTN + h) * HD_ATTN, (part * H_ATTN + h + 1) * HD_ATTN)
        q = qkv_ref[:, cols(0)]
        km = qkv_ref[PAD_ROWS:, cols(1)]
        vm = qkv_ref[PAD_ROWS:, cols(2)]
        s = _dot_nt(q, km)
        m = jnp.max(s, axis=-1, keepdims=True)
        p = jnp.exp(s - m)
        l = jnp.sum(p, axis=-1, keepdims=True)
        o_ref[:, h * HD_ATTN:(h + 1) * HD_ATTN] = (_dot(p.astype(BF16), vm) / l).astype(BF16)


def _meta_attention(m_qkv):
    nseq = m_qkv.shape[0] // CHUNK
    return pl.pallas_call(
        _meta_attn_kernel,
        grid=(nseq,),
        in_specs=[pl.BlockSpec((CHUNK, N_QKV), lambda s: (s, 0))],
        out_specs=pl.BlockSpec((CHUNK, D_ATTN), lambda s: (s, 0)),
        out_shape=jax.ShapeDtypeStruct((m_qkv.shape[0], D_ATTN), BF16),
        compiler_params=pltpu.CompilerParams(dimension_semantics=("arbitrary",)),
        name="meta_attention",
    )(m_qkv)


def _attn_bias_tiles(rpb):
    qc = np.arange(GRID_W)
    kc = np.arange(GRID_W)
    c0 = np.clip(qc - WIN_W // 2, 0, GRID_W - WIN_W)
    colmask = (kc[None, :] >= c0[:, None]) & (kc[None, :] < c0[:, None] + WIN_W)
    dc = np.clip(kc[None, :] - qc[:, None], -(WIN_W - 1), WIN_W - 1) + (WIN_W - 1)
    n_dr = 2 * WIN_H - 1
    n_dc = 2 * WIN_W - 1
    dc_onehot = (dc[None] == np.arange(n_dc)[:, None, None]).astype(np.float32)
    toeplitz = jnp.einsum("hrd,dqk->hrqk", rpb, jnp.asarray(dc_onehot),
                          precision=lax.Precision.HIGHEST)
    toeplitz = jnp.where(jnp.asarray(colmask)[None, None], toeplitz, NEG_INF)
    masked = jnp.full((H_ATTN, 1, GRID_W, GRID_W), NEG_INF, F32)
    padded = jnp.concatenate([masked, toeplitz, masked], axis=1)
    first = padded[:, 0:n_dr + 1]
    second = padded[:, 1:n_dr + 2]
    off = jnp.full_like(first, NEG_INF)
    kinds = [jnp.concatenate([a, b], axis=-1) for a, b in ((first, second), (off, second), (first, off))]
    return jnp.stack(kinds, axis=1)


def _split_bf16(x, n):
    parts = []
    rem = x
    for _ in range(n):
        p = rem.astype(BF16)
        parts.append(p)
        rem = rem - p.astype(F32)
    return parts


def _seg_sum(x, e2):
    stacked = jnp.concatenate([x[:, j * LANE:(j + 1) * LANE] for j in range(N_PAIR)], axis=0)
    hi, lo = _split_bf16(stacked, 2)
    res = _dot(hi, e2) + _dot(lo, e2)
    return jnp.concatenate([res[j * CHUNK:(j + 1) * CHUNK] for j in range(N_PAIR)], axis=1)


_OP_A, _OP_R, _OP_BH, _OP_KH, _OP_V, _OP_B, _OP_K = range(7)
N_OPND = 7


def _rwkv_kernel(gblk_s, mblk_s, flag_s, *refs, reverse):
    del gblk_s, mblk_s
    if reverse:
        (gp_ref, mp_ref, mu_ref, w0_ref, w2_ref, a0_ref, a2_ref, kk_ref, ka_ref, rk_ref,
         gyb_ref, myb_ref, g2_ref, lg_ref, lb_ref, og_ref, om_ref,
         state, carry, opnd, aux, cend, opnd_n, aux_n, cend_n) = refs
    else:
        (gp_ref, mp_ref, mu_ref, w0_ref, w2_ref, a0_ref, a2_ref, kk_ref, ka_ref, rk_ref,
         og_ref, om_ref, state, carry, opnd, aux, cend, opnd_n, aux_n, cend_n) = refs
    step = pl.program_id(0)
    fl = flag_s[step]
    fl_prev = flag_s[jnp.maximum(step - 1, 0)]
    is_first = (fl & 1) == 1
    is_meta = (fl & 2) == 2
    prev_first = (fl_prev & 1) == 1
    prev_meta = (fl_prev & 2) == 2

    @pl.when(step == 0)
    def _():
        opnd[...] = jnp.zeros_like(opnd)
        aux[...] = jnp.zeros_like(aux)
        cend[...] = jnp.zeros_like(cend)

    @pl.when(is_first)
    def _():
        carry[...] = jnp.zeros_like(carry)

    @pl.when(prev_first)
    def _():
        state[...] = jnp.zeros_like(state)

    ii = lax.broadcasted_iota(jnp.int32, (LANE, LANE), 0)
    jj = lax.broadcasted_iota(jnp.int32, (LANE, LANE), 1)
    same = (ii // CHUNK) == (jj // CHUNK)
    e2 = jnp.where(same, 1.0, 0.0).astype(BF16)

    ti2 = ii % CHUNK
    tj2 = jj % CHUNK
    if reverse:
        strict = jnp.logical_and(same, tj2 > ti2)
        incl = jnp.logical_and(same, tj2 >= ti2)
    else:
        strict = jnp.logical_and(same, tj2 < ti2)
        incl = jnp.logical_and(same, tj2 <= ti2)
    eye = jnp.where(ii == jj, 1.0, 0.0).astype(F32)
    lane64 = lax.broadcasted_iota(jnp.int32, (1, CHUNK, LANE), 2) < HD_RWKV

    def tiles(x):
        return jnp.stack([x[:, j * LANE:(j + 1) * LANE] for j in range(N_PAIR)], axis=0)

    def stack_masked(x):
        zero = jnp.zeros_like(x)
        return jnp.concatenate([jnp.where(lane64, x, zero), jnp.where(lane64, zero, x)], axis=1)

    def stack_dup(x):
        return jnp.concatenate([x, x], axis=1)

    def prepare():
        praw = jnp.where(is_meta, mp_ref[...], gp_ref[...])
        row = lax.broadcasted_iota(jnp.int32, (CHUNK, 1), 0)
        pad = jnp.logical_and(is_meta, row < PAD_ROWS)
        p_sh = jnp.where(pad, 0.0, praw[:, :N_SHIFT_P])
        gd = praw[:, N_SHIFT_P:]
        if reverse:
            rolled = pltpu.roll(p_sh, CHUNK - 1, axis=0)
            nb = jnp.where(row == CHUNK - 1, carry[0:1, :], rolled)
            carry[...] = p_sh[0:8, :]
        else:
            rolled = pltpu.roll(p_sh, 1, axis=0)
            nb = jnp.where(row == 0, carry[7:8, :], rolled)
            carry[...] = p_sh[CHUNK - 8:CHUNK, :]
        f = p_sh + (nb - p_sh) * mu_ref[...]
        r = f[:, 0:D_RWKV]
        k = f[:, D_RWKV:2 * D_RWKV]
        v = f[:, 2 * D_RWKV:3 * D_RWKV]
        wd = f[:, 3 * D_RWKV:3 * D_RWKV + R_LOW]
        ad = f[:, 3 * D_RWKV + R_LOW:3 * D_RWKV + 2 * R_LOW]
        opnd_n[_OP_V] = v.astype(BF16)
        yield
        wl = w0_ref[...] + _dot(jnp.tanh(wd).astype(BF16), w2_ref[...])
        nwl = -wl
        softplus = jnp.maximum(nwl, 0.0) + jnp.log(1.0 + jnp.exp(-jnp.abs(nwl)))
        lw = -jnp.exp(-softplus - 0.5)
        yield
        a = 1.0 / (1.0 + jnp.exp(-(a0_ref[...] + _dot(ad.astype(BF16), a2_ref[...]))))
        kmod = k * (1.0 + (a - 1.0) * ka_ref[...])
        yield
        kkr = k * kk_ref[...]
        kk = kkr / jnp.maximum(jnp.sqrt(_seg_sum(kkr * kkr, e2)), 1e-12)
        yield
        aux_n[0] = _seg_sum(r * kmod * rk_ref[...], e2) * v
        if reverse:
            aux_n[1] = _dot((1.0 / (1.0 + jnp.exp(-gd))).astype(BF16), g2_ref[...])
        yield
        ti = lax.broadcasted_iota(jnp.int32, (CHUNK, CHUNK), 0)
        tj = lax.broadcasted_iota(jnp.int32, (CHUNK, CHUNK), 1)
        tri = jnp.where((tj >= ti) if reverse else (tj <= ti), 1.0, 0.0).astype(BF16)
        cum = sum(_dot(tri, part) for part in _split_bf16(lw, 3))
        cum_end = cum[0:1, :] if reverse else cum[CHUNK - 1:CHUNK, :]
        cend_n[0:1, :] = cum_end
        yield
        opnd_n[_OP_R] = (r * jnp.exp(cum)).astype(BF16)
        yield
        e_neg = jnp.exp(-cum)
        bb = kk * a
        opnd_n[_OP_K] = (kmod * e_neg).astype(BF16)
        opnd_n[_OP_B] = (bb * e_neg).astype(BF16)
        yield
        opnd_n[_OP_A] = (-kk * jnp.exp(cum - lw)).astype(BF16)
        yield
        e_end = jnp.exp(cum_end - cum)
        opnd_n[_OP_KH] = (kmod * e_end).astype(BF16)
        opnd_n[_OP_BH] = (bb * e_end).astype(BF16)

    prep_steps = prepare()

    def emit_prep():
        next(prep_steps, None)

    v3 = tiles(opnd[_OP_V])
    a_s = stack_masked(tiles(opnd[_OP_A]))
    r_s = stack_masked(tiles(opnd[_OP_R]))
    bh_s = stack_masked(tiles(opnd[_OP_BH]))
    kh_s = stack_masked(tiles(opnd[_OP_KH]))
    v_s = stack_masked(v3)
    b_d = stack_dup(tiles(opnd[_OP_B]))
    k_d = stack_dup(tiles(opnd[_OP_K]))
    v_d = stack_dup(v3)
    cum_end_prev = cend[0:1, :]
    emit_prep()
    m1 = _bmm_nt(jnp.concatenate([a_s, r_s], axis=1),
                 jnp.concatenate([b_d, k_d], axis=1))
    emit_prep()
    n_ab = jnp.where(strict, m1[:, :LANE, :LANE], 0.0)
    a_ak = jnp.where(strict, m1[:, :LANE, LANE:], 0.0)
    a_rb = jnp.where(incl, m1[:, LANE:, :LANE], 0.0)
    a_rk = jnp.where(incl, m1[:, LANE:, LANE:], 0.0)
    eye_b = eye.astype(BF16)
    tinv = (eye + n_ab).astype(BF16)
    npow = n_ab.astype(BF16)
    for _ in range(5):
        npow = _bmm(npow, npow).astype(BF16)
        emit_prep()
        tinv = _bmm(tinv, eye_b + npow).astype(BF16)
    gy = _bmm(jnp.concatenate([a_ak, a_rk], axis=1).astype(BF16), v_d)
    emit_prep()
    g_s = jnp.where(same, gy[:, :LANE], 0.0)
    y0_s = jnp.where(same, gy[:, LANE:], 0.0)
    tz = _bmm(tinv, jnp.concatenate([a_s, g_s.astype(BF16)], axis=2))
    emit_prep()
    w_s = tz[:, :, :LANE]
    u0_s = tz[:, :, LANE:]
    s_add = _bmm_tn(kh_s, v_s)
    s_old = state[...]
    xr = _bmm(jnp.concatenate([w_s.astype(BF16), r_s], axis=1), s_old.astype(BF16))
    emit_prep()
    u_s = xr[:, :LANE] + u0_s
    y_s = xr[:, LANE:] + _bmm(a_rb.astype(BF16), u_s.astype(BF16)) + y0_s
    emit_prep()
    y3 = y_s[:, :CHUNK] + y_s[:, CHUNK:]
    g_end = jnp.exp(jnp.swapaxes(jnp.broadcast_to(tiles(cum_end_prev), (N_PAIR, LANE, LANE)), 1, 2))
    state[...] = g_end * s_old + _bmm_tn(bh_s, u_s.astype(BF16)) + s_add
    y = jnp.concatenate([y3[j] for j in range(N_PAIR)], axis=1)

    if not reverse:
        res = jnp.concatenate([y, aux[0]], axis=1)
    else:
        yb = jnp.where(prev_meta, myb_ref[...], gyb_ref[...])
        yy = y + yb[:, :D_RWKV]
        mean = _seg_sum(yy, e2) * (1.0 / HD_RWKV)
        yc = yy - mean
        var = _seg_sum(yc * yc, e2) * (1.0 / HD_RWKV)
        yn = yc * lax.rsqrt(var + GN_EPS) * lg_ref[...] + lb_ref[...]
        res = ((yn + yb[:, D_RWKV:] + aux[0]) * aux[1]).astype(BF16)

    for _ in prep_steps:
        pass
    opnd[...] = opnd_n[...]
    aux[...] = aux_n[...]
    cend[0:1, :] = cend_n[0:1, :]

    @pl.when(prev_meta)
    def _():
        om_ref[...] = res

    @pl.when(jnp.logical_not(prev_meta))
    def _():
        og_ref[...] = res


def _rwkv_scan(g_prw, m_prw, params, tabs, reverse, prev=None):
    gblk, mblk, flags = tabs
    nsteps = gblk.shape[0]
    gmap = lambda t, gb, mb, fl: (gb[t], 0)
    mmap = lambda t, gb, mb, fl: (mb[t], 0)
    gmap_prev = lambda t, gb, mb, fl: (gb[jnp.maximum(t - 1, 0)], 0)
    mmap_prev = lambda t, gb, mb, fl: (mb[jnp.maximum(t - 1, 0)], 0)
    cmap = lambda t, gb, mb, fl: (0, 0)
    vec = lambda n: pl.BlockSpec((1, n), cmap)
    in_specs = [
        pl.BlockSpec((CHUNK, N_RW), gmap),
        pl.BlockSpec((CHUNK, N_RW), mmap),
        vec(N_SHIFT_P), vec(D_RWKV),
        pl.BlockSpec((R_LOW, D_RWKV), cmap), vec(D_RWKV),
        pl.BlockSpec((R_LOW, D_RWKV), cmap), vec(D_RWKV), vec(D_RWKV), vec(D_RWKV),
    ]
    args = [g_prw, m_prw, params["mu"], params["w0"], params["w2"], params["a0"], params["a2"],
            params["k_k"], params["k_a"], params["r_k"]]
    if reverse:
        g_yb, m_yb = prev
        in_specs += [
            pl.BlockSpec((CHUNK, 2 * D_RWKV), gmap_prev),
            pl.BlockSpec((CHUNK, 2 * D_RWKV), mmap_prev),
            pl.BlockSpec((R_GATE, D_RWKV), cmap), vec(D_RWKV), vec(D_RWKV),
        ]
        args += [g_yb, m_yb, params["g2"], params["lnx_g"], params["lnx_b"]]
        width, dt, n_aux = D_RWKV, BF16, 2
    else:
        width, dt, n_aux = 2 * D_RWKV, F32, 1
    grid_spec = pltpu.PrefetchScalarGridSpec(
        num_scalar_prefetch=3,
        grid=(nsteps,),
        in_specs=in_specs,
        out_specs=[pl.BlockSpec((CHUNK, width), gmap_prev),
                   pl.BlockSpec((CHUNK, width), mmap_prev)],
        scratch_shapes=[
            pltpu.VMEM((N_PAIR, LANE, LANE), F32),
            pltpu.VMEM((8, N_SHIFT_P), F32),
            pltpu.VMEM((N_OPND, CHUNK, D_RWKV), BF16),
            pltpu.VMEM((n_aux, CHUNK, D_RWKV), F32),
            pltpu.VMEM((8, D_RWKV), F32),
            pltpu.VMEM((N_OPND, CHUNK, D_RWKV), BF16),
            pltpu.VMEM((n_aux, CHUNK, D_RWKV), F32),
            pltpu.VMEM((8, D_RWKV), F32),
        ],
    )
    return pl.pallas_call(
        functools.partial(_rwkv_kernel, reverse=reverse),
        grid_spec=grid_spec,
        out_shape=[jax.ShapeDtypeStruct((g_prw.shape[0], width), dt),
                   jax.ShapeDtypeStruct((m_prw.shape[0], width), dt)],
        compiler_params=pltpu.CompilerParams(
            dimension_semantics=("arbitrary",), vmem_limit_bytes=VMEM_LIMIT),
        name="rwkv_bwd" if reverse else "rwkv_fwd",
    )(gblk, mblk, flags, *args)


def _outproj_kernel(h0_ref, oa_ref, or_ref, wa_ref, wr_ref, g1_ref, b1_ref, h1_ref, h1b_ref):
    mix = _dot(oa_ref[...], wa_ref[...]) + _dot(or_ref[...], wr_ref[...])
    h1 = _layer_norm(DEEPNORM_ALPHA * h0_ref[...] + mix, g1_ref[...], b1_ref[...])
    h1_ref[...] = h1
    h1b_ref[...] = h1.astype(BF16)


def _outproj(o_attn, o_rwkv, h0, w_out_bf, g1, b1, tm):
    rows = o_attn.shape[0]
    cmap = lambda i: (0, 0)
    vec = pl.BlockSpec((1, D_MODEL), cmap)
    return pl.pallas_call(
        _outproj_kernel,
        grid=(rows // tm,),
        in_specs=[
            pl.BlockSpec((tm, D_MODEL), lambda i: (i, 0)),
            pl.BlockSpec((tm, D_ATTN), lambda i: (i, 0)),
            pl.BlockSpec((tm, D_RWKV), lambda i: (i, 0)),
            pl.BlockSpec((D_ATTN, D_MODEL), lambda i: (0, 0), pipeline_mode=pl.Buffered(1)),
            pl.BlockSpec((D_RWKV, D_MODEL), lambda i: (1, 0), pipeline_mode=pl.Buffered(1)),
            vec, vec,
        ],
        out_specs=[pl.BlockSpec((tm, D_MODEL), lambda i: (i, 0)),
                   pl.BlockSpec((tm, D_MODEL), lambda i: (i, 0))],
        out_shape=[jax.ShapeDtypeStruct((rows, D_MODEL), F32),
                   jax.ShapeDtypeStruct((rows, D_MODEL), BF16)],
        compiler_params=pltpu.CompilerParams(
            dimension_semantics=("arbitrary",), vmem_limit_bytes=VMEM_LIMIT),
        name="outproj_ln1",
    )(h0, o_attn, o_rwkv, w_out_bf, w_out_bf, g1, b1)


def _ffn_in_kernel(pg_s, pm_s, nx_s, fl_s, h_ref, pg_ref, pm_ref, nx_ref, wg_ref, wu_ref,
                   cw_ref, cb_ref, act_ref, lhs, gate_s):
    del pg_s, pm_s, nx_s
    i = pl.program_id(0)
    c = pl.program_id(1)
    fl = fl_s[i]

    @pl.when(c == 0)
    def _():
        lhs[0:HALO, :] = jnp.where((fl & 1) == 1, pm_ref[...], pg_ref[...])
        lhs[HALO:HALO + TM_FFN, :] = h_ref[...]
        lhs[HALO + TM_FFN:, :] = jnp.where((fl & 2) == 2, jnp.zeros_like(nx_ref), nx_ref[...])

    units = [(s, k) for s in range(TM_FFN // TM_SLAB) for k in range(FC // FC_SUB)]

    def matmuls(u):
        s, k = units[u]
        cols = slice(k * FC_SUB, (k + 1) * FC_SUB)
        r0 = s * TM_SLAB
        gate_s[u] = _dot(lhs[r0:r0 + TM_SLAB + 2 * HALO, :], wg_ref[:, cols])

    def activation(u):
        s, k = units[u]
        cols = slice(k * FC_SUB, (k + 1) * FC_SUB)
        r0 = s * TM_SLAB
        up = _dot(lhs[r0 + HALO:r0 + HALO + TM_SLAB, :], wu_ref[:, cols])
        g_ext = gate_s[u]
        n_ext = TM_SLAB + 2 * HALO
        centre = slice(HALO, HALO + TM_SLAB)
        gate = (pltpu.roll(g_ext, 1, axis=0)[centre] * cw_ref[0:1, cols]
                + g_ext[centre] * cw_ref[1:2, cols]
                + pltpu.roll(g_ext, n_ext - 1, axis=0)[centre] * cw_ref[2:3, cols] + cb_ref[:, cols])
        act = gate * (1.0 + lax.erf(gate * 0.7071067811865476)) * up
        act_ref[s * TM_SLAB:(s + 1) * TM_SLAB, cols] = act.astype(BF16)

    matmuls(0)
    for u in range(len(units)):
        if u + 1 < len(units):
            matmuls(u + 1)
        activation(u)


def _ffn_in(g_h1b, m_h1b, w_in_bf, conv_w, conv_b, tabs):
    pg, pm, nx, fl = tabs
    rows = g_h1b.shape[0]
    nfc = D_FF // FC
    n_units = (TM_FFN // TM_SLAB) * (FC // FC_SUB)
    grid_spec = pltpu.PrefetchScalarGridSpec(
        num_scalar_prefetch=4,
        grid=(rows // TM_FFN, nfc),
        in_specs=[
            pl.BlockSpec((TM_FFN, D_MODEL), lambda i, c, *_: (i, 0)),
            pl.BlockSpec((HALO, D_MODEL), lambda i, c, pg, pm, nx, fl: (pg[i], 0)),
            pl.BlockSpec((HALO, D_MODEL), lambda i, c, pg, pm, nx, fl: (pm[i], 0)),
            pl.BlockSpec((HALO, D_MODEL), lambda i, c, pg, pm, nx, fl: (nx[i], 0)),
            pl.BlockSpec((D_MODEL, FC), lambda i, c, *_: (0, c)),
            pl.BlockSpec((D_MODEL, FC), lambda i, c, *_: (0, nfc + c)),
            pl.BlockSpec((3, FC), lambda i, c, *_: (0, c)),
            pl.BlockSpec((1, FC), lambda i, c, *_: (0, c)),
        ],
        out_specs=pl.BlockSpec((TM_FFN, FC), lambda i, c, *_: (i, c)),
        scratch_shapes=[
            pltpu.VMEM((TM_FFN + 2 * HALO, D_MODEL), BF16),
            pltpu.VMEM((n_units, TM_SLAB + 2 * HALO, FC_SUB), F32),
        ],
    )
    return pl.pallas_call(
        _ffn_in_kernel,
        grid_spec=grid_spec,
        out_shape=jax.ShapeDtypeStruct((rows, D_FF), BF16),
        compiler_params=pltpu.CompilerParams(
            dimension_semantics=("arbitrary", "arbitrary"), vmem_limit_bytes=VMEM_LIMIT),
        name="ffn_in",
    )(pg, pm, nx, fl, g_h1b, g_h1b, m_h1b, g_h1b, w_in_bf, w_in_bf, conv_w, conv_b)


def _ffn_out_kernel(act_ref, h_ref, wo_ref, g2_ref, b2_ref, y_ref):
    n_slab = TM_OUT // SLAB_OUT
    rows = lambda s: slice(s * SLAB_OUT, (s + 1) * SLAB_OUT)
    ffn = {}

    def project(s):
        ffn[s] = _dot(act_ref[rows(s), :], wo_ref[...])

    def norm(s):
        y_ref[rows(s), :] = _layer_norm(DEEPNORM_ALPHA * h_ref[rows(s), :] + ffn.pop(s),
                                        g2_ref[...], b2_ref[...])

    _emit_skewed(n_slab, [(0, project), (1, norm)])


def _ffn_out(act, g_h1, w_out_bf, g2, b2, row0, n_rows):
    tm = TM_OUT
    tile0 = row0 // tm
    cmap = lambda i: (0, 0)
    vec = pl.BlockSpec((1, D_MODEL), cmap)
    return pl.pallas_call(
        _ffn_out_kernel,
        grid=(n_rows // tm,),
        in_specs=[
            pl.BlockSpec((tm, D_FF), lambda i: (tile0 + i, 0)),
            pl.BlockSpec((tm, D_MODEL), lambda i: (tile0 + i, 0)),
            pl.BlockSpec((D_FF, D_MODEL), cmap, pipeline_mode=pl.Buffered(1)),
            vec, vec,
        ],
        out_specs=pl.BlockSpec((tm, D_MODEL), lambda i: (i, 0)),
        out_shape=jax.ShapeDtypeStruct((n_rows, D_MODEL), F32),
        compiler_params=pltpu.CompilerParams(
            dimension_semantics=("arbitrary",), vmem_limit_bytes=VMEM_LIMIT),
        name="ffn_out_ln2",
    )(act, g_h1, w_out_bf, g2, b2)


def _attn_tables(seq_lens):
    rec = []
    base = 0
    for s, t_len in enumerate(seq_lens):
        rows = t_len // GRID_W
        for qb in range(rows // QB_ROWS):
            w0 = min(max(qb * QB_ROWS - WIN_H // 2, 0), rows - 2 * QB_ROWS)
            variant = 0 if qb == 0 else (2 if qb == rows // QB_ROWS - 1 else 1)
            rec.append((variant, (base + qb * QB_TOK) // QB_TOK, (base + w0 * GRID_W) // KV_BLK, s))
        base += t_len
    rec.sort(key=lambda x: x[0])
    arr = np.asarray(rec, np.int32)
    return tuple(jnp.asarray(arr[:, c]) for c in (1, 2, 0, 3))


def _rwkv_tables(seq_lens, reverse):
    gblk, mblk, flags = [], [], []
    base = 0
    for s, t_len in enumerate(seq_lens):
        n = t_len // CHUNK
        first_blk = base // CHUNK
        steps = [(first_blk, 2)] + [(first_blk + c, 0) for c in range(n)]
        if reverse:
            steps = steps[::-1]
        for idx, (blk, fl) in enumerate(steps):
            gblk.append(blk)
            mblk.append(s)
            flags.append(fl | (1 if idx == 0 else 0))
        base += t_len
    gblk.append(gblk[-1])
    mblk.append(mblk[-1])
    flags.append(0)
    return tuple(jnp.asarray(np.asarray(a, np.int32)) for a in (gblk, mblk, flags))


def _ffn_tables(seq_lens):
    pg, pm, nx, fl = [], [], [], []
    base = 0
    total = sum(seq_lens)
    for s, t_len in enumerate(seq_lens):
        for t in range(t_len // TM_FFN):
            r0 = base + t * TM_FFN
            first = t == 0
            last = t == t_len // TM_FFN - 1
            pg.append(max(r0 // HALO - 1, 0))
            pm.append((s * CHUNK + CHUNK - HALO) // HALO)
            nx.append(min((r0 + TM_FFN) // HALO, total // HALO - 1))
            fl.append((1 if first else 0) | (2 if last else 0))
        base += t_len
    return tuple(jnp.asarray(np.asarray(a, np.int32)) for a in (pg, pm, nx, fl))


def _pad_cols(w, n):
    return jnp.pad(w, ((0, 0), (0, n - w.shape[1])))


def _encode_all(xs, meta_tokens, emb_ln_g, emb_ln_b, w_in, attn_rpb, rwkv_mu, rwkv_w0, rwkv_w2,
                rwkv_a0, rwkv_a2, rwkv_g2, rwkv_k_k, rwkv_k_a, rwkv_r_k, rwkv_lnx_g, rwkv_lnx_b,
                w_out, ln1_g, ln1_b, ffn_w_in, ffn_conv_w, ffn_conv_b, ffn_w_out, ln2_g, ln2_b):
    seq_lens = []
    for x in xs:
        seq_lens += [x.shape[1]] * x.shape[0]
    nseq = len(seq_lens)
    x2d = [x.reshape(-1, D_MODEL) for x in xs]
    meta_blk = jnp.concatenate([jnp.zeros((PAD_ROWS, D_MODEL), F32), meta_tokens.astype(F32)], axis=0)
    x_m = jnp.tile(meta_blk, (nseq, 1))

    o1 = N_QKV
    o2 = o1 + 3 * D_RWKV
    o3 = o2 + R_DECAY
    o4 = o3 + R_ICLR
    wl = w_in[0]
    w_qkv = wl[:, :o1].astype(BF16)
    w_rw = jnp.concatenate(
        [wl[:, o1:o2], _pad_cols(wl[:, o2:o3], R_LOW), _pad_cols(wl[:, o3:o4], R_LOW), wl[:, o4:]],
        axis=1).astype(BF16)
    eg = emb_ln_g.reshape(1, D_MODEL)
    eb = emb_ln_b.reshape(1, D_MODEL)
    mu = rwkv_mu[0]
    o5 = 3 * D_RWKV
    mu_p = jnp.concatenate(
        [mu[:, :o5], _pad_cols(mu[:, o5:o5 + R_DECAY], R_LOW), _pad_cols(mu[:, o5 + R_DECAY:], R_LOW)],
        axis=1)
    pad_rows = lambda w: jnp.pad(w, ((0, R_LOW - w.shape[0]), (0, 0))).astype(BF16)
    row = lambda v: v.reshape(1, -1)

    def dir_params(d):
        return dict(mu=mu_p[d:d + 1], w0=row(rwkv_w0[0, d]), w2=pad_rows(rwkv_w2[0, d]),
                    a0=row(rwkv_a0[0, d]), a2=pad_rows(rwkv_a2[0, d]),
                    k_k=row(rwkv_k_k[0]), k_a=row(rwkv_k_a[0]), r_k=row(rwkv_r_k[0]),
                    g2=rwkv_g2[0].astype(BF16), lnx_g=row(rwkv_lnx_g[0]), lnx_b=row(rwkv_lnx_b[0]))

    w_out_bf = w_out[0].astype(BF16)
    ffn_w_in_bf = ffn_w_in[0].astype(BF16)
    ffn_w_out_bf = (0.5 * ffn_w_out[0]).astype(BF16)
    bias = _attn_bias_tiles(attn_rpb[0])

    g_qkv, g_h0 = _inproj(x2d, eg, eb, w_qkv, BF16, True, TM)
    g_prw = _inproj(x2d, eg, eb, w_rw, F32, False, TM)
    m_qkv, m_h0 = _inproj([x_m], eg, eb, w_qkv, BF16, True, x_m.shape[0])
    m_prw = _inproj([x_m], eg, eb, w_rw, F32, False, x_m.shape[0])

    g_oattn = _attention(g_qkv, m_qkv, bias, _attn_tables(seq_lens))
    m_oattn = _meta_attention(m_qkv)
    g_yb, m_yb = _rwkv_scan(g_prw, m_prw, dir_params(0), _rwkv_tables(seq_lens, False), False)
    g_orwkv, m_orwkv = _rwkv_scan(g_prw, m_prw, dir_params(1), _rwkv_tables(seq_lens, True), True,
                                  prev=(g_yb, m_yb))

    g1 = ln1_g.reshape(1, D_MODEL)
    b1 = ln1_b.reshape(1, D_MODEL)
    g_h1, g_h1b = _outproj(g_oattn, g_orwkv, g_h0, w_out_bf, g1, b1, TM)
    _, m_h1b = _outproj(m_oattn, m_orwkv, m_h0, w_out_bf, g1, b1, x_m.shape[0])

    out_rows = [x.shape[0] * x.shape[1] for x in xs]
    act = _ffn_in(g_h1b, m_h1b, ffn_w_in_bf, ffn_conv_w[0], ffn_conv_b.reshape(1, D_FF),
                  _ffn_tables(seq_lens))
    ys, row0 = [], 0
    for n_rows in out_rows:
        ys.append(_ffn_out(act, g_h1, ffn_w_out_bf, ln2_g.reshape(1, D_MODEL),
                           ln2_b.reshape(1, D_MODEL), row0, n_rows))
        row0 += n_rows
    return [y.reshape(x.shape) for y, x in zip(ys, xs)]


def kernel(x_prompt, x_sample, meta_tokens, emb_ln_g, emb_ln_b, w_in, attn_rpb, rwkv_mu, rwkv_w0, rwkv_w2, rwkv_a0, rwkv_a2, rwkv_g2, rwkv_k_k, rwkv_k_a, rwkv_r_k, rwkv_lnx_g, rwkv_lnx_b, w_out, ln1_g, ln1_b, ffn_w_in, ffn_conv_w, ffn_conv_b, ffn_w_out, ln2_g, ln2_b):
    y_prompt, y_sample = _encode_all(
        [x_prompt, x_sample], meta_tokens, emb_ln_g, emb_ln_b, w_in, attn_rpb, rwkv_mu, rwkv_w0,
        rwkv_w2, rwkv_a0, rwkv_a2, rwkv_g2, rwkv_k_k, rwkv_k_a, rwkv_r_k, rwkv_lnx_g, rwkv_lnx_b,
        w_out, ln1_g, ln1_b, ffn_w_in, ffn_conv_w, ffn_conv_b, ffn_w_out, ln2_g, ln2_b)
    return (y_prompt, y_sample)
```
